```python
import jax
import jax.numpy as jnp
from jax import lax
import numpy as np

D_MODEL = 2048
BATCH = 16
SEQ = 256
DEPTH = 2
DEC_BATCH = 2
DEC_SEQ = 4096
PAST_LEN = 512

GRID_W = 64
N_HEADS = 8
N_KV_HEADS = 2
HEAD_DIM = 128
ATTN_WIDTH = N_HEADS * HEAD_DIM
KV_WIDTH = N_KV_HEADS * HEAD_DIM
ROPE_THETA = 10000.0
Q_BLOCK = 128
CONV_WIDTH = D_MODEL // 4
CONV_KERNEL = 31
POOL_WINDOWS = (2, 4, 8, 16)
POOL_GROUPS = 4
POOL_WIDTH = D_MODEL // 4
POOL_GROUP_CH = POOL_WIDTH // POOL_GROUPS
POOL_OUT_CH = D_MODEL // POOL_GROUPS
SGU_GROUPS = 4
SGU_WIDTH = D_MODEL // 4
SGU_GROUP_CH = SGU_WIDTH // SGU_GROUPS
CHUNK = 128
N_BRANCHES = 4
N_EXPERTS = 32
TOP_K = 4
EXPERT_FF = D_MODEL
SWIGLU_LIMIT = 7.0
SWIGLU_ALPHA = 1.702
MOE_BLOCK = 128
NORM_EPS = 1e-6
DEEPNORM_ALPHA = (2 * DEPTH) ** 0.25
DEEPNORM_BETA = (8 * DEPTH) ** -0.25
IN_WIDTH = ATTN_WIDTH + 2 * KV_WIDTH + 2 * CONV_WIDTH + POOL_WIDTH + 2 * SGU_WIDTH + N_BRANCHES * D_MODEL

kernel_name = 'hybrid_diffusion_gated_branches_moe_step'


def split_points():
    widths = (ATTN_WIDTH, KV_WIDTH, KV_WIDTH, CONV_WIDTH, CONV_WIDTH, POOL_WIDTH, SGU_WIDTH, SGU_WIDTH)
    pts, acc = [], 0
    for w in widths:
        acc += w
        pts.append(acc)
    return pts


def layer_norm(x, g, b):
    xf = x.astype(jnp.float32)
    mu = jnp.mean(xf, axis=-1, keepdims=True)
    xc = xf - mu
    var = jnp.mean(xc * xc, axis=-1, keepdims=True)
    y = xc * lax.rsqrt(var + NORM_EPS) * g.astype(jnp.float32) + b.astype(jnp.float32)
    return y.astype(x.dtype)


def rms_norm(x, g):
    xf = x.astype(jnp.float32)
    y = xf * lax.rsqrt(jnp.mean(xf * xf, axis=-1, keepdims=True) + NORM_EPS) * g.astype(jnp.float32)
    return y.astype(x.dtype)


def axial_rope(x):
    b, t, h, dh = x.shape
    rows = t // GRID_W
    row = jnp.broadcast_to(jnp.arange(rows, dtype=jnp.float32)[:, None], (rows, GRID_W)).reshape(t)
    col = jnp.broadcast_to(jnp.arange(GRID_W, dtype=jnp.float32)[None, :], (rows, GRID_W)).reshape(t)
    half = dh // 2
    inv_freq = ROPE_THETA ** (-jnp.arange(0, half, 2, dtype=jnp.float32) / half)
    ang = jnp.concatenate([row[:, None] * inv_freq, col[:, None] * inv_freq], axis=-1)
    cos = jnp.cos(ang)[None, :, None, :]
    sin = jnp.sin(ang)[None, :, None, :]
    xp = x.astype(jnp.float32).reshape(b, t, h, dh // 2, 2)
    x0, x1 = xp[..., 0], xp[..., 1]
    out = jnp.stack([x0 * cos - x1 * sin, x0 * sin + x1 * cos], axis=-1)
    return out.reshape(b, t, h, dh).astype(x.dtype)


def block_attention(q, k, v):
    b, t, h, dh = q.shape
    n_kv = k.shape[2]
    grp = h // n_kv
    nb = t // Q_BLOCK
    scale = dh ** -0.5
    qb = q.reshape(b, nb, Q_BLOCK, n_kv, grp, dh).transpose(1, 0, 2, 3, 4, 5)

    def one_block(q_blk):
        s = jnp.einsum('bqkgd,bskd->bkgqs', q_blk, k, preferred_element_type=jnp.float32) * scale
        p = jax.nn.softmax(s, axis=-1)
        return jnp.einsum('bkgqs,bskd->bqkgd', p.astype(v.dtype), v)

    o = lax.map(one_block, qb)
    return o.transpose(1, 0, 2, 3, 4, 5).reshape(b, t, h * dh)


def attention_branch(q, k, v, q_norm, k_norm, w_o, ctx_k, ctx_v):
    b, t, _ = q.shape
    q = rms_norm(q.reshape(b, t, N_HEADS, HEAD_DIM), q_norm)
    k = rms_norm(k.reshape(b, t, N_KV_HEADS, HEAD_DIM), k_norm)
    v = v.reshape(b, t, N_KV_HEADS, HEAD_DIM)
    if ctx_k is None:
        o = block_attention(q, k, v)
    else:
        keys = jnp.concatenate([ctx_k, axial_rope(k)], axis=1)
        vals = jnp.concatenate([ctx_v, v], axis=1)
        o = block_attention(axial_rope(q), keys, vals)
    return o @ w_o, k, v


def conformer_conv(a, g, conv_w, conv_b, ln_g, ln_b, w_pw):
    u = a * jax.nn.sigmoid(g)
    y = lax.conv_general_dilated(
        u, conv_w[:, None, :], window_strides=(1,),
        padding=[(CONV_KERNEL // 2, CONV_KERNEL // 2)],
        dimension_numbers=('NWC', 'WIO', 'NWC'),
        feature_group_count=CONV_WIDTH) + conv_b
    y = jax.nn.silu(layer_norm(y, ln_g, ln_b))
    return y @ w_pw


def multiscale_pool(z, w_pool, pool_scale):
    b, t, _ = z.shape
    zf = z.astype(jnp.float32).reshape(b, t, POOL_GROUPS, POOL_GROUP_CH)
    csum = jnp.concatenate([jnp.zeros((b, 1, POOL_GROUPS, POOL_GROUP_CH), jnp.float32),
                            jnp.cumsum(zf, axis=1)], axis=1)
    win = jnp.array(POOL_WINDOWS, jnp.int32)[None, :]
    pos = jnp.arange(t, dtype=jnp.int32)[:, None]
    lo = jnp.clip(pos - win // 2, 0, t)
    hi = jnp.clip(pos - win // 2 + win, 0, t)
    gidx = jnp.arange(POOL_GROUPS)[None, :]
    mean = (csum[:, hi, gidx, :] - csum[:, lo, gidx, :]) / (hi - lo).astype(jnp.float32)[None, :, :, None]
    pooled = (mean - zf).astype(z.dtype)
    out = jnp.einsum('btgc,gce->btge', pooled, w_pool).reshape(b, t, D_MODEL)
    return out * pool_scale


def spatial_gating(u, v, ln_g, ln_b, sgu_w, sgu_b, w_proj):
    b, t, _ = u.shape
    n = t // CHUNK
    vn = layer_norm(v, ln_g, ln_b).reshape(b, n, CHUNK, SGU_GROUPS, SGU_GROUP_CH)
    mixed = jnp.einsum('gpq,bnqgc->bnpgc', sgu_w, vn) + sgu_b.T[None, None, :, :, None]
    return (u * mixed.reshape(b, t, SGU_WIDTH)) @ w_proj


def moe_ffn(h, w_router, b_router, w_gate, b_gate, w_up, b_up, w_down, b_down):
    b, t, d = h.shape
    x = h.reshape(b * t, d)
    n = b * t
    nk = n * TOP_K
    logits = (x @ w_router).astype(jnp.float32) + b_router.astype(jnp.float32)
    top_logit, top_idx = lax.top_k(logits, TOP_K)
    comb = jax.nn.softmax(top_logit, axis=-1).astype(h.dtype)
    flat_e = top_idx.reshape(-1)
    order = jnp.argsort(flat_e, stable=True)
    sorted_e = flat_e[order]
    sorted_tok = order // TOP_K
    sorted_w = comb.reshape(-1)[order]
    counts = jnp.bincount(flat_e, length=N_EXPERTS)
    padded = (counts + MOE_BLOCK - 1) // MOE_BLOCK * MOE_BLOCK
    start = jnp.cumsum(counts) - counts
    pend = jnp.cumsum(padded)
    pstart = pend - padded
    dest = pstart[sorted_e] + jnp.arange(nk, dtype=jnp.int32) - start[sorted_e]
    n_blocks = -(-nk // MOE_BLOCK) + N_EXPERTS
    slot_tok = jnp.full((n_blocks * MOE_BLOCK,), n, jnp.int32).at[dest].set(sorted_tok)
    x_ext = jnp.concatenate([x, jnp.zeros((1, d), x.dtype)], axis=0)
    x_blocks = x_ext[slot_tok].reshape(n_blocks, MOE_BLOCK, d)
    block_start = jnp.arange(n_blocks, dtype=jnp.int32) * MOE_BLOCK
    block_e = jnp.clip(jnp.searchsorted(pend, block_start, side='right'), 0, N_EXPERTS - 1)

    def expert_block(args):
        xb, e = args
        gt = jnp.minimum(xb @ w_gate[e] + b_gate[e], SWIGLU_LIMIT)
        up = jnp.clip(xb @ w_up[e] + b_up[e], -SWIGLU_LIMIT, SWIGLU_LIMIT)
        act = gt * jax.nn.sigmoid(SWIGLU_ALPHA * gt) * (up + 1)
        return act @ w_down[e] + b_down[e]

    y_blocks = lax.map(expert_block, (x_blocks, block_e)).reshape(n_blocks * MOE_BLOCK, d)
    y_sorted = y_blocks[dest] * sorted_w[:, None]
    y = jax.ops.segment_sum(y_sorted, sorted_tok, num_segments=n)
    return y.reshape(b, t, d)


def trunk_layer(x, mod, lp, ctx_k, ctx_v):
    bsz, t, _ = x.shape
    shift1, scale1, gate1, shift2, scale2, gate2 = jnp.split(mod[:, None, :], 6, axis=-1)
    h = x * (1 + scale1) + shift1
    proj = h @ lp['w_in']
    q, k, v, conv_a, conv_g, pool_in, sgu_u, sgu_v, gates = jnp.split(proj, split_points(), axis=-1)
    y_a, k_heads, v_heads = attention_branch(q, k, v, lp['q_norm'], lp['k_norm'], lp['w_attn_o'], ctx_k, ctx_v)
    y_b = conformer_conv(conv_a, conv_g, lp['conv_w'], lp['conv_b'], lp['conv_ln_g'], lp['conv_ln_b'], lp['w_conv_o'])
    y_c = multiscale_pool(pool_in, lp['w_pool'], lp['pool_scale'])
    y_d = spatial_gating(sgu_u, sgu_v, lp['sgu_ln_g'], lp['sgu_ln_b'], lp['sgu_w'], lp['sgu_b'], lp['w_sgu_o'])
    branches = jnp.stack([y_a, y_b, y_c, y_d], axis=2)
    g = jax.nn.sigmoid(gates.reshape(bsz, t, N_BRANCHES, D_MODEL))
    merged = jnp.sum(g * branches, axis=2)
    x = layer_norm(DEEPNORM_ALPHA * x + gate1 * (merged @ lp['w_out']), lp['ln1_g'], lp['ln1_b'])
    h2 = x * (1 + scale2) + shift2
    ff = moe_ffn(h2, lp['w_router'], lp['b_router'], lp['w_gate'], lp['b_gate'],
                 lp['w_up'], lp['b_up'], lp['w_down'], lp['b_down'])
    x = layer_norm(DEEPNORM_ALPHA * x + gate2 * ff, lp['ln2_g'], lp['ln2_b'])
    return x, k_heads, v_heads


def setup_inputs(seed: int = 0) -> dict:
    key = jax.random.key(seed)
    keys = list(jax.random.split(key, 48))

    def nrm(i, shape, std):
        return jax.random.normal(keys[i], shape, jnp.float32) * std

    L, D, E, F = DEPTH, D_MODEL, N_EXPERTS, EXPERT_FF
    return {
        'x_prompt': nrm(0, (BATCH, SEQ, D), 1.0),
        'x_sample': nrm(1, (DEC_BATCH, DEC_SEQ, D), 1.0),
        'cache_k': nrm(2, (DEC_BATCH, L, PAST_LEN, N_KV_HEADS, HEAD_DIM), 1.0),
        'cache_v': nrm(3, (DEC_BATCH, L, PAST_LEN, N_KV_HEADS, HEAD_DIM), 1.0),
        'c': nrm(4, (DEC_BATCH, D), 1.0),
        'c_ctx': nrm(5, (D,), 1.0),
        'w_mod': nrm(6, (L, D, 6 * D), 0.5 * D ** -0.5),
        'b_mod': nrm(7, (L, 6 * D), 0.02),
        'w_in': nrm(8, (L, D, IN_WIDTH), D ** -0.5),
        'q_norm': 1.0 + nrm(9, (L, HEAD_DIM), 0.02),
        'k_norm': 1.0 + nrm(10, (L, HEAD_DIM), 0.02),
        'w_attn_o': nrm(11, (L, ATTN_WIDTH, D), ATTN_WIDTH ** -0.5),
        'conv_w': nrm(12, (L, CONV_KERNEL, CONV_WIDTH), CONV_KERNEL ** -0.5),
        'conv_b': nrm(13, (L, CONV_WIDTH), 0.02),
        'conv_ln_g': 1.0 + nrm(14, (L, CONV_WIDTH), 0.02),
        'conv_ln_b': nrm(15, (L, CONV_WIDTH), 0.02),
        'w_conv_o': nrm(16, (L, CONV_WIDTH, D), CONV_WIDTH ** -0.5),
        'w_pool': nrm(17, (L, POOL_GROUPS, POOL_GROUP_CH, POOL_OUT_CH), POOL_GROUP_CH ** -0.5),
        'pool_scale': 1.0 + nrm(18, (L, D), 0.02),
        'sgu_ln_g': 1.0 + nrm(19, (L, SGU_WIDTH), 0.02),
        'sgu_ln_b': nrm(20, (L, SGU_WIDTH), 0.02),
        'sgu_w': nrm(21, (L, SGU_GROUPS, CHUNK, CHUNK), CHUNK ** -0.5),
        'sgu_b': 1.0 + nrm(22, (L, SGU_GROUPS, CHUNK), 0.02),
        'w_sgu_o': nrm(23, (L, SGU_WIDTH, D), SGU_WIDTH ** -0.5),
        'w_out': nrm(24, (L, D, D), DEEPNORM_BETA * D ** -0.5),
        'ln1_g': 1.0 + nrm(25, (L, D), 0.02),
        'ln1_b': nrm(26, (L, D), 0.02),
        'ln2_g': 1.0 + nrm(27, (L, D), 0.02),
        'ln2_b': nrm(28, (L, D), 0.02),
        'w_router': nrm(29, (L, D, E), D ** -0.5),
        'b_router': nrm(30, (L, E), 0.01),
        'w_gate': nrm(31, (L, E, D, F), D ** -0.5),
        'b_gate': nrm(32, (L, E, F), 0.02),
        'w_up': nrm(33, (L, E, D, F), D ** -0.5),
        'b_up': nrm(34, (L, E, F), 0.02),
        'w_down': nrm(35, (L, E, F, D), DEEPNORM_BETA * F ** -0.5),
        'b_down': nrm(36, (L, E, D), 0.02),
    }


def reference(x_prompt, x_sample, cache_k, cache_v, c, c_ctx, w_mod, b_mod, w_in, q_norm, k_norm,
              w_attn_o, conv_w, conv_b, conv_ln_g, conv_ln_b, w_conv_o, w_pool, pool_scale,
              sgu_ln_g, sgu_ln_b, sgu_w, sgu_b, w_sgu_o, w_out, ln1_g, ln1_b, ln2_g, ln2_b,
              w_router, b_router, w_gate, b_gate, w_up, b_up, w_down, b_down):
    y_prompt = x_prompt
    y_sample = x_sample
    cond_ctx = jax.nn.silu(c_ctx)[None, :]
    cond_lat = jax.nn.silu(c)
    ks, vs = [], []
    for l in range(DEPTH):
        lp = {
            'w_in': w_in[l], 'q_norm': q_norm[l], 'k_norm': k_norm[l], 'w_attn_o': w_attn_o[l],
            'conv_w': conv_w[l], 'conv_b': conv_b[l], 'conv_ln_g': conv_ln_g[l], 'conv_ln_b': conv_ln_b[l],
            'w_conv_o': w_conv_o[l], 'w_pool': w_pool[l], 'pool_scale': pool_scale[l],
            'sgu_ln_g': sgu_ln_g[l], 'sgu_ln_b': sgu_ln_b[l], 'sgu_w': sgu_w[l], 'sgu_b': sgu_b[l],
            'w_sgu_o': w_sgu_o[l], 'w_out': w_out[l], 'ln1_g': ln1_g[l], 'ln1_b': ln1_b[l],
            'ln2_g': ln2_g[l], 'ln2_b': ln2_b[l], 'w_router': w_router[l], 'b_router': b_router[l],
            'w_gate': w_gate[l], 'b_gate': b_gate[l], 'w_up': w_up[l], 'b_up': b_up[l],
            'w_down': w_down[l], 'b_down': b_down[l],
        }
        mod_ctx = cond_ctx @ w_mod[l] + b_mod[l]
        mod_lat = cond_lat @ w_mod[l] + b_mod[l]
        y_prompt, k_l, v_l = trunk_layer(y_prompt, mod_ctx, lp, None, None)
        y_sample, _, _ = trunk_layer(y_sample, mod_lat, lp, cache_k[:, l], cache_v[:, l])
        ks.append(k_l)
        vs.append(v_l)
    state_k = jnp.stack(ks, axis=1)
    state_v = jnp.stack(vs, axis=1)
    return (y_prompt, y_sample, state_k, state_v)
```

```python
import functools
import math

import jax
import jax.numpy as jnp
from jax import lax
from jax.experimental import pallas as pl
from jax.experimental.pallas import tpu as pltpu

F32 = jnp.float32
BF16 = jnp.bfloat16

N_HEADS = 8
N_KV_HEADS = 2
HEAD_DIM = 128
GRID_W = 64
ROPE_THETA = 10000.0
CONV_KERNEL = 31
POOL_WINDOWS = (2, 4, 8, 16)
POOL_GROUPS = 4
SGU_GROUPS = 4
CHUNK = 128
N_BRANCHES = 4
TOP_K = 4
SWIGLU_LIMIT = 7.0
SWIGLU_ALPHA = 1.702
NORM_EPS = 1e-6
N_MOD = 6

ATTN_WIDTH = N_HEADS * HEAD_DIM
KV_WIDTH = N_KV_HEADS * HEAD_DIM
GROUP = N_HEADS // N_KV_HEADS

LANES = 128
SUBLANES = 8
VMEM_LIMIT_BYTES = 56 * 1024 * 1024

ROW_TILE = 256
HALO = 16
IN_PROJ_ROWS = 512
IN_PROJ_COLS = 1024
MOD_COLS = 1024
MOE_SUPER = 1024
MOE_SUB = 256
MOE_FF_TILE = 256


def _dot(a, b):
    return jnp.dot(a, b, preferred_element_type=F32)


def _dot_nt(a, b):
    return lax.dot_general(a, b, (((1,), (1,)), ((), ())), preferred_element_type=F32)


def _layer_norm(x, g, b):
    mu = jnp.mean(x, axis=-1, keepdims=True)
    xc = x - mu
    var = jnp.mean(xc * xc, axis=-1, keepdims=True)
    return xc * lax.rsqrt(var + NORM_EPS) * g + b


def _sigmoid(x):
    return 1.0 / (1.0 + jnp.exp(-x))


def _params(sem, vmem=VMEM_LIMIT_BYTES):
    return pltpu.CompilerParams(dimension_semantics=sem, vmem_limit_bytes=vmem)


class _Dims:
    def __init__(self, bp, tp, bs, ts, d, past, depth, n_exp, ff):
        self.bp, self.tp, self.bs, self.ts = bp, tp, bs, ts
        self.d, self.past, self.depth, self.n_exp, self.ff = d, past, depth, n_exp, ff
        self.np_ = bp * tp
        self.ns = bs * ts
        self.n = self.np_ + self.ns
        self.cw = d // 4
        self.in_w = ATTN_WIDTH + 2 * KV_WIDTH + 5 * self.cw + N_BRANCHES * d
        self.off_k = ATTN_WIDTH
        self.off_v = ATTN_WIDTH + KV_WIDTH
        self.off_a = ATTN_WIDTH + 2 * KV_WIDTH
        self.off_gates = self.off_a + 5 * self.cw
        assert self.cw == POOL_GROUPS * LANES == SGU_GROUPS * CHUNK
        assert self.off_a % self.cw == 0 and self.off_gates % d == 0
        assert tp % ROW_TILE == 0 and ts % ROW_TILE == 0 and self.np_ % ts == 0
        assert ts % GRID_W == 0

    def group_of_tile(self, i, tm):
        npt = self.np_ // tm
        return jnp.where(i < npt, 0, 1 + (i - npt) // (self.ts // tm))

    def seq_tile_pos(self, i, tm):
        npt = self.np_ // tm
        is_p = i < npt
        pos = jnp.where(is_p, i % (self.tp // tm), (i - npt) % (self.ts // tm))
        cnt = jnp.where(is_p, self.tp // tm, self.ts // tm)
        return pos, cnt


def _mod_kernel(c_ref, w_ref, b_ref, o_ref):
    c = c_ref[...]
    s = (c * _sigmoid(c)).astype(BF16)
    o_ref[...] = _dot(s, w_ref[...].astype(BF16)) + b_ref[...]


def _modulation(cond8, w_mod, b_mod):
    depth, d, width = w_mod.shape
    tn = MOD_COLS
    return pl.pallas_call(
        _mod_kernel,
        out_shape=jax.ShapeDtypeStruct((depth, SUBLANES, width), F32),
        grid=(depth, width // tn),
        in_specs=[
            pl.BlockSpec((SUBLANES, d), lambda l, j: (0, 0)),
            pl.BlockSpec((None, d, tn), lambda l, j: (l, 0, j)),
            pl.BlockSpec((None, 1, tn), lambda l, j: (l, 0, j)),
        ],
        out_specs=pl.BlockSpec((None, SUBLANES, tn), lambda l, j: (l, 0, j)),
        compiler_params=_params(("arbitrary", "arbitrary")),
        name="modulation",
    )(cond8, w_mod, b_mod.reshape(depth, 1, width))


def _in_proj_kernel(x_ref, mod_ref, w_ref, o_ref, wbf_ref):
    @pl.when(pl.program_id(1) == 0)
    def _():
        wbf_ref[...] = w_ref[...].astype(BF16)

    shift = mod_ref[0:1, :]
    scale = mod_ref[1:2, :]
    h = (x_ref[...] * (1.0 + scale) + shift).astype(BF16)
    o_ref[...] = _dot(h, wbf_ref[...]).astype(o_ref.dtype)


def _in_proj(dm, layer, x, mod, w_in):
    tm, tn = IN_PROJ_ROWS, IN_PROJ_COLS
    d = dm.d
    return pl.pallas_call(
        _in_proj_kernel,
        out_shape=jax.ShapeDtypeStruct((dm.n, dm.in_w), BF16),
        grid=(dm.in_w // tn, dm.n // tm),
        in_specs=[
            pl.BlockSpec((tm, d), lambda j, i: (i, 0)),
            pl.BlockSpec((None, None, N_MOD, d), lambda j, i: (layer, dm.group_of_tile(i, tm), 0, 0)),
            pl.BlockSpec((None, d, tn), lambda j, i: (layer, 0, j)),
        ],
        out_specs=pl.BlockSpec((tm, tn), lambda j, i: (i, j)),
        scratch_shapes=[pltpu.VMEM((d, tn), BF16)],
        compiler_params=_params(("arbitrary", "arbitrary")),
        name="in_proj",
    )(x, mod, w_in)


def _rope(x, cos, sin_signed):
    lane = lax.broadcasted_iota(jnp.int32, x.shape, 1)
    nxt = pltpu.roll(x, HEAD_DIM - 1, 1)
    prv = pltpu.roll(x, 1, 1)
    partner = jnp.where((lane & 1) == 0, nxt, prv)
    return x * cos + partner * sin_signed


def _rms(x, g):
    return x * lax.rsqrt(jnp.mean(x * x, axis=-1, keepdims=True) + NORM_EPS) * g


def _qk_prep_kernel(q_ref, k_ref, v_ref, cos_ref, sin_ref, qg_ref, kg_ref,
                    qo_ref, ko_ref, kn_ref, vo_ref):
    cos = cos_ref[...]
    sin = sin_ref[...]
    qg = qg_ref[...]
    kg = kg_ref[...]
    scale = HEAD_DIM ** -0.5
    for h in range(N_HEADS):
        cols = slice(h * HEAD_DIM, (h + 1) * HEAD_DIM)
        qn = _rms(q_ref[:, cols].astype(F32), qg)
        qo_ref[:, cols] = (_rope(qn, cos, sin) * scale).astype(qo_ref.dtype)
    for h in range(N_KV_HEADS):
        cols = slice(h * HEAD_DIM, (h + 1) * HEAD_DIM)
        kn = _rms(k_ref[:, cols].astype(F32), kg)
        kn_ref[:, cols] = kn
        ko_ref[:, cols] = _rope(kn, cos, sin).astype(ko_ref.dtype)
    vo_ref[...] = v_ref[...].astype(F32)


def _rope_tables(dm, tm):
    t = dm.ts
    rows = t // GRID_W
    row = jnp.broadcast_to(jnp.arange(rows, dtype=F32)[:, None], (rows, GRID_W)).reshape(t)
    col = jnp.broadcast_to(jnp.arange(GRID_W, dtype=F32)[None, :], (rows, GRID_W)).reshape(t)
    half = HEAD_DIM // 2
    inv_freq = ROPE_THETA ** (-jnp.arange(0, half, 2, dtype=F32) / half)
    ang = jnp.concatenate([row[:, None] * inv_freq, col[:, None] * inv_freq], axis=-1)
    cos = jnp.repeat(jnp.cos(ang), 2, axis=-1)
    sin = jnp.repeat(jnp.sin(ang), 2, axis=-1)
    sign = jnp.tile(jnp.array([-1.0, 1.0], F32), half)
    cos = jnp.concatenate([jnp.ones((tm, HEAD_DIM), F32), cos], axis=0)
    sin = jnp.concatenate([jnp.zeros((tm, HEAD_DIM), F32), sin * sign], axis=0)
    return cos, sin


def _qk_prep(dm, layer, proj, cos, sin, q_norm, k_norm):
    tm = ROW_TILE
    npt = dm.np_ // tm
    tps = dm.ts // tm

    def tab(i):
        return (jnp.where(i < npt, 0, 1 + (i - npt) % tps), 0)

    return pl.pallas_call(
        _qk_prep_kernel,
        out_shape=(
            jax.ShapeDtypeStruct((dm.n, ATTN_WIDTH), BF16),
            jax.ShapeDtypeStruct((dm.n, KV_WIDTH), BF16),
            jax.ShapeDtypeStruct((dm.n, KV_WIDTH), F32),
            jax.ShapeDtypeStruct((dm.n, KV_WIDTH), F32),
        ),
        grid=(dm.n // tm,),
        in_specs=[
            pl.BlockSpec((tm, ATTN_WIDTH), lambda i: (i, 0)),
            pl.BlockSpec((tm, KV_WIDTH), lambda i: (i, dm.off_k // KV_WIDTH)),
            pl.BlockSpec((tm, KV_WIDTH), lambda i: (i, dm.off_v // KV_WIDTH)),
            pl.BlockSpec((tm, HEAD_DIM), tab),
            pl.BlockSpec((tm, HEAD_DIM), tab),
            pl.BlockSpec((None, 1, HEAD_DIM), lambda i: (layer, 0, 0)),
            pl.BlockSpec((None, 1, HEAD_DIM), lambda i: (layer, 0, 0)),
        ],
        out_specs=(
            pl.BlockSpec((tm, ATTN_WIDTH), lambda i: (i, 0)),
            pl.BlockSpec((tm, KV_WIDTH), lambda i: (i, 0)),
            pl.BlockSpec((tm, KV_WIDTH), lambda i: (i, 0)),
            pl.BlockSpec((tm, KV_WIDTH), lambda i: (i, 0)),
        ),
        compiler_params=_params(("arbitrary",)),
        name="qk_prep",
    )(proj, proj, proj, cos, sin, q_norm, k_norm)


def _ctx_attn_kernel(q_ref, k_ref, v_ref, o_ref):
    for h in range(N_HEADS):
        kv = h // GROUP
        cols = slice(h * HEAD_DIM, (h + 1) * HEAD_DIM)
        kcols = slice(kv * HEAD_DIM, (kv + 1) * HEAD_DIM)
        s = _dot_nt(q_ref[:, cols], k_ref[:, kcols])
        m = jnp.max(s, axis=-1, keepdims=True)
        p = jnp.exp(s - m)
        l = jnp.sum(p, axis=-1, keepdims=True)
        o = _dot(p.astype(BF16), v_ref[:, kcols]) / l
        o_ref[:, cols] = o.astype(o_ref.dtype)


def _ctx_attn(dm, q, k, proj):
    t = dm.tp
    return pl.pallas_call(
        _ctx_attn_kernel,
        out_shape=jax.ShapeDtypeStruct((dm.np_, ATTN_WIDTH), BF16),
        grid=(dm.bp,),
        in_specs=[
            pl.BlockSpec((t, ATTN_WIDTH), lambda b: (b, 0)),
            pl.BlockSpec((t, KV_WIDTH), lambda b: (b, 0)),
            pl.BlockSpec((t, KV_WIDTH), lambda b: (b, dm.off_v // KV_WIDTH)),
        ],
        out_specs=pl.BlockSpec((t, ATTN_WIDTH), lambda b: (b, 0)),
        compiler_params=_params(("arbitrary",)),
        name="ctx_attn",
    )(q, k, proj)


def _lat_attn_kernel(q_ref, k_ref, v_ref, ck_ref, cv_ref, o_ref):
    ck = ck_ref[...].astype(BF16)
    cv = cv_ref[...].astype(BF16)
    for h in range(N_HEADS):
        kv = h // GROUP
        cols = slice(h * HEAD_DIM, (h + 1) * HEAD_DIM)
        kcols = slice(kv * HEAD_DIM, (kv + 1) * HEAD_DIM)
        qh = q_ref[:, cols]
        s_ctx = _dot_nt(qh, ck[:, kcols])
        s_lat = _dot_nt(qh, k_ref[:, kcols])
        m = jnp.maximum(jnp.max(s_ctx, axis=-1, keepdims=True),
                        jnp.max(s_lat, axis=-1, keepdims=True))
        p_ctx = jnp.exp(s_ctx - m)
        p_lat = jnp.exp(s_lat - m)
        l = jnp.sum(p_ctx, axis=-1, keepdims=True) + jnp.sum(p_lat, axis=-1, keepdims=True)
        o = _dot(p_ctx.astype(BF16), cv[:, kcols]) + _dot(p_lat.astype(BF16), v_ref[:, kcols])
        o_ref[:, cols] = (o / l).astype(o_ref.dtype)


def _lat_attn(dm, layer, q, k, proj, cache_k, cache_v):
    tq = ROW_TILE
    ts = dm.ts
    qb = ts // tq
    return pl.pallas_call(
        _lat_attn_kernel,
        out_shape=jax.ShapeDtypeStruct((dm.ns, ATTN_WIDTH), BF16),
        grid=(dm.bs, qb),
        in_specs=[
            pl.BlockSpec((tq, ATTN_WIDTH), lambda b, i: (dm.np_ // tq + b * qb + i, 0)),
            pl.BlockSpec((ts, KV_WIDTH), lambda b, i: (dm.np_ // ts + b, 0)),
            pl.BlockSpec((ts, KV_WIDTH), lambda b, i: (dm.np_ // ts + b, dm.off_v // KV_WIDTH)),
            pl.BlockSpec((None, None, dm.past, KV_WIDTH), lambda b, i: (b, layer, 0, 0)),
            pl.BlockSpec((None, None, dm.past, KV_WIDTH), lambda b, i: (b, layer, 0, 0)),
        ],
        out_specs=pl.BlockSpec((tq, ATTN_WIDTH), lambda b, i: (b * qb + i, 0)),
        compiler_params=_params(("arbitrary", "arbitrary")),
        name="lat_attn",
    )(q, k, proj, cache_k, cache_v)


def _local_kernel(dm, a_ref, ap_ref, an_ref, g_ref, gp_ref, gn_ref, z_ref, zp_ref, zn_ref,
                  u_ref, v_ref, cw_ref, cb_ref, clg_ref, clb_ref, slg_ref, slb_ref,
                  sw_ref, sbt_ref, conv_o, pool_o, sgu_o, ubuf, zbuf):
    t = ROW_TILE
    i = pl.program_id(0)
    pos, cnt = dm.seq_tile_pos(i, t)
    has_prev = (pos > 0).astype(F32)
    has_next = (pos < cnt - 1).astype(F32)

    def glu(a, g):
        return a[...].astype(F32) * _sigmoid(g[...].astype(F32))

    ubuf[0:HALO, :] = glu(ap_ref, gp_ref) * has_prev
    ubuf[HALO:HALO + t, :] = glu(a_ref, g_ref)
    ubuf[HALO + t:, :] = glu(an_ref, gn_ref) * has_next
    zbuf[0:HALO, :] = zp_ref[...].astype(F32) * has_prev
    zbuf[HALO:HALO + t, :] = z_ref[...].astype(F32)
    zbuf[HALO + t:, :] = zn_ref[...].astype(F32) * has_next

    half_rows = t // 2
    centre = CONV_KERNEL // 2
    for r in range(2):
        pieces = []
        for c in range(dm.cw // LANES):
            cols = slice(c * LANES, (c + 1) * LANES)
            acc = jnp.zeros((half_rows, LANES), F32)
            for k in range(CONV_KERNEL):
                start = HALO + r * half_rows + k - centre
                acc = acc + cw_ref[k:k + 1, cols] * ubuf[start:start + half_rows, cols]
            pieces.append(acc)
        y = jnp.concatenate(pieces, axis=1) + cb_ref[...]
        y = _layer_norm(y, clg_ref[...], clb_ref[...])
        conv_o[r * half_rows:(r + 1) * half_rows, :] = (y * _sigmoid(y)).astype(conv_o.dtype)

    seq_len = jnp.where(i < dm.np_ // t, dm.tp, dm.ts)
    tok = pos * t + lax.broadcasted_iota(jnp.int32, (t, LANES), 0)
    for gi, w in enumerate(POOL_WINDOWS):
        cols = slice(gi * LANES, (gi + 1) * LANES)
        acc = jnp.zeros((t, LANES), F32)
        for j in range(-(w // 2), w - w // 2):
            acc = acc + zbuf[HALO + j:HALO + j + t, cols]
        lo = jnp.maximum(tok - w // 2, 0)
        hi = jnp.minimum(tok - w // 2 + w, seq_len)
        mean = acc / (hi - lo).astype(F32)
        pool_o[:, cols] = (mean - zbuf[HALO:HALO + t, cols]).astype(pool_o.dtype)

    vn = _layer_norm(v_ref[...].astype(F32), slg_ref[...], slb_ref[...]).astype(BF16)
    for gi in range(SGU_GROUPS):
        cols = slice(gi * CHUNK, (gi + 1) * CHUNK)
        w = sw_ref[gi].astype(BF16)
        bias = sbt_ref[:, gi:gi + 1]
        for c in range(t // CHUNK):
            rows = slice(c * CHUNK, (c + 1) * CHUNK)
            mixed = _dot(w, vn[rows, cols]) + bias
            sgu_o[rows, cols] = (u_ref[rows, cols].astype(F32) * mixed).astype(sgu_o.dtype)


def _local_branches(dm, layer, proj, conv_w, conv_b, conv_ln_g, conv_ln_b,
                    sgu_ln_g, sgu_ln_b, sgu_w, sgu_bt):
    t = ROW_TILE
    cw = dm.cw
    hb = t // HALO
    last = dm.n // HALO - 1
    ca = dm.off_a // cw

    def cur(c):
        return pl.BlockSpec((t, cw), lambda i: (i, c))

    def prev(c):
        return pl.BlockSpec((HALO, cw), lambda i: (jnp.maximum(i * hb - 1, 0), c))

    def nxt(c):
        return pl.BlockSpec((HALO, cw), lambda i: (jnp.minimum((i + 1) * hb, last), c))

    def vec():
        return pl.BlockSpec((None, 1, cw), lambda i: (layer, 0, 0))

    out = jax.ShapeDtypeStruct((dm.n, cw), BF16)
    return pl.pallas_call(
        functools.partial(_local_kernel, dm),
        out_shape=(out, out, out),
        grid=(dm.n // t,),
        in_specs=[
            cur(ca), prev(ca), nxt(ca),
            cur(ca + 1), prev(ca + 1), nxt(ca + 1),
            cur(ca + 2), prev(ca + 2), nxt(ca + 2),
            cur(ca + 3), cur(ca + 4),
            pl.BlockSpec((None, CONV_KERNEL, cw), lambda i: (layer, 0, 0)),
            vec(), vec(), vec(), vec(), vec(),
            pl.BlockSpec((None, SGU_GROUPS, CHUNK, CHUNK), lambda i: (layer, 0, 0, 0)),
            pl.BlockSpec((None, CHUNK, SGU_GROUPS), lambda i: (layer, 0, 0)),
        ],
        out_specs=(pl.BlockSpec((t, cw), lambda i: (i, 0)),) * 3,
        scratch_shapes=[pltpu.VMEM((t + 2 * HALO, cw), F32), pltpu.VMEM((t + 2 * HALO, cw), F32)],
        compiler_params=_params(("arbitrary",)),
        name="local_branches",
    )(proj, proj, proj, proj, proj, proj, proj, proj, proj, proj, proj,
      conv_w, conv_b, conv_ln_g, conv_ln_b, sgu_ln_g, sgu_ln_b, sgu_w, sgu_bt)


def _merge_kernel(dm, o_ref, cv_ref, pl_ref, sg_ref, g0_ref, g1_ref, g2_ref, g3_ref,
                  wa_ref, wc_ref, wp_ref, ws_ref, ps_ref, m_ref):
    cw = dm.cw
    for c in range(dm.d // cw):
        cols = slice(c * cw, (c + 1) * cw)
        ya = _dot(o_ref[...], wa_ref[:, cols])
        yb = _dot(cv_ref[...], wc_ref[:, cols])
        yc = _dot(pl_ref[:, c * LANES:(c + 1) * LANES], wp_ref[c]) * ps_ref[:, cols]
        yd = _dot(sg_ref[...], ws_ref[:, cols])
        merged = (_sigmoid(g0_ref[:, cols].astype(F32)) * ya
                  + _sigmoid(g1_ref[:, cols].astype(F32)) * yb
                  + _sigmoid(g2_ref[:, cols].astype(F32)) * yc
                  + _sigmoid(g3_ref[:, cols].astype(F32)) * yd)
        m_ref[:, cols] = merged.astype(m_ref.dtype)


def _merge(dm, layer, attn_o, conv_h, pool_h, sgu_h, proj, wa, wc, wp, ws, pool_scale):
    tm = ROW_TILE
    d, cw = dm.d, dm.cw
    g0 = dm.off_gates // d
    assert d // cw == POOL_GROUPS and wp.shape[-1] == cw

    def gate(b):
        return pl.BlockSpec((tm, d), lambda i: (i, g0 + b))

    return pl.pallas_call(
        functools.partial(_merge_kernel, dm),
        out_shape=jax.ShapeDtypeStruct((dm.n, d), BF16),
        grid=(dm.n // tm,),
        in_specs=[
            pl.BlockSpec((tm, ATTN_WIDTH), lambda i: (i, 0)),
            pl.BlockSpec((tm, cw), lambda i: (i, 0)),
            pl.BlockSpec((tm, cw), lambda i: (i, 0)),
            pl.BlockSpec((tm, cw), lambda i: (i, 0)),
            gate(0), gate(1), gate(2), gate(3),
            pl.BlockSpec((None, ATTN_WIDTH, d), lambda i: (layer, 0, 0)),
            pl.BlockSpec((None, cw, d), lambda i: (layer, 0, 0)),
            pl.BlockSpec((None, POOL_GROUPS, LANES, cw), lambda i: (layer, 0, 0, 0)),
            pl.BlockSpec((None, cw, d), lambda i: (layer, 0, 0)),
            pl.BlockSpec((None, 1, d), lambda i: (layer, 0, 0)),
        ],
        out_specs=pl.BlockSpec((tm, d), lambda i: (i, 0)),
        compiler_params=_params(("arbitrary",)),
        name="merge",
    )(attn_o, conv_h, pool_h, sgu_h, proj, proj, proj, proj, wa, wc, wp, ws, pool_scale)


def _out_proj_kernel(alpha, m_ref, x_ref, mod_ref, w_ref, lg_ref, lb_ref, wr_ref, br_ref,
                     x1_ref, h2_ref, idx_ref, comb_ref):
    gate1 = mod_ref[2:3, :]
    shift2 = mod_ref[3:4, :]
    scale2 = mod_ref[4:5, :]
    y = _dot(m_ref[...], w_ref[...])
    x1 = _layer_norm(alpha * x_ref[...] + gate1 * y, lg_ref[...], lb_ref[...])
    x1_ref[...] = x1
    h2 = x1 * (1.0 + scale2) + shift2
    h2_hi = h2.astype(BF16)
    h2_ref[...] = h2_hi

    h2_lo = (h2 - h2_hi.astype(F32)).astype(BF16)
    wr = wr_ref[...]
    wr_hi = wr.astype(BF16)
    wr_lo = (wr - wr_hi.astype(F32)).astype(BF16)
    logits = _dot(h2_hi, wr_hi) + _dot(h2_hi, wr_lo) + _dot(h2_lo, wr_hi) + br_ref[...]

    lane = lax.broadcasted_iota(jnp.int32, logits.shape, 1)
    idx_out = jnp.zeros(logits.shape, jnp.int32)
    val_out = jnp.zeros(logits.shape, F32)
    top = None
    den = jnp.zeros((logits.shape[0], 1), F32)
    for k in range(TOP_K):
        m = jnp.max(logits, axis=-1, keepdims=True)
        ik = jnp.min(jnp.where(logits == m, lane, LANES), axis=-1, keepdims=True)
        if top is None:
            top = m
        e = jnp.exp(m - top)
        den = den + e
        idx_out = jnp.where(lane == k, ik, idx_out)
        val_out = jnp.where(lane == k, e, val_out)
        logits = jnp.where(lane == ik, -jnp.inf, logits)
    idx_ref[...] = idx_out
    comb_ref[...] = val_out / den


def _out_proj(dm, layer, merged, x, mod, w_out, ln_g, ln_b, w_router, b_router):
    tm = ROW_TILE
    d = dm.d
    alpha = (2 * dm.depth) ** 0.25

    def vec():
        return pl.BlockSpec((None, 1, d), lambda i: (layer, 0, 0))

    return pl.pallas_call(
        functools.partial(_out_proj_kernel, alpha),
        out_shape=(
            jax.ShapeDtypeStruct((dm.n, d), F32),
            jax.ShapeDtypeStruct((dm.n, d), BF16),
            jax.ShapeDtypeStruct((dm.n, LANES), jnp.int32),
            jax.ShapeDtypeStruct((dm.n, LANES), F32),
        ),
        grid=(dm.n // tm,),
        in_specs=[
            pl.BlockSpec((tm, d), lambda i: (i, 0)),
            pl.BlockSpec((tm, d), lambda i: (i, 0)),
            pl.BlockSpec((None, None, N_MOD, d), lambda i: (layer, dm.group_of_tile(i, tm), 0, 0)),
            pl.BlockSpec((None, d, d), lambda i: (layer, 0, 0)),
            vec(), vec(),
            pl.BlockSpec((None, d, LANES), lambda i: (layer, 0, 0)),
            pl.BlockSpec((None, 1, LANES), lambda i: (layer, 0, 0)),
        ],
        out_specs=(
            pl.BlockSpec((tm, d), lambda i: (i, 0)),
            pl.BlockSpec((tm, d), lambda i: (i, 0)),
            pl.BlockSpec((tm, LANES), lambda i: (i, 0)),
            pl.BlockSpec((tm, LANES), lambda i: (i, 0)),
        ),
        compiler_params=_params(("arbitrary",)),
        name="out_proj",
    )(merged, x, mod, w_out, ln_g, ln_b, w_router, b_router)


def _moe_kernel(be_ref, nv_ref, na_ref, x_ref, wg_ref, bg_ref, wu_ref, bu_ref, wd_ref, bd_ref,
                o_ref):
    s = pl.program_id(0)
    j = pl.program_id(1)
    nv = nv_ref[s]

    @pl.when(s < na_ref[0])
    def _():
        wg = wg_ref[...].astype(BF16)
        wu = wu_ref[...].astype(BF16)
        wd = wd_ref[...].astype(BF16)
        bg = bg_ref[...]
        bu = bu_ref[...]
        bd = bd_ref[...]
        for sub in range(MOE_SUPER // MOE_SUB):
            rows = slice(sub * MOE_SUB, (sub + 1) * MOE_SUB)
            live = sub * MOE_SUB < nv

            @pl.when(live)
            def _():
                xs = x_ref[rows, :]
                gt = jnp.minimum(_dot(xs, wg) + bg, SWIGLU_LIMIT)
                up = jnp.clip(_dot(xs, wu) + bu, -SWIGLU_LIMIT, SWIGLU_LIMIT)
                act = gt * _sigmoid(SWIGLU_ALPHA * gt) * (up + 1.0)
                c = _dot(act.astype(BF16), wd)

                @pl.when(j == 0)
                def _():
                    o_ref[rows, :] = c + bd

                @pl.when(j > 0)
                def _():
                    o_ref[rows, :] += c

            @pl.when(jnp.logical_and(jnp.logical_not(live), j == 0))
            def _():
                o_ref[rows, :] = jnp.zeros((MOE_SUB, o_ref.shape[1]), F32)


def _moe(dm, layer, x_sorted, blk_e, blk_nv, n_act, w_gate, b_gate, w_up, b_up, w_down, b_down):
    d, ff, tf, r = dm.d, dm.ff, MOE_FF_TILE, MOE_SUPER
    nsb = x_sorted.shape[0] // r
    nf = ff // tf

    def blk(s, na):
        return jnp.minimum(s, na[0] - 1)

    def col(s, j, na):
        return jnp.where(s < na[0], j, nf - 1)

    grid_spec = pltpu.PrefetchScalarGridSpec(
        num_scalar_prefetch=3,
        grid=(nsb, nf),
        in_specs=[
            pl.BlockSpec((r, d), lambda s, j, be, nv, na: (blk(s, na), 0)),
            pl.BlockSpec((None, None, d, tf), lambda s, j, be, nv, na: (layer, be[s], 0, col(s, j, na))),
            pl.BlockSpec((None, None, 1, tf), lambda s, j, be, nv, na: (layer, be[s], 0, col(s, j, na))),
            pl.BlockSpec((None, None, d, tf), lambda s, j, be, nv, na: (layer, be[s], 0, col(s, j, na))),
            pl.BlockSpec((None, None, 1, tf), lambda s, j, be, nv, na: (layer, be[s], 0, col(s, j, na))),
            pl.BlockSpec((None, None, tf, d), lambda s, j, be, nv, na: (layer, be[s], col(s, j, na), 0)),
            pl.BlockSpec((None, None, 1, d), lambda s, j, be, nv, na: (layer, be[s], 0, 0)),
        ],
        out_specs=pl.BlockSpec((r, d), lambda s, j, be, nv, na: (blk(s, na), 0)),
    )
    return pl.pallas_call(
        _moe_kernel,
        out_shape=jax.ShapeDtypeStruct((nsb * r, d), F32),
        grid_spec=grid_spec,
        compiler_params=_params(("arbitrary", "arbitrary")),
        name="moe_ffn",
    )(blk_e, blk_nv, n_act, x_sorted, w_gate, b_gate, w_up, b_up, w_down, b_down)


def _route(dm, top_idx, n_super):
    e, r = dm.n_exp, MOE_SUPER
    flat_e = top_idx.reshape(-1)
    onehot = (flat_e[:, None] == jnp.arange(e, dtype=jnp.int32)[None, :]).astype(jnp.int32)
    rank = jnp.take_along_axis(jnp.cumsum(onehot, axis=0), flat_e[:, None], axis=1)[:, 0] - 1
    counts = jnp.sum(onehot, axis=0)
    nsb_e = (counts + r - 1) // r
    sb_end = jnp.cumsum(nsb_e)
    sb_start = sb_end - nsb_e
    dest = sb_start[flat_e] * r + rank
    n_act = sb_end[-1]
    sidx = jnp.arange(n_super, dtype=jnp.int32)
    blk_e = jnp.clip(jnp.searchsorted(sb_end, jnp.minimum(sidx, n_act - 1), side='right'), 0, e - 1)
    blk_e = blk_e.astype(jnp.int32)
    blk_nv = jnp.clip(counts[blk_e] - (sidx - sb_start[blk_e]) * r, 0, r)
    blk_nv = jnp.where(sidx < n_act, blk_nv, 0).astype(jnp.int32)
    return dest.astype(jnp.int32), blk_e, blk_nv, n_act.astype(jnp.int32).reshape(1)


def _combine_kernel(alpha, y_ref, comb_ref, x_ref, mod_ref, lg_ref, lb_ref, o_ref):
    gate2 = mod_ref[5:6, :]
    comb = comb_ref[...]
    ff = comb[:, 0:1] * y_ref[0]
    for k in range(1, TOP_K):
        ff = ff + comb[:, k:k + 1] * y_ref[k]
    o_ref[...] = _layer_norm(alpha * x_ref[...] + gate2 * ff, lg_ref[...], lb_ref[...])


def _combine(dm, layer, y_tok, comb, x1, mod, ln_g, ln_b):
    tm = ROW_TILE
    d = dm.d
    alpha = (2 * dm.depth) ** 0.25
    return pl.pallas_call(
        functools.partial(_combine_kernel, alpha),
        out_shape=jax.ShapeDtypeStruct((dm.n, d), F32),
        grid=(dm.n // tm,),
        in_specs=[
            pl.BlockSpec((TOP_K, tm, d), lambda i: (0, i, 0)),
            pl.BlockSpec((tm, LANES), lambda i: (i, 0)),
            pl.BlockSpec((tm, d), lambda i: (i, 0)),
            pl.BlockSpec((None, None, N_MOD, d), lambda i: (layer, dm.group_of_tile(i, tm), 0, 0)),
            pl.BlockSpec((None, 1, d), lambda i: (layer, 0, 0)),
            pl.BlockSpec((None, 1, d), lambda i: (layer, 0, 0)),
        ],
        out_specs=pl.BlockSpec((tm, d), lambda i: (i, 0)),
        compiler_params=_params(("arbitrary",)),
        name="combine",
    )(y_tok, comb, x1, mod, ln_g, ln_b)


def kernel(x_prompt, x_sample, cache_k, cache_v, c, c_ctx, w_mod, b_mod, w_in, q_norm, k_norm,
           w_attn_o, conv_w, conv_b, conv_ln_g, conv_ln_b, w_conv_o, w_pool, pool_scale,
           sgu_ln_g, sgu_ln_b, sgu_w, sgu_b, w_sgu_o, w_out, ln1_g, ln1_b, ln2_g, ln2_b,
           w_router, b_router, w_gate, b_gate, w_up, b_up, w_down, b_down):
    bp, tp, d = x_prompt.shape
    bs, ts, _ = x_sample.shape
    depth, n_exp, _, ff = w_gate.shape
    past = cache_k.shape[2]
    dm = _Dims(bp, tp, bs, ts, d, past, depth, n_exp, ff)
    assert 1 + bs <= SUBLANES and n_exp <= LANES

    x = jnp.concatenate([x_prompt.reshape(dm.np_, d), x_sample.reshape(dm.ns, d)], axis=0)
    cond = jnp.concatenate([c_ctx[None, :], c, jnp.zeros((SUBLANES - 1 - bs, d), F32)], axis=0)
    mod = _modulation(cond, w_mod, b_mod).reshape(depth, SUBLANES, N_MOD, d)

    cos, sin = _rope_tables(dm, ROW_TILE)
    ck = cache_k.reshape(bs, depth, past, KV_WIDTH)
    cv = cache_v.reshape(bs, depth, past, KV_WIDTH)

    def row(p):
        return p.reshape(depth, 1, p.shape[-1])

    wa, wc, wp, ws, wo = (w.astype(BF16) for w in (w_attn_o, w_conv_o, w_pool, w_sgu_o, w_out))
    sgu_bt = jnp.swapaxes(sgu_b, 1, 2)
    w_router_p = jnp.pad(w_router, ((0, 0), (0, 0), (0, LANES - n_exp)))
    b_router_p = jnp.pad(b_router, ((0, 0), (0, LANES - n_exp)), constant_values=-jnp.inf)
    b_router_p = b_router_p.reshape(depth, 1, LANES)
    bg, bu = b_gate.reshape(depth, n_exp, 1, ff), b_up.reshape(depth, n_exp, 1, ff)
    bd = b_down.reshape(depth, n_exp, 1, d)

    nk = dm.n * TOP_K
    n_super = nk // MOE_SUPER + n_exp
    ks, vs = [], []
    for l in range(depth):
        proj = _in_proj(dm, l, x, mod, w_in)
        q, k, kn, v32 = _qk_prep(dm, l, proj, cos, sin, row(q_norm), row(k_norm))
        ks.append(kn[:dm.np_])
        vs.append(v32[:dm.np_])
        attn_o = jnp.concatenate(
            [_ctx_attn(dm, q, k, proj), _lat_attn(dm, l, q, k, proj, ck, cv)], axis=0)
        conv_h, pool_h, sgu_h = _local_branches(
            dm, l, proj, conv_w, row(conv_b), row(conv_ln_g), row(conv_ln_b),
            row(sgu_ln_g), row(sgu_ln_b), sgu_w, sgu_bt)
        merged = _merge(dm, l, attn_o, conv_h, pool_h, sgu_h, proj, wa, wc, wp, ws, row(pool_scale))
        x1, h2, top_idx, comb = _out_proj(dm, l, merged, x, mod, wo, row(ln1_g), row(ln1_b),
                                          w_router_p, b_router_p)
        dest, blk_e, blk_nv, n_act = _route(dm, top_idx[:, :TOP_K], n_super)
        slot_tok = jnp.full((n_super * MOE_SUPER,), dm.n, jnp.int32).at[dest].set(
            jnp.arange(nk, dtype=jnp.int32) // TOP_K)
        h2_ext = jnp.concatenate([h2, jnp.zeros((1, d), BF16)], axis=0)
        x_sorted = h2_ext[slot_tok]
        y_sorted = _moe(dm, l, x_sorted, blk_e, blk_nv, n_act, w_gate, bg, w_up, bu, w_down, bd)
        y_tok = y_sorted[dest.reshape(dm.n, TOP_K).T]
        x = _combine(dm, l, y_tok, comb, x1, mod, row(ln2_g), row(ln2_b))

    y_prompt = x[:dm.np_].reshape(bp, tp, d)
    y_sample = x[dm.np_:].reshape(bs, ts, d)
    state_k = jnp.stack(ks, axis=1).reshape(bp, tp, depth, N_KV_HEADS, HEAD_DIM).swapaxes(1, 2)
    state_v = jnp.stack(vs, axis=1).reshape(bp, tp, depth, N_KV_HEADS, HEAD_DIM).swapaxes(1, 2)
    return (y_prompt, y_sample, state_k, state_v)
```

```python
import functools
import math

import jax
import jax.numpy as jnp
from jax import lax
from jax.experimental import pallas as pl
from jax.experimental.pallas import tpu as pltpu

F32 = jnp.float32
BF16 = jnp.bfloat16

N_HEADS = 8
N_KV_HEADS = 2
HEAD_DIM = 128
GRID_W = 64
ROPE_THETA = 10000.0
CONV_KERNEL = 31
POOL_WINDOWS = (2, 4, 8, 16)
POOL_GROUPS = 4
SGU_GROUPS = 4
CHUNK = 128
N_BRANCHES = 4
TOP_K = 4
SWIGLU_LIMIT = 7.0
SWIGLU_ALPHA = 1.702
NORM_EPS = 1e-6
N_MOD = 6

ATTN_WIDTH = N_HEADS * HEAD_DIM
KV_WIDTH = N_KV_HEADS * HEAD_DIM
GROUP = N_HEADS // N_KV_HEADS

LANES = 128
SUBLANES = 8
VMEM_LIMIT_BYTES = 56 * 1024 * 1024

ROW_TILE = 256
HALO = 16
IN_PROJ_ROWS = 512
IN_PROJ_COLS = 1024
MOD_COLS = 1024
MOE_SUPER = 1024
MOE_SUB = 256
MOE_FF_TILE = 256
MOE_OUT_TILE = 512


def _dot(a, b):
    return jnp.dot(a, b, preferred_element_type=F32)


def _dot_nt(a, b):
    return lax.dot_general(a, b, (((1,), (1,)), ((), ())), preferred_element_type=F32)


def _layer_norm(x, g, b):
    mu = jnp.mean(x, axis=-1, keepdims=True)
    xc = x - mu
    var = jnp.mean(xc * xc, axis=-1, keepdims=True)
    return xc * lax.rsqrt(var + NORM_EPS) * g + b


def _sigmoid(x):
    return 1.0 / (1.0 + jnp.exp(-x))


def _params(sem, vmem=VMEM_LIMIT_BYTES):
    return pltpu.CompilerParams(dimension_semantics=sem, vmem_limit_bytes=vmem)


class _Dims:
    def __init__(self, bp, tp, bs, ts, d, past, depth, n_exp, ff):
        self.bp, self.tp, self.bs, self.ts = bp, tp, bs, ts
        self.d, self.past, self.depth, self.n_exp, self.ff = d, past, depth, n_exp, ff
        self.np_ = bp * tp
        self.ns = bs * ts
        self.n = self.np_ + self.ns
        self.cw = d // 4
        self.in_w = ATTN_WIDTH + 2 * KV_WIDTH + 5 * self.cw + N_BRANCHES * d
        self.off_k = ATTN_WIDTH
        self.off_v = ATTN_WIDTH + KV_WIDTH
        self.off_a = ATTN_WIDTH + 2 * KV_WIDTH
        self.off_gates = self.off_a + 5 * self.cw
        assert self.cw == POOL_GROUPS * LANES == SGU_GROUPS * CHUNK
        assert self.off_a % self.cw == 0 and self.off_gates % d == 0
        assert tp % ROW_TILE == 0 and ts % ROW_TILE == 0 and self.np_ % ts == 0
        assert ts % GRID_W == 0

    def group_of_tile(self, i, tm):
        npt = self.np_ // tm
        return jnp.where(i < npt, 0, 1 + (i - npt) // (self.ts // tm))

    def seq_tile_pos(self, i, tm):
        npt = self.np_ // tm
        is_p = i < npt
        pos = jnp.where(is_p, i % (self.tp // tm), (i - npt) % (self.ts // tm))
        cnt = jnp.where(is_p, self.tp // tm, self.ts // tm)
        return pos, cnt


def _mod_kernel(c_ref, w_ref, b_ref, o_ref):
    c = c_ref[...]
    s = (c * _sigmoid(c)).astype(BF16)
    o_ref[...] = _dot(s, w_ref[...].astype(BF16)) + b_ref[...]


def _modulation(cond8, w_mod, b_mod):
    depth, d, width = w_mod.shape
    tn = MOD_COLS
    return pl.pallas_call(
        _mod_kernel,
        out_shape=jax.ShapeDtypeStruct((depth, SUBLANES, width), F32),
        grid=(depth, width // tn),
        in_specs=[
            pl.BlockSpec((SUBLANES, d), lambda l, j: (0, 0)),
            pl.BlockSpec((None, d, tn), lambda l, j: (l, 0, j)),
            pl.BlockSpec((None, 1, tn), lambda l, j: (l, 0, j)),
        ],
        out_specs=pl.BlockSpec((None, SUBLANES, tn), lambda l, j: (l, 0, j)),
        compiler_params=_params(("arbitrary", "arbitrary")),
        name="modulation",
    )(cond8, w_mod, b_mod.reshape(depth, 1, width))


def _in_proj_kernel(x_ref, mod_ref, w_ref, o_ref, wbf_ref):
    @pl.when(pl.program_id(1) == 0)
    def _():
        wbf_ref[...] = w_ref[...].astype(BF16)

    shift = mod_ref[0:1, :]
    scale = mod_ref[1:2, :]
    h = (x_ref[...] * (1.0 + scale) + shift).astype(BF16)
    o_ref[...] = _dot(h, wbf_ref[...]).astype(o_ref.dtype)


def _in_proj(dm, layer, x, mod, w_in):
    tm, tn = IN_PROJ_ROWS, IN_PROJ_COLS
    d = dm.d
    return pl.pallas_call(
        _in_proj_kernel,
        out_shape=jax.ShapeDtypeStruct((dm.n, dm.in_w), BF16),
        grid=(dm.in_w // tn, dm.n // tm),
        in_specs=[
            pl.BlockSpec((tm, d), lambda j, i: (i, 0)),
            pl.BlockSpec((None, None, N_MOD, d), lambda j, i: (layer, dm.group_of_tile(i, tm), 0, 0)),
            pl.BlockSpec((None, d, tn), lambda j, i: (layer, 0, j)),
        ],
        out_specs=pl.BlockSpec((tm, tn), lambda j, i: (i, j)),
        scratch_shapes=[pltpu.VMEM((d, tn), BF16)],
        compiler_params=_params(("arbitrary", "arbitrary")),
        name="in_proj",
    )(x, mod, w_in)


def _rope(x, cos, sin_signed):
    lane = lax.broadcasted_iota(jnp.int32, x.shape, 1)
    nxt = pltpu.roll(x, HEAD_DIM - 1, 1)
    prv = pltpu.roll(x, 1, 1)
    partner = jnp.where((lane & 1) == 0, nxt, prv)
    return x * cos + partner * sin_signed


def _rms(x, g):
    return x * lax.rsqrt(jnp.mean(x * x, axis=-1, keepdims=True) + NORM_EPS) * g


def _qk_prep_kernel(q_ref, k_ref, v_ref, cos_ref, sin_ref, qg_ref, kg_ref,
                    qo_ref, ko_ref, kn_ref, vo_ref):
    cos = cos_ref[...]
    sin = sin_ref[...]
    qg = qg_ref[...]
    kg = kg_ref[...]
    scale = HEAD_DIM ** -0.5
    for h in range(N_HEADS):
        cols = slice(h * HEAD_DIM, (h + 1) * HEAD_DIM)
        qn = _rms(q_ref[:, cols].astype(F32), qg)
        qo_ref[:, cols] = (_rope(qn, cos, sin) * scale).astype(qo_ref.dtype)
    for h in range(N_KV_HEADS):
        cols = slice(h * HEAD_DIM, (h + 1) * HEAD_DIM)
        kn = _rms(k_ref[:, cols].astype(F32), kg)
        kn_ref[:, cols] = kn
        ko_ref[:, cols] = _rope(kn, cos, sin).astype(ko_ref.dtype)
    vo_ref[...] = v_ref[...].astype(F32)


def _rope_tables(dm, tm):
    t = dm.ts
    rows = t // GRID_W
    row = jnp.broadcast_to(jnp.arange(rows, dtype=F32)[:, None], (rows, GRID_W)).reshape(t)
    col = jnp.broadcast_to(jnp.arange(GRID_W, dtype=F32)[None, :], (rows, GRID_W)).reshape(t)
    half = HEAD_DIM // 2
    inv_freq = ROPE_THETA ** (-jnp.arange(0, half, 2, dtype=F32) / half)
    ang = jnp.concatenate([row[:, None] * inv_freq, col[:, None] * inv_freq], axis=-1)
    cos = jnp.repeat(jnp.cos(ang), 2, axis=-1)
    sin = jnp.repeat(jnp.sin(ang), 2, axis=-1)
    sign = jnp.tile(jnp.array([-1.0, 1.0], F32), half)
    cos = jnp.concatenate([jnp.ones((tm, HEAD_DIM), F32), cos], axis=0)
    sin = jnp.concatenate([jnp.zeros((tm, HEAD_DIM), F32), sin * sign], axis=0)
    return cos, sin


def _qk_prep(dm, layer, proj, cos, sin, q_norm, k_norm):
    tm = ROW_TILE
    npt = dm.np_ // tm
    tps = dm.ts // tm

    def tab(i):
        return (jnp.where(i < npt, 0, 1 + (i - npt) % tps), 0)

    return pl.pallas_call(
        _qk_prep_kernel,
        out_shape=(
            jax.ShapeDtypeStruct((dm.n, ATTN_WIDTH), BF16),
            jax.ShapeDtypeStruct((dm.n, KV_WIDTH), BF16),
            jax.ShapeDtypeStruct((dm.n, KV_WIDTH), F32),
            jax.ShapeDtypeStruct((dm.n, KV_WIDTH), F32),
        ),
        grid=(dm.n // tm,),
        in_specs=[
            pl.BlockSpec((tm, ATTN_WIDTH), lambda i: (i, 0)),
            pl.BlockSpec((tm, KV_WIDTH), lambda i: (i, dm.off_k // KV_WIDTH)),
            pl.BlockSpec((tm, KV_WIDTH), lambda i: (i, dm.off_v // KV_WIDTH)),
            pl.BlockSpec((tm, HEAD_DIM), tab),
            pl.BlockSpec((tm, HEAD_DIM), tab),
            pl.BlockSpec((None, 1, HEAD_DIM), lambda i: (layer, 0, 0)),
            pl.BlockSpec((None, 1, HEAD_DIM), lambda i: (layer, 0, 0)),
        ],
        out_specs=(
            pl.BlockSpec((tm, ATTN_WIDTH), lambda i: (i, 0)),
            pl.BlockSpec((tm, KV_WIDTH), lambda i: (i, 0)),
            pl.BlockSpec((tm, KV_WIDTH), lambda i: (i, 0)),
            pl.BlockSpec((tm, KV_WIDTH), lambda i: (i, 0)),
        ),
        compiler_params=_params(("arbitrary",)),
        name="qk_prep",
    )(proj, proj, proj, cos, sin, q_norm, k_norm)


def _ctx_attn_kernel(q_ref, k_ref, v_ref, o_ref):
    for h in range(N_HEADS):
        kv = h // GROUP
        cols = slice(h * HEAD_DIM, (h + 1) * HEAD_DIM)
        kcols = slice(kv * HEAD_DIM, (kv + 1) * HEAD_DIM)
        s = _dot_nt(q_ref[:, cols], k_ref[:, kcols])
        m = jnp.max(s, axis=-1, keepdims=True)
        p = jnp.exp(s - m)
        l = jnp.sum(p, axis=-1, keepdims=True)
        o = _dot(p.astype(BF16), v_ref[:, kcols]) / l
        o_ref[:, cols] = o.astype(o_ref.dtype)


def _ctx_attn(dm, q, k, proj):
    t = dm.tp
    return pl.pallas_call(
        _ctx_attn_kernel,
        out_shape=jax.ShapeDtypeStruct((dm.np_, ATTN_WIDTH), BF16),
        grid=(dm.bp,),
        in_specs=[
            pl.BlockSpec((t, ATTN_WIDTH), lambda b: (b, 0)),
            pl.BlockSpec((t, KV_WIDTH), lambda b: (b, 0)),
            pl.BlockSpec((t, KV_WIDTH), lambda b: (b, dm.off_v // KV_WIDTH)),
        ],
        out_specs=pl.BlockSpec((t, ATTN_WIDTH), lambda b: (b, 0)),
        compiler_params=_params(("arbitrary",)),
        name="ctx_attn",
    )(q, k, proj)


def _lat_attn_kernel(q_ref, k_ref, v_ref, ck_ref, cv_ref, o_ref):
    ck = ck_ref[...].astype(BF16)
    cv = cv_ref[...].astype(BF16)
    for h in range(N_HEADS):
        kv = h // GROUP
        cols = slice(h * HEAD_DIM, (h + 1) * HEAD_DIM)
        kcols = slice(kv * HEAD_DIM, (kv + 1) * HEAD_DIM)
        qh = q_ref[:, cols]
        s_ctx = _dot_nt(qh, ck[:, kcols])
        s_lat = _dot_nt(qh, k_ref[:, kcols])
        m = jnp.maximum(jnp.max(s_ctx, axis=-1, keepdims=True),
                        jnp.max(s_lat, axis=-1, keepdims=True))
        p_ctx = jnp.exp(s_ctx - m)
        p_lat = jnp.exp(s_lat - m)
        l = jnp.sum(p_ctx, axis=-1, keepdims=True) + jnp.sum(p_lat, axis=-1, keepdims=True)
        o = _dot(p_ctx.astype(BF16), cv[:, kcols]) + _dot(p_lat.astype(BF16), v_ref[:, kcols])
        o_ref[:, cols] = (o / l).astype(o_ref.dtype)


def _lat_attn(dm, layer, q, k, proj, cache_k, cache_v):
    tq = ROW_TILE
    ts = dm.ts
    qb = ts // tq
    return pl.pallas_call(
        _lat_attn_kernel,
        out_shape=jax.ShapeDtypeStruct((dm.ns, ATTN_WIDTH), BF16),
        grid=(dm.bs, qb),
        in_specs=[
            pl.BlockSpec((tq, ATTN_WIDTH), lambda b, i: (dm.np_ // tq + b * qb + i, 0)),
            pl.BlockSpec((ts, KV_WIDTH), lambda b, i: (dm.np_ // ts + b, 0)),
            pl.BlockSpec((ts, KV_WIDTH), lambda b, i: (dm.np_ // ts + b, dm.off_v // KV_WIDTH)),
            pl.BlockSpec((None, None, dm.past, KV_WIDTH), lambda b, i: (b, layer, 0, 0)),
            pl.BlockSpec((None, None, dm.past, KV_WIDTH), lambda b, i: (b, layer, 0, 0)),
        ],
        out_specs=pl.BlockSpec((tq, ATTN_WIDTH), lambda b, i: (b * qb + i, 0)),
        compiler_params=_params(("arbitrary", "arbitrary")),
        name="lat_attn",
    )(q, k, proj, cache_k, cache_v)


def _local_kernel(dm, a_ref, ap_ref, an_ref, g_ref, gp_ref, gn_ref, z_ref, zp_ref, zn_ref,
                  u_ref, v_ref, cw_ref, cb_ref, clg_ref, clb_ref, slg_ref, slb_ref,
                  sw_ref, sbt_ref, conv_o, pool_o, sgu_o, ubuf, zbuf):
    t = ROW_TILE
    i = pl.program_id(0)
    pos, cnt = dm.seq_tile_pos(i, t)
    has_prev = (pos > 0).astype(F32)
    has_next = (pos < cnt - 1).astype(F32)

    def glu(a, g):
        return a[...].astype(F32) * _sigmoid(g[...].astype(F32))

    ubuf[0:HALO, :] = glu(ap_ref, gp_ref) * has_prev
    ubuf[HALO:HALO + t, :] = glu(a_ref, g_ref)
    ubuf[HALO + t:, :] = glu(an_ref, gn_ref) * has_next
    zbuf[0:HALO, :] = zp_ref[...].astype(F32) * has_prev
    zbuf[HALO:HALO + t, :] = z_ref[...].astype(F32)
    zbuf[HALO + t:, :] = zn_ref[...].astype(F32) * has_next

    half_rows = t // 2
    centre = CONV_KERNEL // 2
    for r in range(2):
        pieces = []
        for c in range(dm.cw // LANES):
            cols = slice(c * LANES, (c + 1) * LANES)
            acc = jnp.zeros((half_rows, LANES), F32)
            for k in range(CONV_KERNEL):
                start = HALO + r * half_rows + k - centre
                acc = acc + cw_ref[k:k + 1, cols] * ubuf[start:start + half_rows, cols]
            pieces.append(acc)
        y = jnp.concatenate(pieces, axis=1) + cb_ref[...]
        y = _layer_norm(y, clg_ref[...], clb_ref[...])
        conv_o[r * half_rows:(r + 1) * half_rows, :] = (y * _sigmoid(y)).astype(conv_o.dtype)

    seq_len = jnp.where(i < dm.np_ // t, dm.tp, dm.ts)
    tok = pos * t + lax.broadcasted_iota(jnp.int32, (t, LANES), 0)
    for gi, w in enumerate(POOL_WINDOWS):
        cols = slice(gi * LANES, (gi + 1) * LANES)
        acc = jnp.zeros((t, LANES), F32)
        for j in range(-(w // 2), w - w // 2):
            acc = acc + zbuf[HALO + j:HALO + j + t, cols]
        lo = jnp.maximum(tok - w // 2, 0)
        hi = jnp.minimum(tok - w // 2 + w, seq_len)
        mean = acc / (hi - lo).astype(F32)
        pool_o[:, cols] = (mean - zbuf[HALO:HALO + t, cols]).astype(pool_o.dtype)

    vn = _layer_norm(v_ref[...].astype(F32), slg_ref[...], slb_ref[...]).astype(BF16)
    for gi in range(SGU_GROUPS):
        cols = slice(gi * CHUNK, (gi + 1) * CHUNK)
        w = sw_ref[gi].astype(BF16)
        bias = sbt_ref[:, gi:gi + 1]
        for c in range(t // CHUNK):
            rows = slice(c * CHUNK, (c + 1) * CHUNK)
            mixed = _dot(w, vn[rows, cols]) + bias
            sgu_o[rows, cols] = (u_ref[rows, cols].astype(F32) * mixed).astype(sgu_o.dtype)


def _local_branches(dm, layer, proj, conv_w, conv_b, conv_ln_g, conv_ln_b,
                    sgu_ln_g, sgu_ln_b, sgu_w, sgu_bt):
    t = ROW_TILE
    cw = dm.cw
    hb = t // HALO
    last = dm.n // HALO - 1
    ca = dm.off_a // cw

    def cur(c):
        return pl.BlockSpec((t, cw), lambda i: (i, c))

    def prev(c):
        return pl.BlockSpec((HALO, cw), lambda i: (jnp.maximum(i * hb - 1, 0), c))

    def nxt(c):
        return pl.BlockSpec((HALO, cw), lambda i: (jnp.minimum((i + 1) * hb, last), c))

    def vec():
        return pl.BlockSpec((None, 1, cw), lambda i: (layer, 0, 0))

    out = jax.ShapeDtypeStruct((dm.n, cw), BF16)
    return pl.pallas_call(
        functools.partial(_local_kernel, dm),
        out_shape=(out, out, out),
        grid=(dm.n // t,),
        in_specs=[
            cur(ca), prev(ca), nxt(ca),
            cur(ca + 1), prev(ca + 1), nxt(ca + 1),
            cur(ca + 2), prev(ca + 2), nxt(ca + 2),
            cur(ca + 3), cur(ca + 4),
            pl.BlockSpec((None, CONV_KERNEL, cw), lambda i: (layer, 0, 0)),
            vec(), vec(), vec(), vec(), vec(),
            pl.BlockSpec((None, SGU_GROUPS, CHUNK, CHUNK), lambda i: (layer, 0, 0, 0)),
            pl.BlockSpec((None, CHUNK, SGU_GROUPS), lambda i: (layer, 0, 0)),
        ],
        out_specs=(pl.BlockSpec((t, cw), lambda i: (i, 0)),) * 3,
        scratch_shapes=[pltpu.VMEM((t + 2 * HALO, cw), F32), pltpu.VMEM((t + 2 * HALO, cw), F32)],
        compiler_params=_params(("arbitrary",)),
        name="local_branches",
    )(proj, proj, proj, proj, proj, proj, proj, proj, proj, proj, proj,
      conv_w, conv_b, conv_ln_g, conv_ln_b, sgu_ln_g, sgu_ln_b, sgu_w, sgu_bt)


def _merge_kernel(dm, o_ref, cv_ref, pl_ref, sg_ref, g0_ref, g1_ref, g2_ref, g3_ref,
                  wa_ref, wc_ref, wp_ref, ws_ref, ps_ref, m_ref):
    cw = dm.cw
    for c in range(dm.d // cw):
        cols = slice(c * cw, (c + 1) * cw)
        ya = _dot(o_ref[...], wa_ref[:, cols])
        yb = _dot(cv_ref[...], wc_ref[:, cols])
        yc = _dot(pl_ref[:, c * LANES:(c + 1) * LANES], wp_ref[c]) * ps_ref[:, cols]
        yd = _dot(sg_ref[...], ws_ref[:, cols])
        merged = (_sigmoid(g0_ref[:, cols].astype(F32)) * ya
                  + _sigmoid(g1_ref[:, cols].astype(F32)) * yb
                  + _sigmoid(g2_ref[:, cols].astype(F32)) * yc
                  + _sigmoid(g3_ref[:, cols].astype(F32)) * yd)
        m_ref[:, cols] = merged.astype(m_ref.dtype)


def _merge(dm, layer, attn_o, conv_h, pool_h, sgu_h, proj, wa, wc, wp, ws, pool_scale):
    tm = ROW_TILE
    d, cw = dm.d, dm.cw
    g0 = dm.off_gates // d
    assert d // cw == POOL_GROUPS and wp.shape[-1] == cw

    def gate(b):
        return pl.BlockSpec((tm, d), lambda i: (i, g0 + b))

    return pl.pallas_call(
        functools.partial(_merge_kernel, dm),
        out_shape=jax.ShapeDtypeStruct((dm.n, d), BF16),
        grid=(dm.n // tm,),
        in_specs=[
            pl.BlockSpec((tm, ATTN_WIDTH), lambda i: (i, 0)),
            pl.BlockSpec((tm, cw), lambda i: (i, 0)),
            pl.BlockSpec((tm, cw), lambda i: (i, 0)),
            pl.BlockSpec((tm, cw), lambda i: (i, 0)),
            gate(0), gate(1), gate(2), gate(3),
            pl.BlockSpec((None, ATTN_WIDTH, d), lambda i: (layer, 0, 0)),
            pl.BlockSpec((None, cw, d), lambda i: (layer, 0, 0)),
            pl.BlockSpec((None, POOL_GROUPS, LANES, cw), lambda i: (layer, 0, 0, 0)),
            pl.BlockSpec((None, cw, d), lambda i: (layer, 0, 0)),
            pl.BlockSpec((None, 1, d), lambda i: (layer, 0, 0)),
        ],
        out_specs=pl.BlockSpec((tm, d), lambda i: (i, 0)),
        compiler_params=_params(("arbitrary",)),
        name="merge",
    )(attn_o, conv_h, pool_h, sgu_h, proj, proj, proj, proj, wa, wc, wp, ws, pool_scale)


def _out_proj_kernel(alpha, m_ref, x_ref, mod_ref, w_ref, lg_ref, lb_ref, wr_ref, br_ref,
                     x1_ref, h2_ref, idx_ref, comb_ref):
    gate1 = mod_ref[2:3, :]
    shift2 = mod_ref[3:4, :]
    scale2 = mod_ref[4:5, :]
    y = _dot(m_ref[...], w_ref[...])
    x1 = _layer_norm(alpha * x_ref[...] + gate1 * y, lg_ref[...], lb_ref[...])
    x1_ref[...] = x1
    h2 = x1 * (1.0 + scale2) + shift2
    h2_hi = h2.astype(BF16)
    h2_ref[...] = h2_hi

    h2_lo = (h2 - h2_hi.astype(F32)).astype(BF16)
    wr = wr_ref[...]
    wr_hi = wr.astype(BF16)
    wr_lo = (wr - wr_hi.astype(F32)).astype(BF16)
    logits = _dot(h2_hi, wr_hi) + _dot(h2_hi, wr_lo) + _dot(h2_lo, wr_hi) + br_ref[...]

    lane = lax.broadcasted_iota(jnp.int32, logits.shape, 1)
    idx_out = jnp.zeros(logits.shape, jnp.int32)
    val_out = jnp.zeros(logits.shape, F32)
    top = None
    den = jnp.zeros((logits.shape[0], 1), F32)
    for k in range(TOP_K):
        m = jnp.max(logits, axis=-1, keepdims=True)
        ik = jnp.min(jnp.where(logits == m, lane, LANES), axis=-1, keepdims=True)
        if top is None:
            top = m
        e = jnp.exp(m - top)
        den = den + e
        idx_out = jnp.where(lane == k, ik, idx_out)
        val_out = jnp.where(lane == k, e, val_out)
        logits = jnp.where(lane == ik, -jnp.inf, logits)
    idx_ref[...] = idx_out
    comb_ref[...] = val_out / den


def _out_proj(dm, layer, merged, x, mod, w_out, ln_g, ln_b, w_router, b_router):
    tm = ROW_TILE
    d = dm.d
    alpha = (2 * dm.depth) ** 0.25

    def vec():
        return pl.BlockSpec((None, 1, d), lambda i: (layer, 0, 0))

    return pl.pallas_call(
        functools.partial(_out_proj_kernel, alpha),
        out_shape=(
            jax.ShapeDtypeStruct((dm.n, d), F32),
            jax.ShapeDtypeStruct((dm.n, d), BF16),
            jax.ShapeDtypeStruct((dm.n, LANES), jnp.int32),
            jax.ShapeDtypeStruct((dm.n, LANES), F32),
        ),
        grid=(dm.n // tm,),
        in_specs=[
            pl.BlockSpec((tm, d), lambda i: (i, 0)),
            pl.BlockSpec((tm, d), lambda i: (i, 0)),
            pl.BlockSpec((None, None, N_MOD, d), lambda i: (layer, dm.group_of_tile(i, tm), 0, 0)),
            pl.BlockSpec((None, d, d), lambda i: (layer, 0, 0)),
            vec(), vec(),
            pl.BlockSpec((None, d, LANES), lambda i: (layer, 0, 0)),
            pl.BlockSpec((None, 1, LANES), lambda i: (layer, 0, 0)),
        ],
        out_specs=(
            pl.BlockSpec((tm, d), lambda i: (i, 0)),
            pl.BlockSpec((tm, d), lambda i: (i, 0)),
            pl.BlockSpec((tm, LANES), lambda i: (i, 0)),
            pl.BlockSpec((tm, LANES), lambda i: (i, 0)),
        ),
        compiler_params=_params(("arbitrary",)),
        name="out_proj",
    )(merged, x, mod, w_out, ln_g, ln_b, w_router, b_router)


def _moe_live_chunks(nv, fn):
    pair = 2 * MOE_SUB
    for base in range(0, MOE_SUPER, pair):
        both = nv > base + MOE_SUB
        only_first = jnp.logical_and(nv > base, jnp.logical_not(both))
        pl.when(both)(functools.partial(fn, base, pair))
        pl.when(only_first)(functools.partial(fn, base, MOE_SUB))


def _moe_kernel(nfa, be_ref, nv_ref, na_ref, x_ref, wg_ref, bg_ref, wu_ref, bu_ref, wd_ref, bd_ref,
                o_ref, act_ref, wgu_ref, wdn_ref):
    s = pl.program_id(0)
    j = pl.program_id(1)
    nv = nv_ref[s]
    tf = MOE_FF_TILE
    active = s < na_ref[0]

    @pl.when(jnp.logical_and(active, j < nfa))
    def _():
        wgu_ref[:, :tf] = wg_ref[...].astype(BF16)
        wgu_ref[:, tf:] = wu_ref[...].astype(BF16)
        bg = bg_ref[...]
        bu = bu_ref[...]
        col0 = pl.multiple_of(j * tf, tf)

        def chunk(base, m):
            gu = _dot(x_ref[base:base + m, :], wgu_ref[...])
            gt = jnp.minimum(gu[:, :tf] + bg, SWIGLU_LIMIT)
            up = jnp.clip(gu[:, tf:] + bu, -SWIGLU_LIMIT, SWIGLU_LIMIT)
            act = gt * _sigmoid(SWIGLU_ALPHA * gt) * (up + 1.0)
            act_ref[base:base + m, pl.ds(col0, tf)] = act.astype(BF16)

        _moe_live_chunks(nv, chunk)

    @pl.when(j >= nfa)
    def _():
        @pl.when(active)
        def _():
            wdn_ref[...] = wd_ref[...].astype(BF16)

        bd = bd_ref[...]

        def chunk(base, m):
            y = _dot(act_ref[base:base + m, :], wdn_ref[...]) + bd
            o_ref[base:base + m, :] = y.astype(o_ref.dtype)

        _moe_live_chunks(nv, chunk)
        for base in range(0, MOE_SUPER, MOE_SUB):
            @pl.when(nv <= base)
            def _():
                o_ref[base:base + MOE_SUB, :] = jnp.zeros((MOE_SUB, o_ref.shape[1]), o_ref.dtype)


def _moe(dm, layer, x_sorted, blk_e, blk_nv, n_act, w_gate, b_gate, w_up, b_up, w_down, b_down):
    d, ff, tf, tn, r = dm.d, dm.ff, MOE_FF_TILE, MOE_OUT_TILE, MOE_SUPER
    nsb = x_sorted.shape[0] // r
    nfa = ff // tf
    nfb = d // tn

    def blk(s, na):
        return jnp.minimum(s, na[0] - 1)

    def col_a(s, j, na):
        return jnp.where(s < na[0], jnp.minimum(j, nfa - 1), nfa - 1)

    def col_b(s, j, na):
        return jnp.where(s < na[0], jnp.maximum(j - nfa, 0), nfb - 1)

    grid_spec = pltpu.PrefetchScalarGridSpec(
        num_scalar_prefetch=3,
        grid=(nsb, nfa + nfb),
        in_specs=[
            pl.BlockSpec((r, d), lambda s, j, be, nv, na: (blk(s, na), 0)),
            pl.BlockSpec((None, None, d, tf), lambda s, j, be, nv, na: (layer, be[s], 0, col_a(s, j, na))),
            pl.BlockSpec((None, None, 1, tf), lambda s, j, be, nv, na: (layer, be[s], 0, col_a(s, j, na))),
            pl.BlockSpec((None, None, d, tf), lambda s, j, be, nv, na: (layer, be[s], 0, col_a(s, j, na))),
            pl.BlockSpec((None, None, 1, tf), lambda s, j, be, nv, na: (layer, be[s], 0, col_a(s, j, na))),
            pl.BlockSpec((None, None, ff, tn), lambda s, j, be, nv, na: (layer, be[s], 0, col_b(s, j, na))),
            pl.BlockSpec((None, None, 1, tn), lambda s, j, be, nv, na: (layer, be[s], 0, col_b(s, j, na))),
        ],
        out_specs=pl.BlockSpec((r, tn), lambda s, j, be, nv, na: (s, jnp.maximum(j - nfa, 0))),
        scratch_shapes=[
            pltpu.VMEM((r, ff), BF16),
            pltpu.VMEM((d, 2 * tf), BF16),
            pltpu.VMEM((ff, tn), BF16),
        ],
    )
    return pl.pallas_call(
        functools.partial(_moe_kernel, nfa),
        out_shape=jax.ShapeDtypeStruct((nsb * r, d), BF16),
        grid_spec=grid_spec,
        compiler_params=_params(("arbitrary", "arbitrary")),
        name="moe_ffn",
    )(blk_e, blk_nv, n_act, x_sorted, w_gate, b_gate, w_up, b_up, w_down, b_down)


def _route(dm, top_idx, n_super):
    e, r = dm.n_exp, MOE_SUPER
    flat_e = top_idx.reshape(-1)
    onehot = (flat_e[:, None] == jnp.arange(e, dtype=jnp.int32)[None, :]).astype(jnp.int32)
    rank = jnp.take_along_axis(jnp.cumsum(onehot, axis=0), flat_e[:, None], axis=1)[:, 0] - 1
    counts = jnp.sum(onehot, axis=0)
    nsb_e = (counts + r - 1) // r
    sb_end = jnp.cumsum(nsb_e)
    sb_start = sb_end - nsb_e
    dest = sb_start[flat_e] * r + rank
    n_act = sb_end[-1]
    sidx = jnp.arange(n_super, dtype=jnp.int32)
    blk_e = jnp.clip(jnp.searchsorted(sb_end, jnp.minimum(sidx, n_act - 1), side='right'), 0, e - 1)
    blk_e = blk_e.astype(jnp.int32)
    blk_nv = jnp.clip(counts[blk_e] - (sidx - sb_start[blk_e]) * r, 0, r)
    blk_nv = jnp.where(sidx < n_act, blk_nv, 0).astype(jnp.int32)
    return dest.astype(jnp.int32), blk_e, blk_nv, n_act.astype(jnp.int32).reshape(1)


def _combine_kernel(alpha, y_ref, comb_ref, x_ref, mod_ref, lg_ref, lb_ref, o_ref):
    gate2 = mod_ref[5:6, :]
    comb = comb_ref[...]
    ff = comb[:, 0:1] * y_ref[0].astype(F32)
    for k in range(1, TOP_K):
        ff = ff + comb[:, k:k + 1] * y_ref[k].astype(F32)
    o_ref[...] = _layer_norm(alpha * x_ref[...] + gate2 * ff, lg_ref[...], lb_ref[...])


def _combine(dm, layer, y_tok, comb, x1, mod, ln_g, ln_b):
    tm = ROW_TILE
    d = dm.d
    alpha = (2 * dm.depth) ** 0.25
    return pl.pallas_call(
        functools.partial(_combine_kernel, alpha),
        out_shape=jax.ShapeDtypeStruct((dm.n, d), F32),
        grid=(dm.n // tm,),
        in_specs=[
            pl.BlockSpec((TOP_K, tm, d), lambda i: (0, i, 0)),
            pl.BlockSpec((tm, LANES), lambda i: (i, 0)),
            pl.BlockSpec((tm, d), lambda i: (i, 0)),
            pl.BlockSpec((None, None, N_MOD, d), lambda i: (layer, dm.group_of_tile(i, tm), 0, 0)),
            pl.BlockSpec((None, 1, d), lambda i: (layer, 0, 0)),
            pl.BlockSpec((None, 1, d), lambda i: (layer, 0, 0)),
        ],
        out_specs=pl.BlockSpec((tm, d), lambda i: (i, 0)),
        compiler_params=_params(("arbitrary",)),
        name="combine",
    )(y_tok, comb, x1, mod, ln_g, ln_b)


def kernel(x_prompt, x_sample, cache_k, cache_v, c, c_ctx, w_mod, b_mod, w_in, q_norm, k_norm,
           w_attn_o, conv_w, conv_b, conv_ln_g, conv_ln_b, w_conv_o, w_pool, pool_scale,
           sgu_ln_g, sgu_ln_b, sgu_w, sgu_b, w_sgu_o, w_out, ln1_g, ln1_b, ln2_g, ln2_b,
           w_router, b_router, w_gate, b_gate, w_up, b_up, w_down, b_down):
    bp, tp, d = x_prompt.shape
    bs, ts, _ = x_sample.shape
    depth, n_exp, _, ff = w_gate.shape
    past = cache_k.shape[2]
    dm = _Dims(bp, tp, bs, ts, d, past, depth, n_exp, ff)
    assert 1 + bs <= SUBLANES and n_exp <= LANES

    x = jnp.concatenate([x_prompt.reshape(dm.np_, d), x_sample.reshape(dm.ns, d)], axis=0)
    cond = jnp.concatenate([c_ctx[None, :], c, jnp.zeros((SUBLANES - 1 - bs, d), F32)], axis=0)
    mod = _modulation(cond, w_mod, b_mod).reshape(depth, SUBLANES, N_MOD, d)

    cos, sin = _rope_tables(dm, ROW_TILE)
    ck = cache_k.reshape(bs, depth, past, KV_WIDTH)
    cv = cache_v.reshape(bs, depth, past, KV_WIDTH)

    def row(p):
        return p.reshape(depth, 1, p.shape[-1])

    wa, wc, wp, ws, wo = (w.astype(BF16) for w in (w_attn_o, w_conv_o, w_pool, w_sgu_o, w_out))
    sgu_bt = jnp.swapaxes(sgu_b, 1, 2)
    w_router_p = jnp.pad(w_router, ((0, 0), (0, 0), (0, LANES - n_exp)))
    b_router_p = jnp.pad(b_router, ((0, 0), (0, LANES - n_exp)), constant_values=-jnp.inf)
    b_router_p = b_router_p.reshape(depth, 1, LANES)
    bg, bu = b_gate.reshape(depth, n_exp, 1, ff), b_up.reshape(depth, n_exp, 1, ff)
    bd = b_down.reshape(depth, n_exp, 1, d)

    nk = dm.n * TOP_K
    n_super = nk // MOE_SUPER + n_exp
    ks, vs = [], []
    for l in range(depth):
        proj = _in_proj(dm, l, x, mod, w_in)
        q, k, kn, v32 = _qk_prep(dm, l, proj, cos, sin, row(q_norm), row(k_norm))
        ks.append(kn[:dm.np_])
        vs.append(v32[:dm.np_])
        attn_o = jnp.concatenate(
            [_ctx_attn(dm, q, k, proj), _lat_attn(dm, l, q, k, proj, ck, cv)], axis=0)
        conv_h, pool_h, sgu_h = _local_branches(
            dm, l, proj, conv_w, row(conv_b), row(conv_ln_g), row(conv_ln_b),
            row(sgu_ln_g), row(sgu_ln_b), sgu_w, sgu_bt)
        merged = _merge(dm, l, attn_o, conv_h, pool_h, sgu_h, proj, wa, wc, wp, ws, row(pool_scale))
        x1, h2, top_idx, comb = _out_proj(dm, l, merged, x, mod, wo, row(ln1_g), row(ln1_b),
                                          w_router_p, b_router_p)
        dest, blk_e, blk_nv, n_act = _route(dm, top_idx[:, :TOP_K], n_super)
        slot_tok = jnp.full((n_super * MOE_SUPER,), dm.n, jnp.int32).at[dest].set(
            jnp.arange(nk, dtype=jnp.int32) // TOP_K)
        h2_ext = jnp.concatenate([h2, jnp.zeros((1, d), BF16)], axis=0)
        x_sorted = h2_ext[slot_tok]
        y_sorted = _moe(dm, l, x_sorted, blk_e, blk_nv, n_act, w_gate, bg, w_up, bu, w_down, bd)
        y_tok = y_sorted[dest.reshape(dm.n, TOP_K).T.reshape(-1)].reshape(TOP_K, dm.n, d)
        x = _combine(dm, l, y_tok, comb, x1, mod, row(ln2_g), row(ln2_b))

    y_prompt = x[:dm.np_].reshape(bp, tp, d)
    y_sample = x[dm.np_:].reshape(bs, ts, d)
    state_k = jnp.stack(ks, axis=1).reshape(bp, tp, depth, N_KV_HEADS, HEAD_DIM).swapaxes(1, 2)
    state_v = jnp.stack(vs, axis=1).reshape(bp, tp, depth, N_KV_HEADS, HEAD_DIM).swapaxes(1, 2)
    return (y_prompt, y_sample, state_k, state_v)
```

```python
import functools

import jax
import jax.numpy as jnp
from jax import lax
from jax.experimental import pallas as pl
from jax.experimental.pallas import tpu as pltpu
from jax.experimental.pallas import tpu_sc as plsc

F32 = jnp.float32
BF16 = jnp.bfloat16

N_HEADS = 8
N_KV_HEADS = 2
HEAD_DIM = 128
GRID_W = 64
ROPE_THETA = 10000.0
CONV_KERNEL = 31
POOL_WINDOWS = (2, 4, 8, 16)
POOL_GROUPS = 4
SGU_GROUPS = 4
CHUNK = 128
N_BRANCHES = 4
TOP_K = 4
SWIGLU_LIMIT = 7.0
SWIGLU_ALPHA = 1.702
NORM_EPS = 1e-6
N_MOD = 6

ATTN_WIDTH = N_HEADS * HEAD_DIM
KV_WIDTH = N_KV_HEADS * HEAD_DIM
GROUP = N_HEADS // N_KV_HEADS

LANES = 128
SUBLANES = 8
VMEM_LIMIT_BYTES = 56 * 1024 * 1024

ROW_TILE = 256
HALO = 16
IN_PROJ_ROWS = 512
IN_PROJ_COLS = 1024
MOD_COLS = 1024
MOE_SUPER = 1024
MOE_SUB = 256
MOE_FF_TILE = 512
MOE_OUT_TILE = 512
GATHER_WINDOW = 128
GATHER_WORDS = 256


def _dot(a, b):
    return jnp.dot(a, b, preferred_element_type=F32)


def _dot_nt(a, b):
    return lax.dot_general(a, b, (((1,), (1,)), ((), ())), preferred_element_type=F32)


def _layer_norm(x, g, b):
    mu = jnp.mean(x, axis=-1, keepdims=True)
    xc = x - mu
    var = jnp.mean(xc * xc, axis=-1, keepdims=True)
    return xc * lax.rsqrt(var + NORM_EPS) * g + b


def _sigmoid(x):
    return 1.0 / (1.0 + jnp.exp(-x))


def _params(sem, vmem=VMEM_LIMIT_BYTES):
    return pltpu.CompilerParams(dimension_semantics=sem, vmem_limit_bytes=vmem)


_HIGH_HALF = 0xFFFF0000


def _pack_pairs(x):
    w = x.shape[1] // 2
    lo = lax.bitcast_convert_type(x[:, :w].astype(BF16).astype(F32), jnp.uint32)
    hi = lax.bitcast_convert_type(x[:, w:].astype(BF16).astype(F32), jnp.uint32)
    return (lo >> 16) | (hi & jnp.uint32(_HIGH_HALF))


def _unpack_pairs(words):
    lo = lax.bitcast_convert_type(words << 16, F32)
    hi = lax.bitcast_convert_type(words & jnp.uint32(_HIGH_HALF), F32)
    return lo, hi


def _gather_rows(table, idx):
    n_rows, full_width = table.shape
    split = full_width // GATHER_WORDS
    assert table.dtype.itemsize == 4 and full_width % GATHER_WORDS == 0
    table = table.reshape(n_rows * split, GATHER_WORDS)
    idx = (idx[:, None] * split + jnp.arange(split, dtype=jnp.int32)[None, :]).reshape(-1)
    out = _gather_pieces(table, idx)
    return out.reshape(-1, full_width)


def _gather_pieces(table, idx):
    n_idx = idx.shape[0]
    width = table.shape[1]
    assert n_idx % GATHER_WINDOW == 0
    mesh = plsc.VectorSubcoreMesh(core_axis_name="core", subcore_axis_name="subcore")

    @functools.partial(
        pl.kernel,
        out_type=jax.ShapeDtypeStruct((n_idx, width), table.dtype),
        mesh=mesh,
        scratch_types=[],
    )
    def gather_kernel(table_hbm, idx_hbm, out_hbm):
        def body(idx_vmem, out_vmem):
            pltpu.sync_copy(table_hbm.at[idx_vmem.at[0]], out_vmem)

        pltpu.emit_pipeline(
            body,
            grid=(n_idx // GATHER_WINDOW,),
            in_specs=[pl.BlockSpec((1, GATHER_WINDOW), lambda i: (0, i))],
            out_specs=[pl.BlockSpec((GATHER_WINDOW, width), lambda i: (i, 0))],
            core_axis_name=("core", "subcore"),
            dimension_semantics=(pltpu.PARALLEL,),
        )(idx_hbm, out_hbm)

    return gather_kernel(table, idx.reshape(1, n_idx))


class _Dims:
    def __init__(self, bp, tp, bs, ts, d, past, depth, n_exp, ff):
        self.bp, self.tp, self.bs, self.ts = bp, tp, bs, ts
        self.d, self.past, self.depth, self.n_exp, self.ff = d, past, depth, n_exp, ff
        self.np_ = bp * tp
        self.ns = bs * ts
        self.n = self.np_ + self.ns
        self.cw = d // 4
        self.in_w = ATTN_WIDTH + 2 * KV_WIDTH + 5 * self.cw + N_BRANCHES * d
        self.off_k = ATTN_WIDTH
        self.off_v = ATTN_WIDTH + KV_WIDTH
        self.off_a = ATTN_WIDTH + 2 * KV_WIDTH
        self.off_gates = self.off_a + 5 * self.cw
        assert self.cw == POOL_GROUPS * LANES == SGU_GROUPS * CHUNK
        assert self.off_a % self.cw == 0 and self.off_gates % d == 0
        assert tp % ROW_TILE == 0 and ts % ROW_TILE == 0 and self.np_ % ts == 0
        assert ts % GRID_W == 0

    def group_of_tile(self, i, tm):
        npt = self.np_ // tm
        return jnp.where(i < npt, 0, 1 + (i - npt) // (self.ts // tm))

    def seq_tile_pos(self, i, tm):
        npt = self.np_ // tm
        is_p = i < npt
        pos = jnp.where(is_p, i % (self.tp // tm), (i - npt) % (self.ts // tm))
        cnt = jnp.where(is_p, self.tp // tm, self.ts // tm)
        return pos, cnt


def _mod_kernel(c_ref, w_ref, b_ref, o_ref):
    c = c_ref[...]
    s = (c * _sigmoid(c)).astype(BF16)
    o_ref[...] = _dot(s, w_ref[...].astype(BF16)) + b_ref[...]


def _modulation(cond8, w_mod, b_mod):
    depth, d, width = w_mod.shape
    tn = MOD_COLS
    return pl.pallas_call(
        _mod_kernel,
        out_shape=jax.ShapeDtypeStruct((depth, SUBLANES, width), F32),
        grid=(depth, width // tn),
        in_specs=[
            pl.BlockSpec((SUBLANES, d), lambda l, j: (0, 0)),
            pl.BlockSpec((None, d, tn), lambda l, j: (l, 0, j)),
            pl.BlockSpec((None, 1, tn), lambda l, j: (l, 0, j)),
        ],
        out_specs=pl.BlockSpec((None, SUBLANES, tn), lambda l, j: (l, 0, j)),
        compiler_params=_params(("arbitrary", "arbitrary")),
        name="modulation",
    )(cond8, w_mod, b_mod.reshape(depth, 1, width))


def _in_proj_kernel(x_ref, mod_ref, w_ref, o_ref, wbf_ref):
    @pl.when(pl.program_id(1) == 0)
    def _():
        wbf_ref[...] = w_ref[...].astype(BF16)

    shift = mod_ref[0:1, :]
    scale = mod_ref[1:2, :]
    h = (x_ref[...] * (1.0 + scale) + shift).astype(BF16)
    o_ref[...] = _dot(h, wbf_ref[...]).astype(o_ref.dtype)


def _in_proj(dm, layer, x, mod, w_in):
    tm, tn = IN_PROJ_ROWS, IN_PROJ_COLS
    d = dm.d
    return pl.pallas_call(
        _in_proj_kernel,
        out_shape=jax.ShapeDtypeStruct((dm.n, dm.in_w), BF16),
        grid=(dm.in_w // tn, dm.n // tm),
        in_specs=[
            pl.BlockSpec((tm, d), lambda j, i: (i, 0)),
            pl.BlockSpec((None, None, N_MOD, d), lambda j, i: (layer, dm.group_of_tile(i, tm), 0, 0)),
            pl.BlockSpec((None, d, tn), lambda j, i: (layer, 0, j)),
        ],
        out_specs=pl.BlockSpec((tm, tn), lambda j, i: (i, j)),
        scratch_shapes=[pltpu.VMEM((d, tn), BF16)],
        compiler_params=_params(("arbitrary", "arbitrary")),
        name="in_proj",
    )(x, mod, w_in)


def _rope(x, cos, sin_signed):
    lane = lax.broadcasted_iota(jnp.int32, x.shape, 1)
    nxt = pltpu.roll(x, HEAD_DIM - 1, 1)
    prv = pltpu.roll(x, 1, 1)
    partner = jnp.where((lane & 1) == 0, nxt, prv)
    return x * cos + partner * sin_signed


def _rms(x, g):
    return x * lax.rsqrt(jnp.mean(x * x, axis=-1, keepdims=True) + NORM_EPS) * g


def _qk_prep_kernel(q_ref, k_ref, v_ref, cos_ref, sin_ref, qg_ref, kg_ref,
                    qo_ref, ko_ref, kn_ref, vo_ref):
    cos = cos_ref[...]
    sin = sin_ref[...]
    qg = qg_ref[...]
    kg = kg_ref[...]
    scale = HEAD_DIM ** -0.5
    for h in range(N_HEADS):
        cols = slice(h * HEAD_DIM, (h + 1) * HEAD_DIM)
        qn = _rms(q_ref[:, cols].astype(F32), qg)
        qo_ref[:, cols] = (_rope(qn, cos, sin) * scale).astype(qo_ref.dtype)
    for h in range(N_KV_HEADS):
        cols = slice(h * HEAD_DIM, (h + 1) * HEAD_DIM)
        kn = _rms(k_ref[:, cols].astype(F32), kg)
        kn_ref[:, cols] = kn
        ko_ref[:, cols] = _rope(kn, cos, sin).astype(ko_ref.dtype)
    vo_ref[...] = v_ref[...].astype(F32)


def _rope_tables(dm, tm):
    t = dm.ts
    rows = t // GRID_W
    row = jnp.broadcast_to(jnp.arange(rows, dtype=F32)[:, None], (rows, GRID_W)).reshape(t)
    col = jnp.broadcast_to(jnp.arange(GRID_W, dtype=F32)[None, :], (rows, GRID_W)).reshape(t)
    half = HEAD_DIM // 2
    inv_freq = ROPE_THETA ** (-jnp.arange(0, half, 2, dtype=F32) / half)
    ang = jnp.concatenate([row[:, None] * inv_freq, col[:, None] * inv_freq], axis=-1)
    cos = jnp.repeat(jnp.cos(ang), 2, axis=-1)
    sin = jnp.repeat(jnp.sin(ang), 2, axis=-1)
    sign = jnp.tile(jnp.array([-1.0, 1.0], F32), half)
    cos = jnp.concatenate([jnp.ones((tm, HEAD_DIM), F32), cos], axis=0)
    sin = jnp.concatenate([jnp.zeros((tm, HEAD_DIM), F32), sin * sign], axis=0)
    return cos, sin


def _qk_prep(dm, layer, proj, cos, sin, q_norm, k_norm):
    tm = ROW_TILE
    npt = dm.np_ // tm
    tps = dm.ts // tm

    def tab(i):
        return (jnp.where(i < npt, 0, 1 + (i - npt) % tps), 0)

    return pl.pallas_call(
        _qk_prep_kernel,
        out_shape=(
            jax.ShapeDtypeStruct((dm.n, ATTN_WIDTH), BF16),
            jax.ShapeDtypeStruct((dm.n, KV_WIDTH), BF16),
            jax.ShapeDtypeStruct((dm.n, KV_WIDTH), F32),
            jax.ShapeDtypeStruct((dm.n, KV_WIDTH), F32),
        ),
        grid=(dm.n // tm,),
        in_specs=[
            pl.BlockSpec((tm, ATTN_WIDTH), lambda i: (i, 0)),
            pl.BlockSpec((tm, KV_WIDTH), lambda i: (i, dm.off_k // KV_WIDTH)),
            pl.BlockSpec((tm, KV_WIDTH), lambda i: (i, dm.off_v // KV_WIDTH)),
            pl.BlockSpec((tm, HEAD_DIM), tab),
            pl.BlockSpec((tm, HEAD_DIM), tab),
            pl.BlockSpec((None, 1, HEAD_DIM), lambda i: (layer, 0, 0)),
            pl.BlockSpec((None, 1, HEAD_DIM), lambda i: (layer, 0, 0)),
        ],
        out_specs=(
            pl.BlockSpec((tm, ATTN_WIDTH), lambda i: (i, 0)),
            pl.BlockSpec((tm, KV_WIDTH), lambda i: (i, 0)),
            pl.BlockSpec((tm, KV_WIDTH), lambda i: (i, 0)),
            pl.BlockSpec((tm, KV_WIDTH), lambda i: (i, 0)),
        ),
        compiler_params=_params(("arbitrary",)),
        name="qk_prep",
    )(proj, proj, proj, cos, sin, q_norm, k_norm)


def _ctx_attn_kernel(q_ref, k_ref, v_ref, o_ref):
    for h in range(N_HEADS):
        kv = h // GROUP
        cols = slice(h * HEAD_DIM, (h + 1) * HEAD_DIM)
        kcols = slice(kv * HEAD_DIM, (kv + 1) * HEAD_DIM)
        s = _dot_nt(q_ref[:, cols], k_ref[:, kcols])
        m = jnp.max(s, axis=-1, keepdims=True)
        p = jnp.exp(s - m)
        l = jnp.sum(p, axis=-1, keepdims=True)
        o = _dot(p.astype(BF16), v_ref[:, kcols]) / l
        o_ref[:, cols] = o.astype(o_ref.dtype)


def _ctx_attn(dm, q, k, proj):
    t = dm.tp
    return pl.pallas_call(
        _ctx_attn_kernel,
        out_shape=jax.ShapeDtypeStruct((dm.np_, ATTN_WIDTH), BF16),
        grid=(dm.bp,),
        in_specs=[
            pl.BlockSpec((t, ATTN_WIDTH), lambda b: (b, 0)),
            pl.BlockSpec((t, KV_WIDTH), lambda b: (b, 0)),
            pl.BlockSpec((t, KV_WIDTH), lambda b: (b, dm.off_v // KV_WIDTH)),
        ],
        out_specs=pl.BlockSpec((t, ATTN_WIDTH), lambda b: (b, 0)),
        compiler_params=_params(("arbitrary",)),
        name="ctx_attn",
    )(q, k, proj)


def _lat_attn_kernel(q_ref, k_ref, v_ref, ck_ref, cv_ref, o_ref):
    ck = ck_ref[...].astype(BF16)
    cv = cv_ref[...].astype(BF16)
    for h in range(N_HEADS):
        kv = h // GROUP
        cols = slice(h * HEAD_DIM, (h + 1) * HEAD_DIM)
        kcols = slice(kv * HEAD_DIM, (kv + 1) * HEAD_DIM)
        qh = q_ref[:, cols]
        s_ctx = _dot_nt(qh, ck[:, kcols])
        s_lat = _dot_nt(qh, k_ref[:, kcols])
        m = jnp.maximum(jnp.max(s_ctx, axis=-1, keepdims=True),
                        jnp.max(s_lat, axis=-1, keepdims=True))
        p_ctx = jnp.exp(s_ctx - m)
        p_lat = jnp.exp(s_lat - m)
        l = jnp.sum(p_ctx, axis=-1, keepdims=True) + jnp.sum(p_lat, axis=-1, keepdims=True)
        o = _dot(p_ctx.astype(BF16), cv[:, kcols]) + _dot(p_lat.astype(BF16), v_ref[:, kcols])
        o_ref[:, cols] = (o / l).astype(o_ref.dtype)


def _lat_attn(dm, layer, q, k, proj, cache_k, cache_v):
    tq = ROW_TILE
    ts = dm.ts
    qb = ts // tq
    return pl.pallas_call(
        _lat_attn_kernel,
        out_shape=jax.ShapeDtypeStruct((dm.ns, ATTN_WIDTH), BF16),
        grid=(dm.bs, qb),
        in_specs=[
            pl.BlockSpec((tq, ATTN_WIDTH), lambda b, i: (dm.np_ // tq + b * qb + i, 0)),
            pl.BlockSpec((ts, KV_WIDTH), lambda b, i: (dm.np_ // ts + b, 0)),
            pl.BlockSpec((ts, KV_WIDTH), lambda b, i: (dm.np_ // ts + b, dm.off_v // KV_WIDTH)),
            pl.BlockSpec((None, None, dm.past, KV_WIDTH), lambda b, i: (b, layer, 0, 0)),
            pl.BlockSpec((None, None, dm.past, KV_WIDTH), lambda b, i: (b, layer, 0, 0)),
        ],
        out_specs=pl.BlockSpec((tq, ATTN_WIDTH), lambda b, i: (b * qb + i, 0)),
        compiler_params=_params(("arbitrary", "arbitrary")),
        name="lat_attn",
    )(q, k, proj, cache_k, cache_v)


def _local_kernel(dm, a_ref, ap_ref, an_ref, g_ref, gp_ref, gn_ref, z_ref, zp_ref, zn_ref,
                  u_ref, v_ref, cw_ref, cb_ref, clg_ref, clb_ref, slg_ref, slb_ref,
                  sw_ref, sbt_ref, conv_o, pool_o, sgu_o, ubuf, zbuf):
    t = ROW_TILE
    i = pl.program_id(0)
    pos, cnt = dm.seq_tile_pos(i, t)
    has_prev = (pos > 0).astype(F32)
    has_next = (pos < cnt - 1).astype(F32)

    def glu(a, g):
        return a[...].astype(F32) * _sigmoid(g[...].astype(F32))

    ubuf[0:HALO, :] = glu(ap_ref, gp_ref) * has_prev
    ubuf[HALO:HALO + t, :] = glu(a_ref, g_ref)
    ubuf[HALO + t:, :] = glu(an_ref, gn_ref) * has_next
    zbuf[0:HALO, :] = zp_ref[...].astype(F32) * has_prev
    zbuf[HALO:HALO + t, :] = z_ref[...].astype(F32)
    zbuf[HALO + t:, :] = zn_ref[...].astype(F32) * has_next

    half_rows = t // 2
    centre = CONV_KERNEL // 2
    for r in range(2):
        pieces = []
        for c in range(dm.cw // LANES):
            cols = slice(c * LANES, (c + 1) * LANES)
            acc = jnp.zeros((half_rows, LANES), F32)
            for k in range(CONV_KERNEL):
                start = HALO + r * half_rows + k - centre
                acc = acc + cw_ref[k:k + 1, cols] * ubuf[start:start + half_rows, cols]
            pieces.append(acc)
        y = jnp.concatenate(pieces, axis=1) + cb_ref[...]
        y = _layer_norm(y, clg_ref[...], clb_ref[...])
        conv_o[r * half_rows:(r + 1) * half_rows, :] = (y * _sigmoid(y)).astype(conv_o.dtype)

    seq_len = jnp.where(i < dm.np_ // t, dm.tp, dm.ts)
    tok = pos * t + lax.broadcasted_iota(jnp.int32, (t, LANES), 0)
    for gi, w in enumerate(POOL_WINDOWS):
        cols = slice(gi * LANES, (gi + 1) * LANES)
        acc = jnp.zeros((t, LANES), F32)
        for j in range(-(w // 2), w - w // 2):
            acc = acc + zbuf[HALO + j:HALO + j + t, cols]
        lo = jnp.maximum(tok - w // 2, 0)
        hi = jnp.minimum(tok - w // 2 + w, seq_len)
        mean = acc / (hi - lo).astype(F32)
        pool_o[:, cols] = (mean - zbuf[HALO:HALO + t, cols]).astype(pool_o.dtype)

    vn = _layer_norm(v_ref[...].astype(F32), slg_ref[...], slb_ref[...]).astype(BF16)
    for gi in range(SGU_GROUPS):
        cols = slice(gi * CHUNK, (gi + 1) * CHUNK)
        w = sw_ref[gi].astype(BF16)
        bias = sbt_ref[:, gi:gi + 1]
        for c in range(t // CHUNK):
            rows = slice(c * CHUNK, (c + 1) * CHUNK)
            mixed = _dot(w, vn[rows, cols]) + bias
            sgu_o[rows, cols] = (u_ref[rows, cols].astype(F32) * mixed).astype(sgu_o.dtype)


def _local_branches(dm, layer, proj, conv_w, conv_b, conv_ln_g, conv_ln_b,
                    sgu_ln_g, sgu_ln_b, sgu_w, sgu_bt):
    t = ROW_TILE
    cw = dm.cw
    hb = t // HALO
    last = dm.n // HALO - 1
    ca = dm.off_a // cw

    def cur(c):
        return pl.BlockSpec((t, cw), lambda i: (i, c))

    def prev(c):
        return pl.BlockSpec((HALO, cw), lambda i: (jnp.maximum(i * hb - 1, 0), c))

    def nxt(c):
        return pl.BlockSpec((HALO, cw), lambda i: (jnp.minimum((i + 1) * hb, last), c))

    def vec():
        return pl.BlockSpec((None, 1, cw), lambda i: (layer, 0, 0))

    out = jax.ShapeDtypeStruct((dm.n, cw), BF16)
    return pl.pallas_call(
        functools.partial(_local_kernel, dm),
        out_shape=(out, out, out),
        grid=(dm.n // t,),
        in_specs=[
            cur(ca), prev(ca), nxt(ca),
            cur(ca + 1), prev(ca + 1), nxt(ca + 1),
            cur(ca + 2), prev(ca + 2), nxt(ca + 2),
            cur(ca + 3), cur(ca + 4),
            pl.BlockSpec((None, CONV_KERNEL, cw), lambda i: (layer, 0, 0)),
            vec(), vec(), vec(), vec(), vec(),
            pl.BlockSpec((None, SGU_GROUPS, CHUNK, CHUNK), lambda i: (layer, 0, 0, 0)),
            pl.BlockSpec((None, CHUNK, SGU_GROUPS), lambda i: (layer, 0, 0)),
        ],
        out_specs=(pl.BlockSpec((t, cw), lambda i: (i, 0)),) * 3,
        scratch_shapes=[pltpu.VMEM((t + 2 * HALO, cw), F32), pltpu.VMEM((t + 2 * HALO, cw), F32)],
        compiler_params=_params(("arbitrary",)),
        name="local_branches",
    )(proj, proj, proj, proj, proj, proj, proj, proj, proj, proj, proj,
      conv_w, conv_b, conv_ln_g, conv_ln_b, sgu_ln_g, sgu_ln_b, sgu_w, sgu_bt)


def _merge_kernel(dm, o_ref, cv_ref, pl_ref, sg_ref, g0_ref, g1_ref, g2_ref, g3_ref,
                  wa_ref, wc_ref, wp_ref, ws_ref, ps_ref, m_ref):
    cw = dm.cw
    for c in range(dm.d // cw):
        cols = slice(c * cw, (c + 1) * cw)
        ya = _dot(o_ref[...], wa_ref[:, cols])
        yb = _dot(cv_ref[...], wc_ref[:, cols])
        yc = _dot(pl_ref[:, c * LANES:(c + 1) * LANES], wp_ref[c]) * ps_ref[:, cols]
        yd = _dot(sg_ref[...], ws_ref[:, cols])
        merged = (_sigmoid(g0_ref[:, cols].astype(F32)) * ya
                  + _sigmoid(g1_ref[:, cols].astype(F32)) * yb
                  + _sigmoid(g2_ref[:, cols].astype(F32)) * yc
                  + _sigmoid(g3_ref[:, cols].astype(F32)) * yd)
        m_ref[:, cols] = merged.astype(m_ref.dtype)


def _merge(dm, layer, attn_o, conv_h, pool_h, sgu_h, proj, wa, wc, wp, ws, pool_scale):
    tm = ROW_TILE
    d, cw = dm.d, dm.cw
    g0 = dm.off_gates // d
    assert d // cw == POOL_GROUPS and wp.shape[-1] == cw

    def gate(b):
        return pl.BlockSpec((tm, d), lambda i: (i, g0 + b))

    return pl.pallas_call(
        functools.partial(_merge_kernel, dm),
        out_shape=jax.ShapeDtypeStruct((dm.n, d), BF16),
        grid=(dm.n // tm,),
        in_specs=[
            pl.BlockSpec((tm, ATTN_WIDTH), lambda i: (i, 0)),
            pl.BlockSpec((tm, cw), lambda i: (i, 0)),
            pl.BlockSpec((tm, cw), lambda i: (i, 0)),
            pl.BlockSpec((tm, cw), lambda i: (i, 0)),
            gate(0), gate(1), gate(2), gate(3),
            pl.BlockSpec((None, ATTN_WIDTH, d), lambda i: (layer, 0, 0)),
            pl.BlockSpec((None, cw, d), lambda i: (layer, 0, 0)),
            pl.BlockSpec((None, POOL_GROUPS, LANES, cw), lambda i: (layer, 0, 0, 0)),
            pl.BlockSpec((None, cw, d), lambda i: (layer, 0, 0)),
            pl.BlockSpec((None, 1, d), lambda i: (layer, 0, 0)),
        ],
        out_specs=pl.BlockSpec((tm, d), lambda i: (i, 0)),
        compiler_params=_params(("arbitrary",)),
        name="merge",
    )(attn_o, conv_h, pool_h, sgu_h, proj, proj, proj, proj, wa, wc, wp, ws, pool_scale)


def _out_proj_kernel(alpha, m_ref, x_ref, mod_ref, w_ref, lg_ref, lb_ref, wr_ref, br_ref,
                     x1_ref, h2_ref, idx_ref, comb_ref):
    gate1 = mod_ref[2:3, :]
    shift2 = mod_ref[3:4, :]
    scale2 = mod_ref[4:5, :]
    y = _dot(m_ref[...], w_ref[...])
    x1 = _layer_norm(alpha * x_ref[...] + gate1 * y, lg_ref[...], lb_ref[...])
    x1_ref[...] = x1
    h2 = x1 * (1.0 + scale2) + shift2
    h2_hi = h2.astype(BF16)
    h2_ref[...] = _pack_pairs(h2)

    h2_lo = (h2 - h2_hi.astype(F32)).astype(BF16)
    wr = wr_ref[...]
    wr_hi = wr.astype(BF16)
    wr_lo = (wr - wr_hi.astype(F32)).astype(BF16)
    logits = _dot(h2_hi, wr_hi) + _dot(h2_hi, wr_lo) + _dot(h2_lo, wr_hi) + br_ref[...]

    lane = lax.broadcasted_iota(jnp.int32, logits.shape, 1)
    idx_out = jnp.zeros(logits.shape, jnp.int32)
    val_out = jnp.zeros(logits.shape, F32)
    top = None
    den = jnp.zeros((logits.shape[0], 1), F32)
    for k in range(TOP_K):
        m = jnp.max(logits, axis=-1, keepdims=True)
        ik = jnp.min(jnp.where(logits == m, lane, LANES), axis=-1, keepdims=True)
        if top is None:
            top = m
        e = jnp.exp(m - top)
        den = den + e
        idx_out = jnp.where(lane == k, ik, idx_out)
        val_out = jnp.where(lane == k, e, val_out)
        logits = jnp.where(lane == ik, -jnp.inf, logits)
    idx_ref[...] = idx_out
    comb_ref[...] = val_out / den


def _out_proj(dm, layer, merged, x, mod, w_out, ln_g, ln_b, w_router, b_router):
    tm = ROW_TILE
    d = dm.d
    alpha = (2 * dm.depth) ** 0.25

    def vec():
        return pl.BlockSpec((None, 1, d), lambda i: (layer, 0, 0))

    return pl.pallas_call(
        functools.partial(_out_proj_kernel, alpha),
        out_shape=(
            jax.ShapeDtypeStruct((dm.n, d), F32),
            jax.ShapeDtypeStruct((dm.n, d // 2), jnp.uint32),
            jax.ShapeDtypeStruct((dm.n, LANES), jnp.int32),
            jax.ShapeDtypeStruct((dm.n, LANES), F32),
        ),
        grid=(dm.n // tm,),
        in_specs=[
            pl.BlockSpec((tm, d), lambda i: (i, 0)),
            pl.BlockSpec((tm, d), lambda i: (i, 0)),
            pl.BlockSpec((None, None, N_MOD, d), lambda i: (layer, dm.group_of_tile(i, tm), 0, 0)),
            pl.BlockSpec((None, d, d), lambda i: (layer, 0, 0)),
            vec(), vec(),
            pl.BlockSpec((None, d, LANES), lambda i: (layer, 0, 0)),
            pl.BlockSpec((None, 1, LANES), lambda i: (layer, 0, 0)),
        ],
        out_specs=(
            pl.BlockSpec((tm, d), lambda i: (i, 0)),
            pl.BlockSpec((tm, d // 2), lambda i: (i, 0)),
            pl.BlockSpec((tm, LANES), lambda i: (i, 0)),
            pl.BlockSpec((tm, LANES), lambda i: (i, 0)),
        ),
        compiler_params=_params(("arbitrary",)),
        name="out_proj",
    )(merged, x, mod, w_out, ln_g, ln_b, w_router, b_router)


def _moe_live_chunks(nv, fn):
    pair = 2 * MOE_SUB
    for base in range(0, MOE_SUPER, pair):
        both = nv > base + MOE_SUB
        only_first = jnp.logical_and(nv > base, jnp.logical_not(both))
        pl.when(both)(functools.partial(fn, base, pair))
        pl.when(only_first)(functools.partial(fn, base, MOE_SUB))


def _moe_kernel(nfa, be_ref, nv_ref, na_ref, x_ref, wg_ref, bg_ref, wu_ref, bu_ref, wd_ref, bd_ref,
                o_ref, act_ref, wgu_ref, wdn_ref, xbf_ref):
    s = pl.program_id(0)
    j = pl.program_id(1)
    nv = nv_ref[s]
    tf = MOE_FF_TILE
    active = s < na_ref[0]

    @pl.when(jnp.logical_and(active, j == 0))
    def _():
        half = xbf_ref.shape[1] // 2

        def unpack(c, carry):
            rows = pl.ds(pl.multiple_of(c * LANES, LANES), LANES)
            lo, hi = _unpack_pairs(x_ref[rows, :])
            xbf_ref[rows, :half] = lo.astype(BF16)
            xbf_ref[rows, half:] = hi.astype(BF16)
            return carry

        lax.fori_loop(0, MOE_SUPER // LANES, unpack, 0)

    @pl.when(jnp.logical_and(active, j < nfa))
    def _():
        wgu_ref[:, :tf] = wg_ref[...].astype(BF16)
        wgu_ref[:, tf:] = wu_ref[...].astype(BF16)
        bg = bg_ref[...]
        bu = bu_ref[...]
        col0 = pl.multiple_of(j * tf, tf)

        def chunk(base, m):
            gu = _dot(xbf_ref[base:base + m, :], wgu_ref[...])
            gt = jnp.minimum(gu[:, :tf] + bg, SWIGLU_LIMIT)
            up = jnp.clip(gu[:, tf:] + bu, -SWIGLU_LIMIT, SWIGLU_LIMIT)
            act = gt * _sigmoid(SWIGLU_ALPHA * gt) * (up + 1.0)
            act_ref[base:base + m, pl.ds(col0, tf)] = act.astype(BF16)

        _moe_live_chunks(nv, chunk)

    @pl.when(j >= nfa)
    def _():
        @pl.when(active)
        def _():
            wdn_ref[...] = wd_ref[...].astype(BF16)

        bd = bd_ref[...]

        def chunk(base, m):
            y = _dot(act_ref[base:base + m, :], wdn_ref[...]) + bd
            o_ref[base:base + m, :] = _pack_pairs(y)

        _moe_live_chunks(nv, chunk)
        for base in range(0, MOE_SUPER, MOE_SUB):
            @pl.when(nv <= base)
            def _():
                o_ref[base:base + MOE_SUB, :] = jnp.zeros((MOE_SUB, o_ref.shape[1]), o_ref.dtype)


def _moe(dm, layer, x_sorted, blk_e, blk_nv, n_act, w_gate, b_gate, w_up, b_up, w_down, b_down):
    d, ff, tf, tn, r = dm.d, dm.ff, MOE_FF_TILE, MOE_OUT_TILE, MOE_SUPER
    nsb = x_sorted.shape[0] // r
    nfa = ff // tf
    nfb = d // tn

    def blk(s, na):
        return jnp.minimum(s, na[0] - 1)

    def col_a(s, j, na):
        return jnp.where(s < na[0], jnp.minimum(j, nfa - 1), nfa - 1)

    def col_b(s, j, na):
        return jnp.where(s < na[0], jnp.maximum(j - nfa, 0), nfb - 1)

    grid_spec = pltpu.PrefetchScalarGridSpec(
        num_scalar_prefetch=3,
        grid=(nsb, nfa + nfb),
        in_specs=[
            pl.BlockSpec((r, d // 2), lambda s, j, be, nv, na: (blk(s, na), 0)),
            pl.BlockSpec((None, None, d, tf), lambda s, j, be, nv, na: (layer, be[s], 0, col_a(s, j, na))),
            pl.BlockSpec((None, None, 1, tf), lambda s, j, be, nv, na: (layer, be[s], 0, col_a(s, j, na))),
            pl.BlockSpec((None, None, d, tf), lambda s, j, be, nv, na: (layer, be[s], 0, col_a(s, j, na))),
            pl.BlockSpec((None, None, 1, tf), lambda s, j, be, nv, na: (layer, be[s], 0, col_a(s, j, na))),
            pl.BlockSpec((None, None, ff, tn), lambda s, j, be, nv, na: (layer, be[s], 0, col_b(s, j, na))),
            pl.BlockSpec((None, None, 1, tn), lambda s, j, be, nv, na: (layer, be[s], 0, col_b(s, j, na))),
        ],
        out_specs=pl.BlockSpec((r, tn // 2), lambda s, j, be, nv, na: (s, jnp.maximum(j - nfa, 0))),
        scratch_shapes=[
            pltpu.VMEM((r, ff), BF16),
            pltpu.VMEM((d, 2 * tf), BF16),
            pltpu.VMEM((ff, tn), BF16),
            pltpu.VMEM((r, d), BF16),
        ],
    )
    return pl.pallas_call(
        functools.partial(_moe_kernel, nfa),
        out_shape=jax.ShapeDtypeStruct((nsb * r, d // 2), jnp.uint32),
        grid_spec=grid_spec,
        compiler_params=_params(("arbitrary", "arbitrary")),
        name="moe_ffn",
    )(blk_e, blk_nv, n_act, x_sorted, w_gate, b_gate, w_up, b_up, w_down, b_down)


def _route(dm, top_idx, n_super):
    e, r = dm.n_exp, MOE_SUPER
    flat_e = top_idx.reshape(-1)
    onehot = (flat_e[:, None] == jnp.arange(e, dtype=jnp.int32)[None, :]).astype(jnp.int32)
    rank = jnp.take_along_axis(jnp.cumsum(onehot, axis=0), flat_e[:, None], axis=1)[:, 0] - 1
    counts = jnp.sum(onehot, axis=0)
    nsb_e = (counts + r - 1) // r
    sb_end = jnp.cumsum(nsb_e)
    sb_start = sb_end - nsb_e
    dest = sb_start[flat_e] * r + rank
    n_act = sb_end[-1]
    sidx = jnp.arange(n_super, dtype=jnp.int32)
    blk_e = jnp.clip(jnp.searchsorted(sb_end, jnp.minimum(sidx, n_act - 1), side='right'), 0, e - 1)
    blk_e = blk_e.astype(jnp.int32)
    blk_nv = jnp.clip(counts[blk_e] - (sidx - sb_start[blk_e]) * r, 0, r)
    blk_nv = jnp.where(sidx < n_act, blk_nv, 0).astype(jnp.int32)
    return dest.astype(jnp.int32), blk_e, blk_nv, n_act.astype(jnp.int32).reshape(1)


def _combine_kernel(alpha, y_ref, comb_ref, x_ref, mod_ref, lg_ref, lb_ref, o_ref):
    gate2 = mod_ref[5:6, :]
    comb = comb_ref[...]
    hw = MOE_OUT_TILE // 2
    slabs = []
    for b in range(y_ref.shape[2] // hw):
        lo_sum = None
        hi_sum = None
        for k in range(TOP_K):
            lo, hi = _unpack_pairs(y_ref[k, :, b * hw:(b + 1) * hw])
            wk = comb[:, k:k + 1]
            lo_sum = wk * lo if lo_sum is None else lo_sum + wk * lo
            hi_sum = wk * hi if hi_sum is None else hi_sum + wk * hi
        slabs += [lo_sum, hi_sum]
    ff = jnp.concatenate(slabs, axis=1)
    o_ref[...] = _layer_norm(alpha * x_ref[...] + gate2 * ff, lg_ref[...], lb_ref[...])


def _combine(dm, layer, y_tok, comb, x1, mod, ln_g, ln_b):
    tm = ROW_TILE
    d = dm.d
    alpha = (2 * dm.depth) ** 0.25
    return pl.pallas_call(
        functools.partial(_combine_kernel, alpha),
        out_shape=jax.ShapeDtypeStruct((dm.n, d), F32),
        grid=(dm.n // tm,),
        in_specs=[
            pl.BlockSpec((TOP_K, tm, d // 2), lambda i: (0, i, 0)),
            pl.BlockSpec((tm, LANES), lambda i: (i, 0)),
            pl.BlockSpec((tm, d), lambda i: (i, 0)),
            pl.BlockSpec((None, None, N_MOD, d), lambda i: (layer, dm.group_of_tile(i, tm), 0, 0)),
            pl.BlockSpec((None, 1, d), lambda i: (layer, 0, 0)),
            pl.BlockSpec((None, 1, d), lambda i: (layer, 0, 0)),
        ],
        out_specs=pl.BlockSpec((tm, d), lambda i: (i, 0)),
        compiler_params=_params(("arbitrary",)),
        name="combine",
    )(y_tok, comb, x1, mod, ln_g, ln_b)


def kernel(x_prompt, x_sample, cache_k, cache_v, c, c_ctx, w_mod, b_mod, w_in, q_norm, k_norm,
           w_attn_o, conv_w, conv_b, conv_ln_g, conv_ln_b, w_conv_o, w_pool, pool_scale,
           sgu_ln_g, sgu_ln_b, sgu_w, sgu_b, w_sgu_o, w_out, ln1_g, ln1_b, ln2_g, ln2_b,
           w_router, b_router, w_gate, b_gate, w_up, b_up, w_down, b_down):
    bp, tp, d = x_prompt.shape
    bs, ts, _ = x_sample.shape
    depth, n_exp, _, ff = w_gate.shape
    past = cache_k.shape[2]
    dm = _Dims(bp, tp, bs, ts, d, past, depth, n_exp, ff)
    assert 1 + bs <= SUBLANES and n_exp <= LANES

    x = jnp.concatenate([x_prompt.reshape(dm.np_, d), x_sample.reshape(dm.ns, d)], axis=0)
    cond = jnp.concatenate([c_ctx[None, :], c, jnp.zeros((SUBLANES - 1 - bs, d), F32)], axis=0)
    mod = _modulation(cond, w_mod, b_mod).reshape(depth, SUBLANES, N_MOD, d)

    cos, sin = _rope_tables(dm, ROW_TILE)
    ck = cache_k.reshape(bs, depth, past, KV_WIDTH)
    cv = cache_v.reshape(bs, depth, past, KV_WIDTH)

    def row(p):
        return p.reshape(depth, 1, p.shape[-1])

    wa, wc, wp, ws, wo = (w.astype(BF16) for w in (w_attn_o, w_conv_o, w_pool, w_sgu_o, w_out))
    sgu_bt = jnp.swapaxes(sgu_b, 1, 2)
    w_router_p = jnp.pad(w_router, ((0, 0), (0, 0), (0, LANES - n_exp)))
    b_router_p = jnp.pad(b_router, ((0, 0), (0, LANES - n_exp)), constant_values=-jnp.inf)
    b_router_p = b_router_p.reshape(depth, 1, LANES)
    bg, bu = b_gate.reshape(depth, n_exp, 1, ff), b_up.reshape(depth, n_exp, 1, ff)
    bd = b_down.reshape(depth, n_exp, 1, d)

    nk = dm.n * TOP_K
    n_super = nk // MOE_SUPER + n_exp
    ks, vs = [], []
    for l in range(depth):
        proj = _in_proj(dm, l, x, mod, w_in)
        q, k, kn, v32 = _qk_prep(dm, l, proj, cos, sin, row(q_norm), row(k_norm))
        ks.append(kn[:dm.np_])
        vs.append(v32[:dm.np_])
        attn_o = jnp.concatenate(
            [_ctx_attn(dm, q, k, proj), _lat_attn(dm, l, q, k, proj, ck, cv)], axis=0)
        conv_h, pool_h, sgu_h = _local_branches(
            dm, l, proj, conv_w, row(conv_b), row(conv_ln_g), row(conv_ln_b),
            row(sgu_ln_g), row(sgu_ln_b), sgu_w, sgu_bt)
        merged = _merge(dm, l, attn_o, conv_h, pool_h, sgu_h, proj, wa, wc, wp, ws, row(pool_scale))
        x1, h2, top_idx, comb = _out_proj(dm, l, merged, x, mod, wo, row(ln1_g), row(ln1_b),
                                          w_router_p, b_router_p)
        dest, blk_e, blk_nv, n_act = _route(dm, top_idx[:, :TOP_K], n_super)
        slot_tok = jnp.zeros((n_super * MOE_SUPER,), jnp.int32).at[dest].set(
            jnp.arange(nk, dtype=jnp.int32) // TOP_K)
        x_sorted = _gather_rows(h2, slot_tok)
        y_sorted = _moe(dm, l, x_sorted, blk_e, blk_nv, n_act, w_gate, bg, w_up, bu, w_down, bd)
        y_tok = _gather_rows(y_sorted, dest.reshape(dm.n, TOP_K).T.reshape(-1))
        y_tok = y_tok.reshape(TOP_K, dm.n, d // 2)
        x = _combine(dm, l, y_tok, comb, x1, mod, row(ln2_g), row(ln2_b))

    y_prompt = x[:dm.np_].reshape(bp, tp, d)
    y_sample = x[dm.np_:].reshape(bs, ts, d)
    state_k = jnp.stack(ks, axis=1).reshape(bp, tp, depth, N_KV_HEADS, HEAD_DIM).swapaxes(1, 2)
    state_v = jnp.stack(vs, axis=1).reshape(bp, tp, depth, N_KV_HEADS, HEAD_DIM).swapaxes(1, 2)
    return (y_prompt, y_sample, state_k, state_v)
```

```python
import functools

import jax
import jax.numpy as jnp
from jax import lax
from jax.experimental import pallas as pl
from jax.experimental.pallas import tpu as pltpu
from jax.experimental.pallas import tpu_sc as plsc

F32 = jnp.float32
BF16 = jnp.bfloat16

N_HEADS = 8
N_KV_HEADS = 2
HEAD_DIM = 128
GRID_W = 64
ROPE_THETA = 10000.0
CONV_KERNEL = 31
POOL_WINDOWS = (2, 4, 8, 16)
POOL_GROUPS = 4
SGU_GROUPS = 4
CHUNK = 128
N_BRANCHES = 4
TOP_K = 4
SWIGLU_LIMIT = 7.0
SWIGLU_ALPHA = 1.702
NORM_EPS = 1e-6
N_MOD = 6

ATTN_WIDTH = N_HEADS * HEAD_DIM
KV_WIDTH = N_KV_HEADS * HEAD_DIM
GROUP = N_HEADS // N_KV_HEADS

LANES = 128
SUBLANES = 8
VMEM_LIMIT_BYTES = 56 * 1024 * 1024

ROW_TILE = 256
HALO = 16
IN_PROJ_ROWS = 512
IN_PROJ_COLS = 1024
MOD_COLS = 1024
MOE_SUPER = 1024
MOE_SUB = 256
MOE_FF_TILE = 512
MOE_OUT_TILE = 512
GATHER_WINDOW = 128
GATHER_WORDS = 256


def _dot(a, b):
    return jnp.dot(a, b, preferred_element_type=F32)


def _dot_nt(a, b):
    return lax.dot_general(a, b, (((1,), (1,)), ((), ())), preferred_element_type=F32)


def _layer_norm(x, g, b):
    mu = jnp.mean(x, axis=-1, keepdims=True)
    xc = x - mu
    var = jnp.mean(xc * xc, axis=-1, keepdims=True)
    return xc * lax.rsqrt(var + NORM_EPS) * g + b


def _sigmoid(x):
    return 1.0 / (1.0 + jnp.exp(-x))


def _params(sem, vmem=VMEM_LIMIT_BYTES):
    return pltpu.CompilerParams(dimension_semantics=sem, vmem_limit_bytes=vmem)


_HIGH_HALF = 0xFFFF0000


def _pack_pairs(x):
    w = x.shape[1] // 2
    lo = lax.bitcast_convert_type(x[:, :w].astype(BF16).astype(F32), jnp.uint32)
    hi = lax.bitcast_convert_type(x[:, w:].astype(BF16).astype(F32), jnp.uint32)
    return (lo >> 16) | (hi & jnp.uint32(_HIGH_HALF))


def _unpack_pairs(words):
    lo = lax.bitcast_convert_type(words << 16, F32)
    hi = lax.bitcast_convert_type(words & jnp.uint32(_HIGH_HALF), F32)
    return lo, hi


def _gather_rows(planes, idx):
    n_planes, n_rows, width = planes.shape
    assert planes.dtype.itemsize == 4 and width == GATHER_WORDS
    offsets = jnp.arange(n_planes, dtype=jnp.int32) * n_rows
    flat_idx = (offsets[:, None] + idx[None, :]).reshape(-1)
    out = _gather_pieces(planes.reshape(n_planes * n_rows, width), flat_idx)
    return out.reshape(n_planes, idx.shape[0], width)


def _gather_pieces(table, idx):
    n_idx = idx.shape[0]
    width = table.shape[1]
    assert n_idx % GATHER_WINDOW == 0
    mesh = plsc.VectorSubcoreMesh(core_axis_name="core", subcore_axis_name="subcore")

    @functools.partial(
        pl.kernel,
        out_type=jax.ShapeDtypeStruct((n_idx, width), table.dtype),
        mesh=mesh,
        scratch_types=[],
    )
    def gather_kernel(table_hbm, idx_hbm, out_hbm):
        def body(idx_vmem, out_vmem):
            pltpu.sync_copy(table_hbm.at[idx_vmem.at[0]], out_vmem)

        pltpu.emit_pipeline(
            body,
            grid=(n_idx // GATHER_WINDOW,),
            in_specs=[pl.BlockSpec((1, GATHER_WINDOW), lambda i: (0, i))],
            out_specs=[pl.BlockSpec((GATHER_WINDOW, width), lambda i: (i, 0))],
            core_axis_name=("core", "subcore"),
            dimension_semantics=(pltpu.PARALLEL,),
        )(idx_hbm, out_hbm)

    return gather_kernel(table, idx.reshape(1, n_idx))


class _Dims:
    def __init__(self, bp, tp, bs, ts, d, past, depth, n_exp, ff):
        self.bp, self.tp, self.bs, self.ts = bp, tp, bs, ts
        self.d, self.past, self.depth, self.n_exp, self.ff = d, past, depth, n_exp, ff
        self.np_ = bp * tp
        self.ns = bs * ts
        self.n = self.np_ + self.ns
        self.cw = d // 4
        self.in_w = ATTN_WIDTH + 2 * KV_WIDTH + 5 * self.cw + N_BRANCHES * d
        self.off_k = ATTN_WIDTH
        self.off_v = ATTN_WIDTH + KV_WIDTH
        self.off_a = ATTN_WIDTH + 2 * KV_WIDTH
        self.off_gates = self.off_a + 5 * self.cw
        assert self.cw == POOL_GROUPS * LANES == SGU_GROUPS * CHUNK
        assert self.off_a % self.cw == 0 and self.off_gates % d == 0
        assert tp % ROW_TILE == 0 and ts % ROW_TILE == 0 and self.np_ % ts == 0
        assert ts % GRID_W == 0

    def group_of_tile(self, i, tm):
        npt = self.np_ // tm
        return jnp.where(i < npt, 0, 1 + (i - npt) // (self.ts // tm))

    def seq_tile_pos(self, i, tm):
        npt = self.np_ // tm
        is_p = i < npt
        pos = jnp.where(is_p, i % (self.tp // tm), (i - npt) % (self.ts // tm))
        cnt = jnp.where(is_p, self.tp // tm, self.ts // tm)
        return pos, cnt


def _mod_kernel(c_ref, w_ref, b_ref, o_ref):
    c = c_ref[...]
    s = (c * _sigmoid(c)).astype(BF16)
    o_ref[...] = _dot(s, w_ref[...].astype(BF16)) + b_ref[...]


def _modulation(cond8, w_mod, b_mod):
    depth, d, width = w_mod.shape
    tn = MOD_COLS
    return pl.pallas_call(
        _mod_kernel,
        out_shape=jax.ShapeDtypeStruct((depth, SUBLANES, width), F32),
        grid=(depth, width // tn),
        in_specs=[
            pl.BlockSpec((SUBLANES, d), lambda l, j: (0, 0)),
            pl.BlockSpec((None, d, tn), lambda l, j: (l, 0, j)),
            pl.BlockSpec((None, 1, tn), lambda l, j: (l, 0, j)),
        ],
        out_specs=pl.BlockSpec((None, SUBLANES, tn), lambda l, j: (l, 0, j)),
        compiler_params=_params(("arbitrary", "arbitrary")),
        name="modulation",
    )(cond8, w_mod, b_mod.reshape(depth, 1, width))


def _in_proj_kernel(x_ref, mod_ref, w_ref, o_ref, wbf_ref):
    @pl.when(pl.program_id(1) == 0)
    def _():
        wbf_ref[...] = w_ref[...].astype(BF16)

    shift = mod_ref[0:1, :]
    scale = mod_ref[1:2, :]
    h = (x_ref[...] * (1.0 + scale) + shift).astype(BF16)
    o_ref[...] = _dot(h, wbf_ref[...]).astype(o_ref.dtype)


def _in_proj(dm, layer, x, mod, w_in):
    tm, tn = IN_PROJ_ROWS, IN_PROJ_COLS
    d = dm.d
    return pl.pallas_call(
        _in_proj_kernel,
        out_shape=jax.ShapeDtypeStruct((dm.n, dm.in_w), BF16),
        grid=(dm.in_w // tn, dm.n // tm),
        in_specs=[
            pl.BlockSpec((tm, d), lambda j, i: (i, 0)),
            pl.BlockSpec((None, None, N_MOD, d), lambda j, i: (layer, dm.group_of_tile(i, tm), 0, 0)),
            pl.BlockSpec((None, d, tn), lambda j, i: (layer, 0, j)),
        ],
        out_specs=pl.BlockSpec((tm, tn), lambda j, i: (i, j)),
        scratch_shapes=[pltpu.VMEM((d, tn), BF16)],
        compiler_params=_params(("arbitrary", "arbitrary")),
        name="in_proj",
    )(x, mod, w_in)


def _rope(x, cos, sin_signed):
    lane = lax.broadcasted_iota(jnp.int32, x.shape, 1)
    nxt = pltpu.roll(x, HEAD_DIM - 1, 1)
    prv = pltpu.roll(x, 1, 1)
    partner = jnp.where((lane & 1) == 0, nxt, prv)
    return x * cos + partner * sin_signed


def _rms(x, g):
    return x * lax.rsqrt(jnp.mean(x * x, axis=-1, keepdims=True) + NORM_EPS) * g


def _qk_prep_kernel(q_ref, k_ref, v_ref, cos_ref, sin_ref, qg_ref, kg_ref,
                    qo_ref, ko_ref, kn_ref, vo_ref):
    cos = cos_ref[...]
    sin = sin_ref[...]
    qg = qg_ref[...]
    kg = kg_ref[...]
    scale = HEAD_DIM ** -0.5
    for h in range(N_HEADS):
        cols = slice(h * HEAD_DIM, (h + 1) * HEAD_DIM)
        qn = _rms(q_ref[:, cols].astype(F32), qg)
        qo_ref[:, cols] = (_rope(qn, cos, sin) * scale).astype(qo_ref.dtype)
    for h in range(N_KV_HEADS):
        cols = slice(h * HEAD_DIM, (h + 1) * HEAD_DIM)
        kn = _rms(k_ref[:, cols].astype(F32), kg)
        kn_ref[:, cols] = kn
        ko_ref[:, cols] = _rope(kn, cos, sin).astype(ko_ref.dtype)
    vo_ref[...] = v_ref[...].astype(F32)


def _rope_tables(dm, tm):
    t = dm.ts
    rows = t // GRID_W
    row = jnp.broadcast_to(jnp.arange(rows, dtype=F32)[:, None], (rows, GRID_W)).reshape(t)
    col = jnp.broadcast_to(jnp.arange(GRID_W, dtype=F32)[None, :], (rows, GRID_W)).reshape(t)
    half = HEAD_DIM // 2
    inv_freq = ROPE_THETA ** (-jnp.arange(0, half, 2, dtype=F32) / half)
    ang = jnp.concatenate([row[:, None] * inv_freq, col[:, None] * inv_freq], axis=-1)
    cos = jnp.repeat(jnp.cos(ang), 2, axis=-1)
    sin = jnp.repeat(jnp.sin(ang), 2, axis=-1)
    sign = jnp.tile(jnp.array([-1.0, 1.0], F32), half)
    cos = jnp.concatenate([jnp.ones((tm, HEAD_DIM), F32), cos], axis=0)
    sin = jnp.concatenate([jnp.zeros((tm, HEAD_DIM), F32), sin * sign], axis=0)
    return cos, sin


def _qk_prep(dm, layer, proj, cos, sin, q_norm, k_norm):
    tm = ROW_TILE
    npt = dm.np_ // tm
    tps = dm.ts // tm

    def tab(i):
        return (jnp.where(i < npt, 0, 1 + (i - npt) % tps), 0)

    return pl.pallas_call(
        _qk_prep_kernel,
        out_shape=(
            jax.ShapeDtypeStruct((dm.n, ATTN_WIDTH), BF16),
            jax.ShapeDtypeStruct((dm.n, KV_WIDTH), BF16),
            jax.ShapeDtypeStruct((dm.n, KV_WIDTH), F32),
            jax.ShapeDtypeStruct((dm.n, KV_WIDTH), F32),
        ),
        grid=(dm.n // tm,),
        in_specs=[
            pl.BlockSpec((tm, ATTN_WIDTH), lambda i: (i, 0)),
            pl.BlockSpec((tm, KV_WIDTH), lambda i: (i, dm.off_k // KV_WIDTH)),
            pl.BlockSpec((tm, KV_WIDTH), lambda i: (i, dm.off_v // KV_WIDTH)),
            pl.BlockSpec((tm, HEAD_DIM), tab),
            pl.BlockSpec((tm, HEAD_DIM), tab),
            pl.BlockSpec((None, 1, HEAD_DIM), lambda i: (layer, 0, 0)),
            pl.BlockSpec((None, 1, HEAD_DIM), lambda i: (layer, 0, 0)),
        ],
        out_specs=(
            pl.BlockSpec((tm, ATTN_WIDTH), lambda i: (i, 0)),
            pl.BlockSpec((tm, KV_WIDTH), lambda i: (i, 0)),
            pl.BlockSpec((tm, KV_WIDTH), lambda i: (i, 0)),
            pl.BlockSpec((tm, KV_WIDTH), lambda i: (i, 0)),
        ),
        compiler_params=_params(("arbitrary",)),
        name="qk_prep",
    )(proj, proj, proj, cos, sin, q_norm, k_norm)


def _ctx_attn_kernel(q_ref, k_ref, v_ref, o_ref):
    for h in range(N_HEADS):
        kv = h // GROUP
        cols = slice(h * HEAD_DIM, (h + 1) * HEAD_DIM)
        kcols = slice(kv * HEAD_DIM, (kv + 1) * HEAD_DIM)
        s = _dot_nt(q_ref[:, cols], k_ref[:, kcols])
        m = jnp.max(s, axis=-1, keepdims=True)
        p = jnp.exp(s - m)
        l = jnp.sum(p, axis=-1, keepdims=True)
        o = _dot(p.astype(BF16), v_ref[:, kcols]) / l
        o_ref[:, cols] = o.astype(o_ref.dtype)


def _ctx_attn(dm, q, k, proj):
    t = dm.tp
    return pl.pallas_call(
        _ctx_attn_kernel,
        out_shape=jax.ShapeDtypeStruct((dm.np_, ATTN_WIDTH), BF16),
        grid=(dm.bp,),
        in_specs=[
            pl.BlockSpec((t, ATTN_WIDTH), lambda b: (b, 0)),
            pl.BlockSpec((t, KV_WIDTH), lambda b: (b, 0)),
            pl.BlockSpec((t, KV_WIDTH), lambda b: (b, dm.off_v // KV_WIDTH)),
        ],
        out_specs=pl.BlockSpec((t, ATTN_WIDTH), lambda b: (b, 0)),
        compiler_params=_params(("arbitrary",)),
        name="ctx_attn",
    )(q, k, proj)


def _lat_attn_kernel(q_ref, k_ref, v_ref, ck_ref, cv_ref, o_ref):
    ck = ck_ref[...].astype(BF16)
    cv = cv_ref[...].astype(BF16)
    for h in range(N_HEADS):
        kv = h // GROUP
        cols = slice(h * HEAD_DIM, (h + 1) * HEAD_DIM)
        kcols = slice(kv * HEAD_DIM, (kv + 1) * HEAD_DIM)
        qh = q_ref[:, cols]
        s_ctx = _dot_nt(qh, ck[:, kcols])
        s_lat = _dot_nt(qh, k_ref[:, kcols])
        m = jnp.maximum(jnp.max(s_ctx, axis=-1, keepdims=True),
                        jnp.max(s_lat, axis=-1, keepdims=True))
        p_ctx = jnp.exp(s_ctx - m)
        p_lat = jnp.exp(s_lat - m)
        l = jnp.sum(p_ctx, axis=-1, keepdims=True) + jnp.sum(p_lat, axis=-1, keepdims=True)
        o = _dot(p_ctx.astype(BF16), cv[:, kcols]) + _dot(p_lat.astype(BF16), v_ref[:, kcols])
        o_ref[:, cols] = (o / l).astype(o_ref.dtype)


def _lat_attn(dm, layer, q, k, proj, cache_k, cache_v):
    tq = ROW_TILE
    ts = dm.ts
    qb = ts // tq
    return pl.pallas_call(
        _lat_attn_kernel,
        out_shape=jax.ShapeDtypeStruct((dm.ns, ATTN_WIDTH), BF16),
        grid=(dm.bs, qb),
        in_specs=[
            pl.BlockSpec((tq, ATTN_WIDTH), lambda b, i: (dm.np_ // tq + b * qb + i, 0)),
            pl.BlockSpec((ts, KV_WIDTH), lambda b, i: (dm.np_ // ts + b, 0)),
            pl.BlockSpec((ts, KV_WIDTH), lambda b, i: (dm.np_ // ts + b, dm.off_v // KV_WIDTH)),
            pl.BlockSpec((None, None, dm.past, KV_WIDTH), lambda b, i: (b, layer, 0, 0)),
            pl.BlockSpec((None, None, dm.past, KV_WIDTH), lambda b, i: (b, layer, 0, 0)),
        ],
        out_specs=pl.BlockSpec((tq, ATTN_WIDTH), lambda b, i: (b * qb + i, 0)),
        compiler_params=_params(("arbitrary", "arbitrary")),
        name="lat_attn",
    )(q, k, proj, cache_k, cache_v)


def _local_kernel(dm, a_ref, ap_ref, an_ref, g_ref, gp_ref, gn_ref, z_ref, zp_ref, zn_ref,
                  u_ref, v_ref, cw_ref, cb_ref, clg_ref, clb_ref, slg_ref, slb_ref,
                  sw_ref, sbt_ref, conv_o, pool_o, sgu_o, ubuf, zbuf):
    t = ROW_TILE
    i = pl.program_id(0)
    pos, cnt = dm.seq_tile_pos(i, t)
    has_prev = (pos > 0).astype(F32)
    has_next = (pos < cnt - 1).astype(F32)

    def glu(a, g):
        return a[...].astype(F32) * _sigmoid(g[...].astype(F32))

    ubuf[0:HALO, :] = glu(ap_ref, gp_ref) * has_prev
    ubuf[HALO:HALO + t, :] = glu(a_ref, g_ref)
    ubuf[HALO + t:, :] = glu(an_ref, gn_ref) * has_next
    zbuf[0:HALO, :] = zp_ref[...].astype(F32) * has_prev
    zbuf[HALO:HALO + t, :] = z_ref[...].astype(F32)
    zbuf[HALO + t:, :] = zn_ref[...].astype(F32) * has_next

    half_rows = t // 2
    centre = CONV_KERNEL // 2
    for r in range(2):
        pieces = []
        for c in range(dm.cw // LANES):
            cols = slice(c * LANES, (c + 1) * LANES)
            acc = jnp.zeros((half_rows, LANES), F32)
            for k in range(CONV_KERNEL):
                start = HALO + r * half_rows + k - centre
                acc = acc + cw_ref[k:k + 1, cols] * ubuf[start:start + half_rows, cols]
            pieces.append(acc)
        y = jnp.concatenate(pieces, axis=1) + cb_ref[...]
        y = _layer_norm(y, clg_ref[...], clb_ref[...])
        conv_o[r * half_rows:(r + 1) * half_rows, :] = (y * _sigmoid(y)).astype(conv_o.dtype)

    seq_len = jnp.where(i < dm.np_ // t, dm.tp, dm.ts)
    tok = pos * t + lax.broadcasted_iota(jnp.int32, (t, LANES), 0)
    for gi, w in enumerate(POOL_WINDOWS):
        cols = slice(gi * LANES, (gi + 1) * LANES)
        acc = jnp.zeros((t, LANES), F32)
        for j in range(-(w // 2), w - w // 2):
            acc = acc + zbuf[HALO + j:HALO + j + t, cols]
        lo = jnp.maximum(tok - w // 2, 0)
        hi = jnp.minimum(tok - w // 2 + w, seq_len)
        mean = acc / (hi - lo).astype(F32)
        pool_o[:, cols] = (mean - zbuf[HALO:HALO + t, cols]).astype(pool_o.dtype)

    vn = _layer_norm(v_ref[...].astype(F32), slg_ref[...], slb_ref[...]).astype(BF16)
    for gi in range(SGU_GROUPS):
        cols = slice(gi * CHUNK, (gi + 1) * CHUNK)
        w = sw_ref[gi].astype(BF16)
        bias = sbt_ref[:, gi:gi + 1]
        for c in range(t // CHUNK):
            rows = slice(c * CHUNK, (c + 1) * CHUNK)
            mixed = _dot(w, vn[rows, cols]) + bias
            sgu_o[rows, cols] = (u_ref[rows, cols].astype(F32) * mixed).astype(sgu_o.dtype)


def _local_branches(dm, layer, proj, conv_w, conv_b, conv_ln_g, conv_ln_b,
                    sgu_ln_g, sgu_ln_b, sgu_w, sgu_bt):
    t = ROW_TILE
    cw = dm.cw
    hb = t // HALO
    last = dm.n // HALO - 1
    ca = dm.off_a // cw

    def cur(c):
        return pl.BlockSpec((t, cw), lambda i: (i, c))

    def prev(c):
        return pl.BlockSpec((HALO, cw), lambda i: (jnp.maximum(i * hb - 1, 0), c))

    def nxt(c):
        return pl.BlockSpec((HALO, cw), lambda i: (jnp.minimum((i + 1) * hb, last), c))

    def vec():
        return pl.BlockSpec((None, 1, cw), lambda i: (layer, 0, 0))

    out = jax.ShapeDtypeStruct((dm.n, cw), BF16)
    return pl.pallas_call(
        functools.partial(_local_kernel, dm),
        out_shape=(out, out, out),
        grid=(dm.n // t,),
        in_specs=[
            cur(ca), prev(ca), nxt(ca),
            cur(ca + 1), prev(ca + 1), nxt(ca + 1),
            cur(ca + 2), prev(ca + 2), nxt(ca + 2),
            cur(ca + 3), cur(ca + 4),
            pl.BlockSpec((None, CONV_KERNEL, cw), lambda i: (layer, 0, 0)),
            vec(), vec(), vec(), vec(), vec(),
            pl.BlockSpec((None, SGU_GROUPS, CHUNK, CHUNK), lambda i: (layer, 0, 0, 0)),
            pl.BlockSpec((None, CHUNK, SGU_GROUPS), lambda i: (layer, 0, 0)),
        ],
        out_specs=(pl.BlockSpec((t, cw), lambda i: (i, 0)),) * 3,
        scratch_shapes=[pltpu.VMEM((t + 2 * HALO, cw), F32), pltpu.VMEM((t + 2 * HALO, cw), F32)],
        compiler_params=_params(("arbitrary",)),
        name="local_branches",
    )(proj, proj, proj, proj, proj, proj, proj, proj, proj, proj, proj,
      conv_w, conv_b, conv_ln_g, conv_ln_b, sgu_ln_g, sgu_ln_b, sgu_w, sgu_bt)


def _merge_kernel(dm, o_ref, cv_ref, pl_ref, sg_ref, g0_ref, g1_ref, g2_ref, g3_ref,
                  wa_ref, wc_ref, wp_ref, ws_ref, ps_ref, m_ref):
    cw = dm.cw
    for c in range(dm.d // cw):
        cols = slice(c * cw, (c + 1) * cw)
        ya = _dot(o_ref[...], wa_ref[:, cols])
        yb = _dot(cv_ref[...], wc_ref[:, cols])
        yc = _dot(pl_ref[:, c * LANES:(c + 1) * LANES], wp_ref[c]) * ps_ref[:, cols]
        yd = _dot(sg_ref[...], ws_ref[:, cols])
        merged = (_sigmoid(g0_ref[:, cols].astype(F32)) * ya
                  + _sigmoid(g1_ref[:, cols].astype(F32)) * yb
                  + _sigmoid(g2_ref[:, cols].astype(F32)) * yc
                  + _sigmoid(g3_ref[:, cols].astype(F32)) * yd)
        m_ref[:, cols] = merged.astype(m_ref.dtype)


def _merge(dm, layer, attn_o, conv_h, pool_h, sgu_h, proj, wa, wc, wp, ws, pool_scale):
    tm = ROW_TILE
    d, cw = dm.d, dm.cw
    g0 = dm.off_gates // d
    assert d // cw == POOL_GROUPS and wp.shape[-1] == cw

    def gate(b):
        return pl.BlockSpec((tm, d), lambda i: (i, g0 + b))

    return pl.pallas_call(
        functools.partial(_merge_kernel, dm),
        out_shape=jax.ShapeDtypeStruct((dm.n, d), BF16),
        grid=(dm.n // tm,),
        in_specs=[
            pl.BlockSpec((tm, ATTN_WIDTH), lambda i: (i, 0)),
            pl.BlockSpec((tm, cw), lambda i: (i, 0)),
            pl.BlockSpec((tm, cw), lambda i: (i, 0)),
            pl.BlockSpec((tm, cw), lambda i: (i, 0)),
            gate(0), gate(1), gate(2), gate(3),
            pl.BlockSpec((None, ATTN_WIDTH, d), lambda i: (layer, 0, 0)),
            pl.BlockSpec((None, cw, d), lambda i: (layer, 0, 0)),
            pl.BlockSpec((None, POOL_GROUPS, LANES, cw), lambda i: (layer, 0, 0, 0)),
            pl.BlockSpec((None, cw, d), lambda i: (layer, 0, 0)),
            pl.BlockSpec((None, 1, d), lambda i: (layer, 0, 0)),
        ],
        out_specs=pl.BlockSpec((tm, d), lambda i: (i, 0)),
        compiler_params=_params(("arbitrary",)),
        name="merge",
    )(attn_o, conv_h, pool_h, sgu_h, proj, proj, proj, proj, wa, wc, wp, ws, pool_scale)


def _out_proj_kernel(alpha, m_ref, x_ref, mod_ref, w_ref, lg_ref, lb_ref, wr_ref, br_ref,
                     x1_ref, h2_ref, idx_ref, comb_ref):
    gate1 = mod_ref[2:3, :]
    shift2 = mod_ref[3:4, :]
    scale2 = mod_ref[4:5, :]
    y = _dot(m_ref[...], w_ref[...])
    x1 = _layer_norm(alpha * x_ref[...] + gate1 * y, lg_ref[...], lb_ref[...])
    x1_ref[...] = x1
    h2 = x1 * (1.0 + scale2) + shift2
    h2_hi = h2.astype(BF16)
    packed = _pack_pairs(h2)
    for p in range(h2_ref.shape[0]):
        h2_ref[p] = packed[:, p * GATHER_WORDS:(p + 1) * GATHER_WORDS]

    h2_lo = (h2 - h2_hi.astype(F32)).astype(BF16)
    wr = wr_ref[...]
    wr_hi = wr.astype(BF16)
    wr_lo = (wr - wr_hi.astype(F32)).astype(BF16)
    logits = _dot(h2_hi, wr_hi) + _dot(h2_hi, wr_lo) + _dot(h2_lo, wr_hi) + br_ref[...]

    lane = lax.broadcasted_iota(jnp.int32, logits.shape, 1)
    idx_out = jnp.zeros(logits.shape, jnp.int32)
    val_out = jnp.zeros(logits.shape, F32)
    top = None
    den = jnp.zeros((logits.shape[0], 1), F32)
    for k in range(TOP_K):
        m = jnp.max(logits, axis=-1, keepdims=True)
        ik = jnp.min(jnp.where(logits == m, lane, LANES), axis=-1, keepdims=True)
        if top is None:
            top = m
        e = jnp.exp(m - top)
        den = den + e
        idx_out = jnp.where(lane == k, ik, idx_out)
        val_out = jnp.where(lane == k, e, val_out)
        logits = jnp.where(lane == ik, -jnp.inf, logits)
    idx_ref[...] = idx_out
    comb_ref[...] = val_out / den


def _out_proj(dm, layer, merged, x, mod, w_out, ln_g, ln_b, w_router, b_router):
    tm = ROW_TILE
    d = dm.d
    alpha = (2 * dm.depth) ** 0.25

    def vec():
        return pl.BlockSpec((None, 1, d), lambda i: (layer, 0, 0))

    return pl.pallas_call(
        functools.partial(_out_proj_kernel, alpha),
        out_shape=(
            jax.ShapeDtypeStruct((dm.n, d), F32),
            jax.ShapeDtypeStruct((d // 2 // GATHER_WORDS, dm.n, GATHER_WORDS), jnp.uint32),
            jax.ShapeDtypeStruct((dm.n, LANES), jnp.int32),
            jax.ShapeDtypeStruct((dm.n, LANES), F32),
        ),
        grid=(dm.n // tm,),
        in_specs=[
            pl.BlockSpec((tm, d), lambda i: (i, 0)),
            pl.BlockSpec((tm, d), lambda i: (i, 0)),
            pl.BlockSpec((None, None, N_MOD, d), lambda i: (layer, dm.group_of_tile(i, tm), 0, 0)),
            pl.BlockSpec((None, d, d), lambda i: (layer, 0, 0)),
            vec(), vec(),
            pl.BlockSpec((None, d, LANES), lambda i: (layer, 0, 0)),
            pl.BlockSpec((None, 1, LANES), lambda i: (layer, 0, 0)),
        ],
        out_specs=(
            pl.BlockSpec((tm, d), lambda i: (i, 0)),
            pl.BlockSpec((d // 2 // GATHER_WORDS, tm, GATHER_WORDS), lambda i: (0, i, 0)),
            pl.BlockSpec((tm, LANES), lambda i: (i, 0)),
            pl.BlockSpec((tm, LANES), lambda i: (i, 0)),
        ),
        compiler_params=_params(("arbitrary",)),
        name="out_proj",
    )(merged, x, mod, w_out, ln_g, ln_b, w_router, b_router)


def _moe_live_chunks(nv, fn):
    pair = 2 * MOE_SUB
    for base in range(0, MOE_SUPER, pair):
        both = nv > base + MOE_SUB
        only_first = jnp.logical_and(nv > base, jnp.logical_not(both))
        pl.when(both)(functools.partial(fn, base, pair))
        pl.when(only_first)(functools.partial(fn, base, MOE_SUB))


def _moe_kernel(nfa, be_ref, nv_ref, na_ref, x_ref, wg_ref, bg_ref, wu_ref, bu_ref, wd_ref, bd_ref,
                o_ref, act_ref, wgu_ref, wdn_ref, xbf_ref):
    s = pl.program_id(0)
    j = pl.program_id(1)
    nv = nv_ref[s]
    tf = MOE_FF_TILE
    active = s < na_ref[0]

    @pl.when(jnp.logical_and(active, j == 0))
    def _():
        half = xbf_ref.shape[1] // 2

        def unpack(c, carry):
            rows = pl.ds(pl.multiple_of(c * LANES, LANES), LANES)
            for p in range(x_ref.shape[0]):
                lo, hi = _unpack_pairs(x_ref[p, rows, :])
                xbf_ref[rows, p * GATHER_WORDS:(p + 1) * GATHER_WORDS] = lo.astype(BF16)
                xbf_ref[rows, half + p * GATHER_WORDS:half + (p + 1) * GATHER_WORDS] = hi.astype(BF16)
            return carry

        lax.fori_loop(0, MOE_SUPER // LANES, unpack, 0)

    @pl.when(jnp.logical_and(active, j < nfa))
    def _():
        wgu_ref[:, :tf] = wg_ref[...].astype(BF16)
        wgu_ref[:, tf:] = wu_ref[...].astype(BF16)
        bg = bg_ref[...]
        bu = bu_ref[...]
        col0 = pl.multiple_of(j * tf, tf)

        def chunk(base, m):
            gu = _dot(xbf_ref[base:base + m, :], wgu_ref[...])
            gt = jnp.minimum(gu[:, :tf] + bg, SWIGLU_LIMIT)
            up = jnp.clip(gu[:, tf:] + bu, -SWIGLU_LIMIT, SWIGLU_LIMIT)
            act = gt * _sigmoid(SWIGLU_ALPHA * gt) * (up + 1.0)
            act_ref[base:base + m, pl.ds(col0, tf)] = act.astype(BF16)

        _moe_live_chunks(nv, chunk)

    @pl.when(j >= nfa)
    def _():
        @pl.when(active)
        def _():
            wdn_ref[...] = wd_ref[...].astype(BF16)

        bd = bd_ref[...]

        def chunk(base, m):
            y = _dot(act_ref[base:base + m, :], wdn_ref[...]) + bd
            o_ref[base:base + m, :] = _pack_pairs(y)

        _moe_live_chunks(nv, chunk)
        for base in range(0, MOE_SUPER, MOE_SUB):
            @pl.when(nv <= base)
            def _():
                o_ref[base:base + MOE_SUB, :] = jnp.zeros((MOE_SUB, o_ref.shape[1]), o_ref.dtype)


def _moe(dm, layer, x_sorted, blk_e, blk_nv, n_act, w_gate, b_gate, w_up, b_up, w_down, b_down):
    d, ff, tf, tn, r = dm.d, dm.ff, MOE_FF_TILE, MOE_OUT_TILE, MOE_SUPER
    n_planes = x_sorted.shape[0]
    nsb = x_sorted.shape[1] // r
    nfa = ff // tf
    nfb = d // tn
    assert tn // 2 == GATHER_WORDS

    def blk(s, na):
        return jnp.minimum(s, na[0] - 1)

    def col_a(s, j, na):
        return jnp.where(s < na[0], jnp.minimum(j, nfa - 1), nfa - 1)

    def col_b(s, j, na):
        return jnp.where(s < na[0], jnp.maximum(j - nfa, 0), nfb - 1)

    grid_spec = pltpu.PrefetchScalarGridSpec(
        num_scalar_prefetch=3,
        grid=(nsb, nfa + nfb),
        in_specs=[
            pl.BlockSpec((n_planes, r, GATHER_WORDS), lambda s, j, be, nv, na: (0, blk(s, na), 0)),
            pl.BlockSpec((None, None, d, tf), lambda s, j, be, nv, na: (layer, be[s], 0, col_a(s, j, na))),
            pl.BlockSpec((None, None, 1, tf), lambda s, j, be, nv, na: (layer, be[s], 0, col_a(s, j, na))),
            pl.BlockSpec((None, None, d, tf), lambda s, j, be, nv, na: (layer, be[s], 0, col_a(s, j, na))),
            pl.BlockSpec((None, None, 1, tf), lambda s, j, be, nv, na: (layer, be[s], 0, col_a(s, j, na))),
            pl.BlockSpec((None, None, ff, tn), lambda s, j, be, nv, na: (layer, be[s], 0, col_b(s, j, na))),
            pl.BlockSpec((None, None, 1, tn), lambda s, j, be, nv, na: (layer, be[s], 0, col_b(s, j, na))),
        ],
        out_specs=pl.BlockSpec((None, r, tn // 2), lambda s, j, be, nv, na: (jnp.maximum(j - nfa, 0), s, 0)),
        scratch_shapes=[
            pltpu.VMEM((r, ff), BF16),
            pltpu.VMEM((d, 2 * tf), BF16),
            pltpu.VMEM((ff, tn), BF16),
            pltpu.VMEM((r, d), BF16),
        ],
    )
    return pl.pallas_call(
        functools.partial(_moe_kernel, nfa),
        out_shape=jax.ShapeDtypeStruct((nfb, nsb * r, tn // 2), jnp.uint32),
        grid_spec=grid_spec,
        compiler_params=_params(("arbitrary", "arbitrary")),
        name="moe_ffn",
    )(blk_e, blk_nv, n_act, x_sorted, w_gate, b_gate, w_up, b_up, w_down, b_down)


def _route(dm, top_idx, n_super):
    e, r = dm.n_exp, MOE_SUPER
    flat_e = top_idx.reshape(-1)
    onehot = (flat_e[:, None] == jnp.arange(e, dtype=jnp.int32)[None, :]).astype(jnp.int32)
    rank = jnp.take_along_axis(jnp.cumsum(onehot, axis=0), flat_e[:, None], axis=1)[:, 0] - 1
    counts = jnp.sum(onehot, axis=0)
    nsb_e = (counts + r - 1) // r
    sb_end = jnp.cumsum(nsb_e)
    sb_start = sb_end - nsb_e
    dest = sb_start[flat_e] * r + rank
    n_act = sb_end[-1]
    sidx = jnp.arange(n_super, dtype=jnp.int32)
    blk_e = jnp.clip(jnp.searchsorted(sb_end, jnp.minimum(sidx, n_act - 1), side='right'), 0, e - 1)
    blk_e = blk_e.astype(jnp.int32)
    blk_nv = jnp.clip(counts[blk_e] - (sidx - sb_start[blk_e]) * r, 0, r)
    blk_nv = jnp.where(sidx < n_act, blk_nv, 0).astype(jnp.int32)
    return dest.astype(jnp.int32), blk_e, blk_nv, n_act.astype(jnp.int32).reshape(1)


def _combine_kernel(alpha, y_ref, comb_ref, x_ref, mod_ref, lg_ref, lb_ref, o_ref):
    gate2 = mod_ref[5:6, :]
    comb = comb_ref[...]
    slabs = []
    for b in range(y_ref.shape[0]):
        lo_sum = None
        hi_sum = None
        for k in range(TOP_K):
            lo, hi = _unpack_pairs(y_ref[b, k])
            wk = comb[:, k:k + 1]
            lo_sum = wk * lo if lo_sum is None else lo_sum + wk * lo
            hi_sum = wk * hi if hi_sum is None else hi_sum + wk * hi
        slabs += [lo_sum, hi_sum]
    ff = jnp.concatenate(slabs, axis=1)
    o_ref[...] = _layer_norm(alpha * x_ref[...] + gate2 * ff, lg_ref[...], lb_ref[...])


def _combine(dm, layer, y_tok, comb, x1, mod, ln_g, ln_b):
    tm = ROW_TILE
    d = dm.d
    alpha = (2 * dm.depth) ** 0.25
    return pl.pallas_call(
        functools.partial(_combine_kernel, alpha),
        out_shape=jax.ShapeDtypeStruct((dm.n, d), F32),
        grid=(dm.n // tm,),
        in_specs=[
            pl.BlockSpec((y_tok.shape[0], TOP_K, tm, y_tok.shape[3]), lambda i: (0, 0, i, 0)),
            pl.BlockSpec((tm, LANES), lambda i: (i, 0)),
            pl.BlockSpec((tm, d), lambda i: (i, 0)),
            pl.BlockSpec((None, None, N_MOD, d), lambda i: (layer, dm.group_of_tile(i, tm), 0, 0)),
            pl.BlockSpec((None, 1, d), lambda i: (layer, 0, 0)),
            pl.BlockSpec((None, 1, d), lambda i: (layer, 0, 0)),
        ],
        out_specs=pl.BlockSpec((tm, d), lambda i: (i, 0)),
        compiler_params=_params(("arbitrary",)),
        name="combine",
    )(y_tok, comb, x1, mod, ln_g, ln_b)


def kernel(x_prompt, x_sample, cache_k, cache_v, c, c_ctx, w_mod, b_mod, w_in, q_norm, k_norm,
           w_attn_o, conv_w, conv_b, conv_ln_g, conv_ln_b, w_conv_o, w_pool, pool_scale,
           sgu_ln_g, sgu_ln_b, sgu_w, sgu_b, w_sgu_o, w_out, ln1_g, ln1_b, ln2_g, ln2_b,
           w_router, b_router, w_gate, b_gate, w_up, b_up, w_down, b_down):
    bp, tp, d = x_prompt.shape
    bs, ts, _ = x_sample.shape
    depth, n_exp, _, ff = w_gate.shape
    past = cache_k.shape[2]
    dm = _Dims(bp, tp, bs, ts, d, past, depth, n_exp, ff)
    assert 1 + bs <= SUBLANES and n_exp <= LANES

    x = jnp.concatenate([x_prompt.reshape(dm.np_, d), x_sample.reshape(dm.ns, d)], axis=0)
    cond = jnp.concatenate([c_ctx[None, :], c, jnp.zeros((SUBLANES - 1 - bs, d), F32)], axis=0)
    mod = _modulation(cond, w_mod, b_mod).reshape(depth, SUBLANES, N_MOD, d)

    cos, sin = _rope_tables(dm, ROW_TILE)
    ck = cache_k.reshape(bs, depth, past, KV_WIDTH)
    cv = cache_v.reshape(bs, depth, past, KV_WIDTH)

    def row(p):
        return p.reshape(depth, 1, p.shape[-1])

    wa, wc, wp, ws, wo = (w.astype(BF16) for w in (w_attn_o, w_conv_o, w_pool, w_sgu_o, w_out))
    sgu_bt = jnp.swapaxes(sgu_b, 1, 2)
    w_router_p = jnp.pad(w_router, ((0, 0), (0, 0), (0, LANES - n_exp)))
    b_router_p = jnp.pad(b_router, ((0, 0), (0, LANES - n_exp)), constant_values=-jnp.inf)
    b_router_p = b_router_p.reshape(depth, 1, LANES)
    bg, bu = b_gate.reshape(depth, n_exp, 1, ff), b_up.reshape(depth, n_exp, 1, ff)
    bd = b_down.reshape(depth, n_exp, 1, d)

    nk = dm.n * TOP_K
    n_super = nk // MOE_SUPER + n_exp
    ks, vs = [], []
    for l in range(depth):
        proj = _in_proj(dm, l, x, mod, w_in)
        q, k, kn, v32 = _qk_prep(dm, l, proj, cos, sin, row(q_norm), row(k_norm))
        ks.append(kn[:dm.np_])
        vs.append(v32[:dm.np_])
        attn_o = jnp.concatenate(
            [_ctx_attn(dm, q, k, proj), _lat_attn(dm, l, q, k, proj, ck, cv)], axis=0)
        conv_h, pool_h, sgu_h = _local_branches(
            dm, l, proj, conv_w, row(conv_b), row(conv_ln_g), row(conv_ln_b),
            row(sgu_ln_g), row(sgu_ln_b), sgu_w, sgu_bt)
        merged = _merge(dm, l, attn_o, conv_h, pool_h, sgu_h, proj, wa, wc, wp, ws, row(pool_scale))
        x1, h2, top_idx, comb = _out_proj(dm, l, merged, x, mod, wo, row(ln1_g), row(ln1_b),
                                          w_router_p, b_router_p)
        dest, blk_e, blk_nv, n_act = _route(dm, top_idx[:, :TOP_K], n_super)
        n_slots = n_super * MOE_SUPER
        slot_tok = (jnp.arange(n_slots, dtype=jnp.int32) % dm.n).at[dest].set(
            jnp.arange(nk, dtype=jnp.int32) // TOP_K)
        x_sorted = _gather_rows(h2, slot_tok)
        y_sorted = _moe(dm, l, x_sorted, blk_e, blk_nv, n_act, w_gate, bg, w_up, bu, w_down, bd)
        y_tok = _gather_rows(y_sorted, dest.reshape(dm.n, TOP_K).T.reshape(-1))
        y_tok = y_tok.reshape(y_sorted.shape[0], TOP_K, dm.n, GATHER_WORDS)
        x = _combine(dm, l, y_tok, comb, x1, mod, row(ln2_g), row(ln2_b))

    y_prompt = x[:dm.np_].reshape(bp, tp, d)
    y_sample = x[dm.np_:].reshape(bs, ts, d)
    state_k = jnp.stack(ks, axis=1).reshape(bp, tp, depth, N_KV_HEADS, HEAD_DIM).swapaxes(1, 2)
    state_v = jnp.stack(vs, axis=1).reshape(bp, tp, depth, N_KV_HEADS, HEAD_DIM).swapaxes(1, 2)
    return (y_prompt, y_sample, state_k, state_v)
```

```python
import functools

import jax
import jax.numpy as jnp
from jax import lax
from jax.experimental import pallas as pl
from jax.experimental.pallas import tpu as pltpu
from jax.experimental.pallas import tpu_sc as plsc

F32 = jnp.float32
BF16 = jnp.bfloat16

N_HEADS = 8
N_KV_HEADS = 2
HEAD_DIM = 128
GRID_W = 64
ROPE_THETA = 10000.0
CONV_KERNEL = 31
POOL_WINDOWS = (2, 4, 8, 16)
POOL_GROUPS = 4
SGU_GROUPS = 4
CHUNK = 128
N_BRANCHES = 4
TOP_K = 4
SWIGLU_LIMIT = 7.0
SWIGLU_ALPHA = 1.702
NORM_EPS = 1e-6
N_MOD = 6

ATTN_WIDTH = N_HEADS * HEAD_DIM
KV_WIDTH = N_KV_HEADS * HEAD_DIM
GROUP = N_HEADS // N_KV_HEADS

LANES = 128
SUBLANES = 8
VMEM_LIMIT_BYTES = 56 * 1024 * 1024

ROW_TILE = 256
HALO = 16
IN_PROJ_ROWS = 1024
IN_PROJ_COLS = 1024
MOD_COLS = 1024
MOE_SUPER = 1024
MOE_SUB = 256
MOE_TAIL = 128
MOE_FF_TILE = 512
MOE_OUT_TILE = 512
GATHER_WINDOW = 128
GATHER_WORDS = 256


def _dot(a, b):
    return jnp.dot(a, b, preferred_element_type=F32)


def _dot_nt(a, b):
    return lax.dot_general(a, b, (((1,), (1,)), ((), ())), preferred_element_type=F32)


def _layer_norm(x, g, b):
    mu = jnp.mean(x, axis=-1, keepdims=True)
    xc = x - mu
    var = jnp.mean(xc * xc, axis=-1, keepdims=True)
    return xc * lax.rsqrt(var + NORM_EPS) * g + b


def _sigmoid(x):
    return 0.5 * jnp.tanh(0.5 * x) + 0.5


def _params(sem, vmem=VMEM_LIMIT_BYTES):
    return pltpu.CompilerParams(dimension_semantics=sem, vmem_limit_bytes=vmem)


_HIGH_HALF = 0xFFFF0000


def _pack_pairs(x):
    w = x.shape[1] // 2
    lo = lax.bitcast_convert_type(x[:, :w].astype(BF16).astype(F32), jnp.uint32)
    hi = lax.bitcast_convert_type(x[:, w:].astype(BF16).astype(F32), jnp.uint32)
    return (lo >> 16) | (hi & jnp.uint32(_HIGH_HALF))


def _unpack_pairs(words):
    lo = lax.bitcast_convert_type(words << 16, F32)
    hi = lax.bitcast_convert_type(words & jnp.uint32(_HIGH_HALF), F32)
    return lo, hi


def _gather_rows(planes, idx):
    n_planes, n_rows, width = planes.shape
    assert planes.dtype.itemsize == 4 and width == GATHER_WORDS
    offsets = jnp.arange(n_planes, dtype=jnp.int32) * n_rows
    flat_idx = (offsets[:, None] + idx[None, :]).reshape(-1)
    out = _gather_pieces(planes.reshape(n_planes * n_rows, width), flat_idx)
    return out.reshape(n_planes, idx.shape[0], width)


def _gather_pieces(table, idx):
    n_idx = idx.shape[0]
    width = table.shape[1]
    assert n_idx % GATHER_WINDOW == 0
    mesh = plsc.VectorSubcoreMesh(core_axis_name="core", subcore_axis_name="subcore")

    @functools.partial(
        pl.kernel,
        out_type=jax.ShapeDtypeStruct((n_idx, width), table.dtype),
        mesh=mesh,
        scratch_types=[],
    )
    def gather_kernel(table_hbm, idx_hbm, out_hbm):
        def body(idx_vmem, out_vmem):
            pltpu.sync_copy(table_hbm.at[idx_vmem.at[0]], out_vmem)

        pltpu.emit_pipeline(
            body,
            grid=(n_idx // GATHER_WINDOW,),
            in_specs=[pl.BlockSpec((1, GATHER_WINDOW), lambda i: (0, i))],
            out_specs=[pl.BlockSpec((GATHER_WINDOW, width), lambda i: (i, 0))],
            core_axis_name=("core", "subcore"),
            dimension_semantics=(pltpu.PARALLEL,),
        )(idx_hbm, out_hbm)

    return gather_kernel(table, idx.reshape(1, n_idx))


class _Dims:
    def __init__(self, bp, tp, bs, ts, d, past, depth, n_exp, ff):
        self.bp, self.tp, self.bs, self.ts = bp, tp, bs, ts
        self.d, self.past, self.depth, self.n_exp, self.ff = d, past, depth, n_exp, ff
        self.np_ = bp * tp
        self.ns = bs * ts
        self.n = self.np_ + self.ns
        self.cw = d // 4
        self.in_w = ATTN_WIDTH + 2 * KV_WIDTH + 5 * self.cw + N_BRANCHES * d
        self.off_k = ATTN_WIDTH
        self.off_v = ATTN_WIDTH + KV_WIDTH
        self.off_a = ATTN_WIDTH + 2 * KV_WIDTH
        self.off_gates = self.off_a + 5 * self.cw
        assert self.cw == POOL_GROUPS * LANES == SGU_GROUPS * CHUNK
        assert self.off_a % self.cw == 0 and self.off_gates % d == 0
        assert tp % ROW_TILE == 0 and ts % ROW_TILE == 0 and self.np_ % ts == 0
        assert ts % GRID_W == 0

    def group_of_tile(self, i, tm):
        npt = self.np_ // tm
        return jnp.where(i < npt, 0, 1 + (i - npt) // (self.ts // tm))

    def seq_tile_pos(self, i, tm):
        npt = self.np_ // tm
        is_p = i < npt
        pos = jnp.where(is_p, i % (self.tp // tm), (i - npt) % (self.ts // tm))
        cnt = jnp.where(is_p, self.tp // tm, self.ts // tm)
        return pos, cnt


def _mod_kernel(c_ref, w_ref, b_ref, o_ref):
    c = c_ref[...]
    s = c * _sigmoid(c)
    o_ref[...] = _dot(s, w_ref[...]) + b_ref[...]


def _modulation(cond8, w_mod, b_mod):
    depth, d, width = w_mod.shape
    tn = MOD_COLS
    return pl.pallas_call(
        _mod_kernel,
        out_shape=jax.ShapeDtypeStruct((depth, SUBLANES, width), F32),
        grid=(depth, width // tn),
        in_specs=[
            pl.BlockSpec((SUBLANES, d), lambda l, j: (0, 0)),
            pl.BlockSpec((None, d, tn), lambda l, j: (l, 0, j)),
            pl.BlockSpec((None, 1, tn), lambda l, j: (l, 0, j)),
        ],
        out_specs=pl.BlockSpec((None, SUBLANES, tn), lambda l, j: (l, 0, j)),
        compiler_params=_params(("arbitrary", "arbitrary")),
        name="modulation",
    )(cond8, w_mod, b_mod.reshape(depth, 1, width))


def _in_proj_kernel(x_ref, mod_ref, w_ref, o_ref, wbf_ref):
    @pl.when(pl.program_id(1) == 0)
    def _():
        wbf_ref[...] = w_ref[...].astype(BF16)

    shift = mod_ref[0:1, :]
    scale = mod_ref[1:2, :]
    h = (x_ref[...] * (1.0 + scale) + shift).astype(BF16)
    o_ref[...] = _dot(h, wbf_ref[...]).astype(o_ref.dtype)


def _in_proj(dm, layer, x, mod, w_in):
    tm, tn = IN_PROJ_ROWS, IN_PROJ_COLS
    d = dm.d
    return pl.pallas_call(
        _in_proj_kernel,
        out_shape=jax.ShapeDtypeStruct((dm.n, dm.in_w), BF16),
        grid=(dm.in_w // tn, dm.n // tm),
        in_specs=[
            pl.BlockSpec((tm, d), lambda j, i: (i, 0)),
            pl.BlockSpec((None, None, N_MOD, d), lambda j, i: (layer, dm.group_of_tile(i, tm), 0, 0)),
            pl.BlockSpec((None, d, tn), lambda j, i: (layer, 0, j)),
        ],
        out_specs=pl.BlockSpec((tm, tn), lambda j, i: (i, j)),
        scratch_shapes=[pltpu.VMEM((d, tn), BF16)],
        compiler_params=_params(("arbitrary", "arbitrary")),
        name="in_proj",
    )(x, mod, w_in)


def _rope(x, cos, sin_signed):
    lane = lax.broadcasted_iota(jnp.int32, x.shape, 1)
    nxt = pltpu.roll(x, HEAD_DIM - 1, 1)
    prv = pltpu.roll(x, 1, 1)
    partner = jnp.where((lane & 1) == 0, nxt, prv)
    return x * cos + partner * sin_signed


def _rms(x, g):
    return x * lax.rsqrt(jnp.mean(x * x, axis=-1, keepdims=True) + NORM_EPS) * g


def _qk_prep_kernel(q_ref, k_ref, v_ref, cos_ref, sin_ref, qg_ref, kg_ref,
                    qo_ref, ko_ref, kn_ref, vo_ref):
    cos = cos_ref[...]
    sin = sin_ref[...]
    qg = qg_ref[...]
    kg = kg_ref[...]
    scale = HEAD_DIM ** -0.5
    for h in range(N_HEADS):
        cols = slice(h * HEAD_DIM, (h + 1) * HEAD_DIM)
        qn = _rms(q_ref[:, cols].astype(F32), qg)
        qo_ref[:, cols] = (_rope(qn, cos, sin) * scale).astype(qo_ref.dtype)
    for h in range(N_KV_HEADS):
        cols = slice(h * HEAD_DIM, (h + 1) * HEAD_DIM)
        kn = _rms(k_ref[:, cols].astype(F32), kg)
        kn_ref[:, cols] = kn
        ko_ref[:, cols] = _rope(kn, cos, sin).astype(ko_ref.dtype)
    vo_ref[...] = v_ref[...].astype(F32)


def _rope_tables(dm, tm):
    t = dm.ts
    rows = t // GRID_W
    row = jnp.broadcast_to(jnp.arange(rows, dtype=F32)[:, None], (rows, GRID_W)).reshape(t)
    col = jnp.broadcast_to(jnp.arange(GRID_W, dtype=F32)[None, :], (rows, GRID_W)).reshape(t)
    half = HEAD_DIM // 2
    inv_freq = ROPE_THETA ** (-jnp.arange(0, half, 2, dtype=F32) / half)
    ang = jnp.concatenate([row[:, None] * inv_freq, col[:, None] * inv_freq], axis=-1)
    cos = jnp.repeat(jnp.cos(ang), 2, axis=-1)
    sin = jnp.repeat(jnp.sin(ang), 2, axis=-1)
    sign = jnp.tile(jnp.array([-1.0, 1.0], F32), half)
    cos = jnp.concatenate([jnp.ones((tm, HEAD_DIM), F32), cos], axis=0)
    sin = jnp.concatenate([jnp.zeros((tm, HEAD_DIM), F32), sin * sign], axis=0)
    return cos, sin


def _qk_prep(dm, layer, proj, cos, sin, q_norm, k_norm):
    tm = ROW_TILE
    npt = dm.np_ // tm
    tps = dm.ts // tm

    def tab(i):
        return (jnp.where(i < npt, 0, 1 + (i - npt) % tps), 0)

    return pl.pallas_call(
        _qk_prep_kernel,
        out_shape=(
            jax.ShapeDtypeStruct((dm.n, ATTN_WIDTH), BF16),
            jax.ShapeDtypeStruct((dm.n, KV_WIDTH), BF16),
            jax.ShapeDtypeStruct((dm.n, KV_WIDTH), F32),
            jax.ShapeDtypeStruct((dm.n, KV_WIDTH), F32),
        ),
        grid=(dm.n // tm,),
        in_specs=[
            pl.BlockSpec((tm, ATTN_WIDTH), lambda i: (i, 0)),
            pl.BlockSpec((tm, KV_WIDTH), lambda i: (i, dm.off_k // KV_WIDTH)),
            pl.BlockSpec((tm, KV_WIDTH), lambda i: (i, dm.off_v // KV_WIDTH)),
            pl.BlockSpec((tm, HEAD_DIM), tab),
            pl.BlockSpec((tm, HEAD_DIM), tab),
            pl.BlockSpec((None, 1, HEAD_DIM), lambda i: (layer, 0, 0)),
            pl.BlockSpec((None, 1, HEAD_DIM), lambda i: (layer, 0, 0)),
        ],
        out_specs=(
            pl.BlockSpec((tm, ATTN_WIDTH), lambda i: (i, 0)),
            pl.BlockSpec((tm, KV_WIDTH), lambda i: (i, 0)),
            pl.BlockSpec((tm, KV_WIDTH), lambda i: (i, 0)),
            pl.BlockSpec((tm, KV_WIDTH), lambda i: (i, 0)),
        ),
        compiler_params=_params(("arbitrary",)),
        name="qk_prep",
    )(proj, proj, proj, cos, sin, q_norm, k_norm)


def _ctx_attn_kernel(q_ref, k_ref, v_ref, o_ref):
    for h in range(N_HEADS):
        kv = h // GROUP
        cols = slice(h * HEAD_DIM, (h + 1) * HEAD_DIM)
        kcols = slice(kv * HEAD_DIM, (kv + 1) * HEAD_DIM)
        s = _dot_nt(q_ref[:, cols], k_ref[:, kcols])
        m = jnp.max(s, axis=-1, keepdims=True)
        p = jnp.exp(s - m)
        l = jnp.sum(p, axis=-1, keepdims=True)
        o = _dot(p.astype(BF16), v_ref[:, kcols]) / l
        o_ref[:, cols] = o.astype(o_ref.dtype)


def _ctx_attn(dm, q, k, proj):
    t = dm.tp
    return pl.pallas_call(
        _ctx_attn_kernel,
        out_shape=jax.ShapeDtypeStruct((dm.np_, ATTN_WIDTH), BF16),
        grid=(dm.bp,),
        in_specs=[
            pl.BlockSpec((t, ATTN_WIDTH), lambda b: (b, 0)),
            pl.BlockSpec((t, KV_WIDTH), lambda b: (b, 0)),
            pl.BlockSpec((t, KV_WIDTH), lambda b: (b, dm.off_v // KV_WIDTH)),
        ],
        out_specs=pl.BlockSpec((t, ATTN_WIDTH), lambda b: (b, 0)),
        compiler_params=_params(("arbitrary",)),
        name="ctx_attn",
    )(q, k, proj)


def _lat_attn_kernel(q_ref, k_ref, v_ref, ck_ref, cv_ref, o_ref):
    ck = ck_ref[...].astype(BF16)
    cv = cv_ref[...].astype(BF16)
    for h in range(N_HEADS):
        kv = h // GROUP
        cols = slice(h * HEAD_DIM, (h + 1) * HEAD_DIM)
        kcols = slice(kv * HEAD_DIM, (kv + 1) * HEAD_DIM)
        qh = q_ref[:, cols]
        s_ctx = _dot_nt(qh, ck[:, kcols])
        s_lat = _dot_nt(qh, k_ref[:, kcols])
        m = jnp.maximum(jnp.max(s_ctx, axis=-1, keepdims=True),
                        jnp.max(s_lat, axis=-1, keepdims=True))
        p_ctx = jnp.exp(s_ctx - m)
        p_lat = jnp.exp(s_lat - m)
        l = jnp.sum(p_ctx, axis=-1, keepdims=True) + jnp.sum(p_lat, axis=-1, keepdims=True)
        o = _dot(p_ctx.astype(BF16), cv[:, kcols]) + _dot(p_lat.astype(BF16), v_ref[:, kcols])
        o_ref[:, cols] = (o / l).astype(o_ref.dtype)


def _lat_attn(dm, layer, q, k, proj, cache_k, cache_v):
    tq = ROW_TILE
    ts = dm.ts
    qb = ts // tq
    return pl.pallas_call(
        _lat_attn_kernel,
        out_shape=jax.ShapeDtypeStruct((dm.ns, ATTN_WIDTH), BF16),
        grid=(dm.bs, qb),
        in_specs=[
            pl.BlockSpec((tq, ATTN_WIDTH), lambda b, i: (dm.np_ // tq + b * qb + i, 0)),
            pl.BlockSpec((ts, KV_WIDTH), lambda b, i: (dm.np_ // ts + b, 0)),
            pl.BlockSpec((ts, KV_WIDTH), lambda b, i: (dm.np_ // ts + b, dm.off_v // KV_WIDTH)),
            pl.BlockSpec((None, None, dm.past, KV_WIDTH), lambda b, i: (b, layer, 0, 0)),
            pl.BlockSpec((None, None, dm.past, KV_WIDTH), lambda b, i: (b, layer, 0, 0)),
        ],
        out_specs=pl.BlockSpec((tq, ATTN_WIDTH), lambda b, i: (b * qb + i, 0)),
        compiler_params=_params(("arbitrary", "arbitrary")),
        name="lat_attn",
    )(q, k, proj, cache_k, cache_v)


def _local_kernel(dm, a_ref, ap_ref, an_ref, g_ref, gp_ref, gn_ref, z_ref, zp_ref, zn_ref,
                  u_ref, v_ref, cw_ref, cb_ref, clg_ref, clb_ref, slg_ref, slb_ref,
                  sw_ref, sbt_ref, conv_o, pool_o, sgu_o, ubuf, zbuf):
    t = ROW_TILE
    i = pl.program_id(0)
    pos, cnt = dm.seq_tile_pos(i, t)
    has_prev = (pos > 0).astype(F32)
    has_next = (pos < cnt - 1).astype(F32)

    def glu(a, g):
        return a[...].astype(F32) * _sigmoid(g[...].astype(F32))

    ubuf[0:HALO, :] = glu(ap_ref, gp_ref) * has_prev
    ubuf[HALO:HALO + t, :] = glu(a_ref, g_ref)
    ubuf[HALO + t:, :] = glu(an_ref, gn_ref) * has_next
    zbuf[0:HALO, :] = zp_ref[...].astype(F32) * has_prev
    zbuf[HALO:HALO + t, :] = z_ref[...].astype(F32)
    zbuf[HALO + t:, :] = zn_ref[...].astype(F32) * has_next

    half_rows = t // 2
    centre = CONV_KERNEL // 2
    for r in range(2):
        pieces = []
        for c in range(dm.cw // LANES):
            cols = slice(c * LANES, (c + 1) * LANES)
            acc = jnp.zeros((half_rows, LANES), F32)
            for k in range(CONV_KERNEL):
                start = HALO + r * half_rows + k - centre
                acc = acc + cw_ref[k:k + 1, cols] * ubuf[start:start + half_rows, cols]
            pieces.append(acc)
        y = jnp.concatenate(pieces, axis=1) + cb_ref[...]
        y = _layer_norm(y, clg_ref[...], clb_ref[...])
        conv_o[r * half_rows:(r + 1) * half_rows, :] = (y * _sigmoid(y)).astype(conv_o.dtype)

    seq_len = jnp.where(i < dm.np_ // t, dm.tp, dm.ts)
    tok = pos * t + lax.broadcasted_iota(jnp.int32, (t, LANES), 0)
    for gi, w in enumerate(POOL_WINDOWS):
        cols = slice(gi * LANES, (gi + 1) * LANES)
        acc = jnp.zeros((t, LANES), F32)
        for j in range(-(w // 2), w - w // 2):
            acc = acc + zbuf[HALO + j:HALO + j + t, cols]
        lo = jnp.maximum(tok - w // 2, 0)
        hi = jnp.minimum(tok - w // 2 + w, seq_len)
        mean = acc / (hi - lo).astype(F32)
        pool_o[:, cols] = (mean - zbuf[HALO:HALO + t, cols]).astype(pool_o.dtype)

    vn = _layer_norm(v_ref[...].astype(F32), slg_ref[...], slb_ref[...]).astype(BF16)
    for gi in range(SGU_GROUPS):
        cols = slice(gi * CHUNK, (gi + 1) * CHUNK)
        w = sw_ref[gi].astype(BF16)
        bias = sbt_ref[:, gi:gi + 1]
        for c in range(t // CHUNK):
            rows = slice(c * CHUNK, (c + 1) * CHUNK)
            mixed = _dot(w, vn[rows, cols]) + bias
            sgu_o[rows, cols] = (u_ref[rows, cols].astype(F32) * mixed).astype(sgu_o.dtype)


def _local_branches(dm, layer, proj, conv_w, conv_b, conv_ln_g, conv_ln_b,
                    sgu_ln_g, sgu_ln_b, sgu_w, sgu_bt):
    t = ROW_TILE
    cw = dm.cw
    hb = t // HALO
    last = dm.n // HALO - 1
    ca = dm.off_a // cw

    def cur(c):
        return pl.BlockSpec((t, cw), lambda i: (i, c))

    def prev(c):
        return pl.BlockSpec((HALO, cw), lambda i: (jnp.maximum(i * hb - 1, 0), c))

    def nxt(c):
        return pl.BlockSpec((HALO, cw), lambda i: (jnp.minimum((i + 1) * hb, last), c))

    def vec():
        return pl.BlockSpec((None, 1, cw), lambda i: (layer, 0, 0))

    out = jax.ShapeDtypeStruct((dm.n, cw), BF16)
    return pl.pallas_call(
        functools.partial(_local_kernel, dm),
        out_shape=(out, out, out),
        grid=(dm.n // t,),
        in_specs=[
            cur(ca), prev(ca), nxt(ca),
            cur(ca + 1), prev(ca + 1), nxt(ca + 1),
            cur(ca + 2), prev(ca + 2), nxt(ca + 2),
            cur(ca + 3), cur(ca + 4),
            pl.BlockSpec((None, CONV_KERNEL, cw), lambda i: (layer, 0, 0)),
            vec(), vec(), vec(), vec(), vec(),
            pl.BlockSpec((None, SGU_GROUPS, CHUNK, CHUNK), lambda i: (layer, 0, 0, 0)),
            pl.BlockSpec((None, CHUNK, SGU_GROUPS), lambda i: (layer, 0, 0)),
        ],
        out_specs=(pl.BlockSpec((t, cw), lambda i: (i, 0)),) * 3,
        scratch_shapes=[pltpu.VMEM((t + 2 * HALO, cw), F32), pltpu.VMEM((t + 2 * HALO, cw), F32)],
        compiler_params=_params(("arbitrary",)),
        name="local_branches",
    )(proj, proj, proj, proj, proj, proj, proj, proj, proj, proj, proj,
      conv_w, conv_b, conv_ln_g, conv_ln_b, sgu_ln_g, sgu_ln_b, sgu_w, sgu_bt)


def _merge_kernel(dm, o_ref, cv_ref, pl_ref, sg_ref, g0_ref, g1_ref, g2_ref, g3_ref,
                  wa_ref, wc_ref, wp_ref, ws_ref, ps_ref, m_ref):
    cw = dm.cw
    for c in range(dm.d // cw):
        cols = slice(c * cw, (c + 1) * cw)
        ya = _dot(o_ref[...], wa_ref[:, cols])
        yb = _dot(cv_ref[...], wc_ref[:, cols])
        yc = _dot(pl_ref[:, c * LANES:(c + 1) * LANES], wp_ref[c]) * ps_ref[:, cols]
        yd = _dot(sg_ref[...], ws_ref[:, cols])
        merged = (_sigmoid(g0_ref[:, cols].astype(F32)) * ya
                  + _sigmoid(g1_ref[:, cols].astype(F32)) * yb
                  + _sigmoid(g2_ref[:, cols].astype(F32)) * yc
                  + _sigmoid(g3_ref[:, cols].astype(F32)) * yd)
        m_ref[:, cols] = merged.astype(m_ref.dtype)


def _merge(dm, layer, attn_o, conv_h, pool_h, sgu_h, proj, wa, wc, wp, ws, pool_scale):
    tm = ROW_TILE
    d, cw = dm.d, dm.cw
    g0 = dm.off_gates // d
    assert d // cw == POOL_GROUPS and wp.shape[-1] == cw

    def gate(b):
        return pl.BlockSpec((tm, d), lambda i: (i, g0 + b))

    return pl.pallas_call(
        functools.partial(_merge_kernel, dm),
        out_shape=jax.ShapeDtypeStruct((dm.n, d), BF16),
        grid=(dm.n // tm,),
        in_specs=[
            pl.BlockSpec((tm, ATTN_WIDTH), lambda i: (i, 0)),
            pl.BlockSpec((tm, cw), lambda i: (i, 0)),
            pl.BlockSpec((tm, cw), lambda i: (i, 0)),
            pl.BlockSpec((tm, cw), lambda i: (i, 0)),
            gate(0), gate(1), gate(2), gate(3),
            pl.BlockSpec((None, ATTN_WIDTH, d), lambda i: (layer, 0, 0)),
            pl.BlockSpec((None, cw, d), lambda i: (layer, 0, 0)),
            pl.BlockSpec((None, POOL_GROUPS, LANES, cw), lambda i: (layer, 0, 0, 0)),
            pl.BlockSpec((None, cw, d), lambda i: (layer, 0, 0)),
            pl.BlockSpec((None, 1, d), lambda i: (layer, 0, 0)),
        ],
        out_specs=pl.BlockSpec((tm, d), lambda i: (i, 0)),
        compiler_params=_params(("arbitrary",)),
        name="merge",
    )(attn_o, conv_h, pool_h, sgu_h, proj, proj, proj, proj, wa, wc, wp, ws, pool_scale)


def _out_proj_kernel(alpha, m_ref, x_ref, mod_ref, w_ref, lg_ref, lb_ref, wr_ref, br_ref,
                     x1_ref, h2_ref, idx_ref, comb_ref):
    gate1 = mod_ref[2:3, :]
    shift2 = mod_ref[3:4, :]
    scale2 = mod_ref[4:5, :]
    wr = wr_ref[...]
    wr_hi = wr.astype(BF16)
    wr_lo = (wr - wr_hi.astype(F32)).astype(BF16)

    half_rows = m_ref.shape[0] // 2
    for r in range(2):
        rows = slice(r * half_rows, (r + 1) * half_rows)
        y = _dot(m_ref[rows, :], w_ref[...])
        x1 = _layer_norm(alpha * x_ref[rows, :] + gate1 * y, lg_ref[...], lb_ref[...])
        x1_ref[rows, :] = x1
        h2 = x1 * (1.0 + scale2) + shift2
        h2_hi = h2.astype(BF16)
        packed = _pack_pairs(h2)
        for p in range(h2_ref.shape[0]):
            h2_ref[p, rows, :] = packed[:, p * GATHER_WORDS:(p + 1) * GATHER_WORDS]

        h2_lo = (h2 - h2_hi.astype(F32)).astype(BF16)
        logits = _dot(h2_hi, wr_hi) + _dot(h2_hi, wr_lo) + _dot(h2_lo, wr_hi) + br_ref[...]

        lane = lax.broadcasted_iota(jnp.int32, logits.shape, 1)
        idx_out = jnp.zeros(logits.shape, jnp.int32)
        val_out = jnp.zeros(logits.shape, F32)
        top = None
        den = jnp.zeros((logits.shape[0], 1), F32)
        for k in range(TOP_K):
            m = jnp.max(logits, axis=-1, keepdims=True)
            ik = jnp.min(jnp.where(logits == m, lane, LANES), axis=-1, keepdims=True)
            if top is None:
                top = m
            e = jnp.exp(m - top)
            den = den + e
            idx_out = jnp.where(lane == k, ik, idx_out)
            val_out = jnp.where(lane == k, e, val_out)
            logits = jnp.where(lane == ik, -jnp.inf, logits)
        idx_ref[rows, :] = idx_out
        comb_ref[rows, :] = val_out / den


def _out_proj(dm, layer, merged, x, mod, w_out, ln_g, ln_b, w_router, b_router):
    tm = 2 * ROW_TILE
    d = dm.d
    alpha = (2 * dm.depth) ** 0.25

    def vec():
        return pl.BlockSpec((None, 1, d), lambda i: (layer, 0, 0))

    return pl.pallas_call(
        functools.partial(_out_proj_kernel, alpha),
        out_shape=(
            jax.ShapeDtypeStruct((dm.n, d), F32),
            jax.ShapeDtypeStruct((d // 2 // GATHER_WORDS, dm.n, GATHER_WORDS), jnp.uint32),
            jax.ShapeDtypeStruct((dm.n, LANES), jnp.int32),
            jax.ShapeDtypeStruct((dm.n, LANES), F32),
        ),
        grid=(dm.n // tm,),
        in_specs=[
            pl.BlockSpec((tm, d), lambda i: (i, 0)),
            pl.BlockSpec((tm, d), lambda i: (i, 0)),
            pl.BlockSpec((None, None, N_MOD, d), lambda i: (layer, dm.group_of_tile(i, tm), 0, 0)),
            pl.BlockSpec((None, d, d), lambda i: (layer, 0, 0)),
            vec(), vec(),
            pl.BlockSpec((None, d, LANES), lambda i: (layer, 0, 0)),
            pl.BlockSpec((None, 1, LANES), lambda i: (layer, 0, 0)),
        ],
        out_specs=(
            pl.BlockSpec((tm, d), lambda i: (i, 0)),
            pl.BlockSpec((d // 2 // GATHER_WORDS, tm, GATHER_WORDS), lambda i: (0, i, 0)),
            pl.BlockSpec((tm, LANES), lambda i: (i, 0)),
            pl.BlockSpec((tm, LANES), lambda i: (i, 0)),
        ),
        compiler_params=_params(("arbitrary",)),
        name="out_proj",
    )(merged, x, mod, w_out, ln_g, ln_b, w_router, b_router)


def _moe_live_chunks(nv, fn):
    pair = 2 * MOE_SUB

    def sub_and_tail(base):
        fn(base, MOE_SUB)
        fn(base + MOE_SUB, MOE_TAIL)

    for base in range(0, MOE_SUPER, pair):
        n = nv - base
        pl.when(n > MOE_SUB + MOE_TAIL)(functools.partial(fn, base, pair))
        pl.when(jnp.logical_and(n > MOE_SUB, n <= MOE_SUB + MOE_TAIL))(functools.partial(sub_and_tail, base))
        pl.when(jnp.logical_and(n > MOE_TAIL, n <= MOE_SUB))(functools.partial(fn, base, MOE_SUB))
        pl.when(jnp.logical_and(n > 0, n <= MOE_TAIL))(functools.partial(fn, base, MOE_TAIL))


def _moe_kernel(nfa, be_ref, nv_ref, na_ref, x_ref, wg_ref, bg_ref, wu_ref, bu_ref, wd_ref, bd_ref,
                o_ref, act_ref, xf_ref):
    s = pl.program_id(0)
    j = pl.program_id(1)
    nv = nv_ref[s]
    tf = MOE_FF_TILE
    active = s < na_ref[0]

    @pl.when(jnp.logical_and(active, j == 0))
    def _():
        half = xf_ref.shape[1] // 2

        def unpack(c, carry):
            rows = pl.ds(pl.multiple_of(c * LANES, LANES), LANES)
            for p in range(x_ref.shape[0]):
                lo, hi = _unpack_pairs(x_ref[p, rows, :])
                xf_ref[rows, p * GATHER_WORDS:(p + 1) * GATHER_WORDS] = lo
                xf_ref[rows, half + p * GATHER_WORDS:half + (p + 1) * GATHER_WORDS] = hi
            return carry

        lax.fori_loop(0, MOE_SUPER // LANES, unpack, 0)

    @pl.when(jnp.logical_and(active, j < nfa))
    def _():
        bg = bg_ref[...]
        bu = bu_ref[...]
        col0 = pl.multiple_of(j * tf, tf)

        def chunk(base, m):
            xs = xf_ref[base:base + m, :]
            gt = jnp.minimum(_dot(xs, wg_ref[...]) + bg, SWIGLU_LIMIT)
            up = jnp.clip(_dot(xs, wu_ref[...]) + bu, -SWIGLU_LIMIT, SWIGLU_LIMIT)
            act = gt * _sigmoid(SWIGLU_ALPHA * gt) * (up + 1.0)
            act_ref[base:base + m, pl.ds(col0, tf)] = act.astype(BF16).astype(F32)

        _moe_live_chunks(nv, chunk)

    @pl.when(j >= nfa)
    def _():
        bd = bd_ref[...]

        def chunk(base, m):
            y = _dot(act_ref[base:base + m, :], wd_ref[...]) + bd
            o_ref[base:base + m, :] = _pack_pairs(y)

        _moe_live_chunks(nv, chunk)
        for base in range(0, MOE_SUPER, MOE_TAIL):
            @pl.when(nv <= base)
            def _():
                o_ref[base:base + MOE_TAIL, :] = jnp.zeros((MOE_TAIL, o_ref.shape[1]), o_ref.dtype)


def _moe(dm, layer, x_sorted, blk_e, blk_nv, n_act, w_gate, b_gate, w_up, b_up, w_down, b_down):
    d, ff, tf, tn, r = dm.d, dm.ff, MOE_FF_TILE, MOE_OUT_TILE, MOE_SUPER
    n_planes = x_sorted.shape[0]
    nsb = x_sorted.shape[1] // r
    nfa = ff // tf
    nfb = d // tn
    assert tn // 2 == GATHER_WORDS

    def blk(s, na):
        return jnp.minimum(s, na[0] - 1)

    def col_a(s, j, na):
        return jnp.where(s < na[0], jnp.minimum(j, nfa - 1), nfa - 1)

    def col_b(s, j, na):
        return jnp.where(s < na[0], jnp.maximum(j - nfa, 0), nfb - 1)

    grid_spec = pltpu.PrefetchScalarGridSpec(
        num_scalar_prefetch=3,
        grid=(nsb, nfa + nfb),
        in_specs=[
            pl.BlockSpec((n_planes, r, GATHER_WORDS), lambda s, j, be, nv, na: (0, blk(s, na), 0)),
            pl.BlockSpec((None, None, d, tf), lambda s, j, be, nv, na: (layer, be[s], 0, col_a(s, j, na))),
            pl.BlockSpec((None, None, 1, tf), lambda s, j, be, nv, na: (layer, be[s], 0, col_a(s, j, na))),
            pl.BlockSpec((None, None, d, tf), lambda s, j, be, nv, na: (layer, be[s], 0, col_a(s, j, na))),
            pl.BlockSpec((None, None, 1, tf), lambda s, j, be, nv, na: (layer, be[s], 0, col_a(s, j, na))),
            pl.BlockSpec((None, None, ff, tn), lambda s, j, be, nv, na: (layer, be[s], 0, col_b(s, j, na))),
            pl.BlockSpec((None, None, 1, tn), lambda s, j, be, nv, na: (layer, be[s], 0, col_b(s, j, na))),
        ],
        out_specs=pl.BlockSpec((None, r, tn // 2), lambda s, j, be, nv, na: (jnp.maximum(j - nfa, 0), s, 0)),
        scratch_shapes=[
            pltpu.VMEM((r, ff), F32),
            pltpu.VMEM((r, d), F32),
        ],
    )
    return pl.pallas_call(
        functools.partial(_moe_kernel, nfa),
        out_shape=jax.ShapeDtypeStruct((nfb, nsb * r, tn // 2), jnp.uint32),
        grid_spec=grid_spec,
        compiler_params=_params(("arbitrary", "arbitrary")),
        name="moe_ffn",
    )(blk_e, blk_nv, n_act, x_sorted, w_gate, b_gate, w_up, b_up, w_down, b_down)


def _route(dm, top_idx, n_super):
    e, r = dm.n_exp, MOE_SUPER
    flat_e = top_idx.reshape(-1)
    onehot = (flat_e[:, None] == jnp.arange(e, dtype=jnp.int32)[None, :]).astype(jnp.int32)
    rank = jnp.take_along_axis(jnp.cumsum(onehot, axis=0), flat_e[:, None], axis=1)[:, 0] - 1
    counts = jnp.sum(onehot, axis=0)
    nsb_e = (counts + r - 1) // r
    sb_end = jnp.cumsum(nsb_e)
    sb_start = sb_end - nsb_e
    dest = sb_start[flat_e] * r + rank
    n_act = sb_end[-1]
    sidx = jnp.arange(n_super, dtype=jnp.int32)
    blk_e = jnp.clip(jnp.searchsorted(sb_end, jnp.minimum(sidx, n_act - 1), side='right'), 0, e - 1)
    blk_e = blk_e.astype(jnp.int32)
    blk_nv = jnp.clip(counts[blk_e] - (sidx - sb_start[blk_e]) * r, 0, r)
    blk_nv = jnp.where(sidx < n_act, blk_nv, 0).astype(jnp.int32)
    return dest.astype(jnp.int32), blk_e, blk_nv, n_act.astype(jnp.int32).reshape(1)


def _combine_kernel(alpha, y_ref, comb_ref, x_ref, mod_ref, lg_ref, lb_ref, o_ref):
    gate2 = mod_ref[5:6, :]
    comb = comb_ref[...]
    slabs = []
    for b in range(y_ref.shape[0]):
        lo_sum = None
        hi_sum = None
        for k in range(TOP_K):
            lo, hi = _unpack_pairs(y_ref[b, k])
            wk = comb[:, k:k + 1]
            lo_sum = wk * lo if lo_sum is None else lo_sum + wk * lo
            hi_sum = wk * hi if hi_sum is None else hi_sum + wk * hi
        slabs += [lo_sum, hi_sum]
    ff = jnp.concatenate(slabs, axis=1)
    o_ref[...] = _layer_norm(alpha * x_ref[...] + gate2 * ff, lg_ref[...], lb_ref[...])


def _combine(dm, layer, y_tok, comb, x1, mod, ln_g, ln_b):
    tm = ROW_TILE
    d = dm.d
    alpha = (2 * dm.depth) ** 0.25
    return pl.pallas_call(
        functools.partial(_combine_kernel, alpha),
        out_shape=jax.ShapeDtypeStruct((dm.n, d), F32),
        grid=(dm.n // tm,),
        in_specs=[
            pl.BlockSpec((y_tok.shape[0], TOP_K, tm, y_tok.shape[3]), lambda i: (0, 0, i, 0)),
            pl.BlockSpec((tm, LANES), lambda i: (i, 0)),
            pl.BlockSpec((tm, d), lambda i: (i, 0)),
            pl.BlockSpec((None, None, N_MOD, d), lambda i: (layer, dm.group_of_tile(i, tm), 0, 0)),
            pl.BlockSpec((None, 1, d), lambda i: (layer, 0, 0)),
            pl.BlockSpec((None, 1, d), lambda i: (layer, 0, 0)),
        ],
        out_specs=pl.BlockSpec((tm, d), lambda i: (i, 0)),
        compiler_params=_params(("arbitrary",)),
        name="combine",
    )(y_tok, comb, x1, mod, ln_g, ln_b)


def kernel(x_prompt, x_sample, cache_k, cache_v, c, c_ctx, w_mod, b_mod, w_in, q_norm, k_norm,
           w_attn_o, conv_w, conv_b, conv_ln_g, conv_ln_b, w_conv_o, w_pool, pool_scale,
           sgu_ln_g, sgu_ln_b, sgu_w, sgu_b, w_sgu_o, w_out, ln1_g, ln1_b, ln2_g, ln2_b,
           w_router, b_router, w_gate, b_gate, w_up, b_up, w_down, b_down):
    bp, tp, d = x_prompt.shape
    bs, ts, _ = x_sample.shape
    depth, n_exp, _, ff = w_gate.shape
    past = cache_k.shape[2]
    dm = _Dims(bp, tp, bs, ts, d, past, depth, n_exp, ff)
    assert 1 + bs <= SUBLANES and n_exp <= LANES

    x = jnp.concatenate([x_prompt.reshape(dm.np_, d), x_sample.reshape(dm.ns, d)], axis=0)
    cond = jnp.concatenate([c_ctx[None, :], c, jnp.zeros((SUBLANES - 1 - bs, d), F32)], axis=0)
    mod = _modulation(cond, w_mod, b_mod).reshape(depth, SUBLANES, N_MOD, d)

    cos, sin = _rope_tables(dm, ROW_TILE)
    ck = cache_k.reshape(bs, depth, past, KV_WIDTH)
    cv = cache_v.reshape(bs, depth, past, KV_WIDTH)

    def row(p):
        return p.reshape(depth, 1, p.shape[-1])

    wa, wc, wp, ws, wo = (w.astype(BF16) for w in (w_attn_o, w_conv_o, w_pool, w_sgu_o, w_out))
    sgu_bt = jnp.swapaxes(sgu_b, 1, 2)
    w_router_p = jnp.pad(w_router, ((0, 0), (0, 0), (0, LANES - n_exp)))
    b_router_p = jnp.pad(b_router, ((0, 0), (0, LANES - n_exp)), constant_values=-jnp.inf)
    b_router_p = b_router_p.reshape(depth, 1, LANES)
    bg, bu = b_gate.reshape(depth, n_exp, 1, ff), b_up.reshape(depth, n_exp, 1, ff)
    bd = b_down.reshape(depth, n_exp, 1, d)

    nk = dm.n * TOP_K
    n_super = nk // MOE_SUPER + n_exp
    ks, vs = [], []
    for l in range(depth):
        proj = _in_proj(dm, l, x, mod, w_in)
        q, k, kn, v32 = _qk_prep(dm, l, proj, cos, sin, row(q_norm), row(k_norm))
        ks.append(kn[:dm.np_])
        vs.append(v32[:dm.np_])
        attn_o = jnp.concatenate(
            [_ctx_attn(dm, q, k, proj), _lat_attn(dm, l, q, k, proj, ck, cv)], axis=0)
        conv_h, pool_h, sgu_h = _local_branches(
            dm, l, proj, conv_w, row(conv_b), row(conv_ln_g), row(conv_ln_b),
            row(sgu_ln_g), row(sgu_ln_b), sgu_w, sgu_bt)
        merged = _merge(dm, l, attn_o, conv_h, pool_h, sgu_h, proj, wa, wc, wp, ws, row(pool_scale))
        x1, h2, top_idx, comb = _out_proj(dm, l, merged, x, mod, wo, row(ln1_g), row(ln1_b),
                                          w_router_p, b_router_p)
        dest, blk_e, blk_nv, n_act = _route(dm, top_idx[:, :TOP_K], n_super)
        n_slots = n_super * MOE_SUPER
        slot_tok = (jnp.arange(n_slots, dtype=jnp.int32) % dm.n).at[dest].set(
            jnp.arange(nk, dtype=jnp.int32) // TOP_K)
        x_sorted = _gather_rows(h2, slot_tok)
        y_sorted = _moe(dm, l, x_sorted, blk_e, blk_nv, n_act, w_gate, bg, w_up, bu, w_down, bd)
        y_tok = _gather_rows(y_sorted, dest.reshape(dm.n, TOP_K).T.reshape(-1))
        y_tok = y_tok.reshape(y_sorted.shape[0], TOP_K, dm.n, GATHER_WORDS)
        x = _combine(dm, l, y_tok, comb, x1, mod, row(ln2_g), row(ln2_b))

    y_prompt = x[:dm.np_].reshape(bp, tp, d)
    y_sample = x[dm.np_:].reshape(bs, ts, d)
    state_k = jnp.stack(ks, axis=1).reshape(bp, tp, depth, N_KV_HEADS, HEAD_DIM).swapaxes(1, 2)
    state_v = jnp.stack(vs, axis=1).reshape(bp, tp, depth, N_KV_HEADS, HEAD_DIM).swapaxes(1, 2)
    return (y_prompt, y_sample, state_k, state_v)
```

```python
import functools

import jax
import jax.numpy as jnp
from jax import lax
from jax.experimental import pallas as pl
from jax.experimental.pallas import tpu as pltpu
from jax.experimental.pallas import tpu_sc as plsc

F32 = jnp.float32
BF16 = jnp.bfloat16

N_HEADS = 8
N_KV_HEADS = 2
HEAD_DIM = 128
GRID_W = 64
ROPE_THETA = 10000.0
CONV_KERNEL = 31
POOL_WINDOWS = (2, 4, 8, 16)
POOL_GROUPS = 4
SGU_GROUPS = 4
CHUNK = 128
N_BRANCHES = 4
TOP_K = 4
SWIGLU_LIMIT = 7.0
SWIGLU_ALPHA = 1.702
NORM_EPS = 1e-6
N_MOD = 6

ATTN_WIDTH = N_HEADS * HEAD_DIM
KV_WIDTH = N_KV_HEADS * HEAD_DIM
GROUP = N_HEADS // N_KV_HEADS

LANES = 128
SUBLANES = 8
VMEM_LIMIT_BYTES = 56 * 1024 * 1024
MOE_VMEM_LIMIT_BYTES = 60 * 1024 * 1024

ROW_TILE = 256
HALO = 16
IN_PROJ_ROWS = 1024
IN_PROJ_COLS = 1024
MOD_COLS = 1024
MOE_SUPER = 2048
MOE_SUB = 256
MOE_TAIL = 128
MOE_FF_TILE = 512
MOE_OUT_TILE = 512
GATHER_WINDOW = 128
GATHER_WORDS = 256


def _dot(a, b):
    return jnp.dot(a, b, preferred_element_type=F32)


def _dot_nt(a, b):
    return lax.dot_general(a, b, (((1,), (1,)), ((), ())), preferred_element_type=F32)


def _layer_norm(x, g, b):
    mu = jnp.mean(x, axis=-1, keepdims=True)
    xc = x - mu
    var = jnp.mean(xc * xc, axis=-1, keepdims=True)
    return xc * lax.rsqrt(var + NORM_EPS) * g + b


def _sigmoid(x):
    return 0.5 * jnp.tanh(0.5 * x) + 0.5


def _params(sem, vmem=VMEM_LIMIT_BYTES):
    return pltpu.CompilerParams(dimension_semantics=sem, vmem_limit_bytes=vmem)


_HIGH_HALF = 0xFFFF0000


def _pack_pairs(x):
    w = x.shape[1] // 2
    lo = lax.bitcast_convert_type(x[:, :w].astype(BF16).astype(F32), jnp.uint32)
    hi = lax.bitcast_convert_type(x[:, w:].astype(BF16).astype(F32), jnp.uint32)
    return (lo >> 16) | (hi & jnp.uint32(_HIGH_HALF))


def _unpack_pairs(words):
    lo = lax.bitcast_convert_type(words << 16, F32)
    hi = lax.bitcast_convert_type(words & jnp.uint32(_HIGH_HALF), F32)
    return lo, hi


def _gather_rows(planes, idx):
    n_planes, n_rows, width = planes.shape
    assert planes.dtype.itemsize == 4 and width == GATHER_WORDS
    offsets = jnp.arange(n_planes, dtype=jnp.int32) * n_rows
    flat_idx = (offsets[:, None] + idx[None, :]).reshape(-1)
    out = _gather_pieces(planes.reshape(n_planes * n_rows, width), flat_idx)
    return out.reshape(n_planes, idx.shape[0], width)


def _gather_pieces(table, idx):
    n_idx = idx.shape[0]
    width = table.shape[1]
    assert n_idx % GATHER_WINDOW == 0
    mesh = plsc.VectorSubcoreMesh(core_axis_name="core", subcore_axis_name="subcore")

    @functools.partial(
        pl.kernel,
        out_type=jax.ShapeDtypeStruct((n_idx, width), table.dtype),
        mesh=mesh,
        scratch_types=[],
    )
    def gather_kernel(table_hbm, idx_hbm, out_hbm):
        def body(idx_vmem, out_vmem):
            pltpu.sync_copy(table_hbm.at[idx_vmem.at[0]], out_vmem)

        pltpu.emit_pipeline(
            body,
            grid=(n_idx // GATHER_WINDOW,),
            in_specs=[pl.BlockSpec((1, GATHER_WINDOW), lambda i: (0, i))],
            out_specs=[pl.BlockSpec((GATHER_WINDOW, width), lambda i: (i, 0))],
            core_axis_name=("core", "subcore"),
            dimension_semantics=(pltpu.PARALLEL,),
        )(idx_hbm, out_hbm)

    return gather_kernel(table, idx.reshape(1, n_idx))


class _Dims:
    def __init__(self, bp, tp, bs, ts, d, past, depth, n_exp, ff):
        self.bp, self.tp, self.bs, self.ts = bp, tp, bs, ts
        self.d, self.past, self.depth, self.n_exp, self.ff = d, past, depth, n_exp, ff
        self.np_ = bp * tp
        self.ns = bs * ts
        self.n = self.np_ + self.ns
        self.cw = d // 4
        self.in_w = ATTN_WIDTH + 2 * KV_WIDTH + 5 * self.cw + N_BRANCHES * d
        self.off_k = ATTN_WIDTH
        self.off_v = ATTN_WIDTH + KV_WIDTH
        self.off_a = ATTN_WIDTH + 2 * KV_WIDTH
        self.off_gates = self.off_a + 5 * self.cw
        assert self.cw == POOL_GROUPS * LANES == SGU_GROUPS * CHUNK
        assert self.off_a % self.cw == 0 and self.off_gates % d == 0
        assert tp % ROW_TILE == 0 and ts % ROW_TILE == 0 and self.np_ % ts == 0
        assert ts % GRID_W == 0

    def group_of_tile(self, i, tm):
        npt = self.np_ // tm
        return jnp.where(i < npt, 0, 1 + (i - npt) // (self.ts // tm))

    def seq_tile_pos(self, i, tm):
        npt = self.np_ // tm
        is_p = i < npt
        pos = jnp.where(is_p, i % (self.tp // tm), (i - npt) % (self.ts // tm))
        cnt = jnp.where(is_p, self.tp // tm, self.ts // tm)
        return pos, cnt


def _mod_kernel(c_ref, w_ref, b_ref, o_ref):
    c = c_ref[...]
    s = c * _sigmoid(c)
    o_ref[...] = _dot(s, w_ref[...]) + b_ref[...]


def _modulation(cond8, w_mod, b_mod):
    depth, d, width = w_mod.shape
    tn = MOD_COLS
    return pl.pallas_call(
        _mod_kernel,
        out_shape=jax.ShapeDtypeStruct((depth, SUBLANES, width), F32),
        grid=(depth, width // tn),
        in_specs=[
            pl.BlockSpec((SUBLANES, d), lambda l, j: (0, 0)),
            pl.BlockSpec((None, d, tn), lambda l, j: (l, 0, j)),
            pl.BlockSpec((None, 1, tn), lambda l, j: (l, 0, j)),
        ],
        out_specs=pl.BlockSpec((None, SUBLANES, tn), lambda l, j: (l, 0, j)),
        compiler_params=_params(("arbitrary", "arbitrary")),
        name="modulation",
    )(cond8, w_mod, b_mod.reshape(depth, 1, width))


def _in_proj_kernel(x_ref, mod_ref, w_ref, o_ref, wbf_ref):
    @pl.when(pl.program_id(1) == 0)
    def _():
        wbf_ref[...] = w_ref[...].astype(BF16)

    shift = mod_ref[0:1, :]
    scale = mod_ref[1:2, :]
    h = (x_ref[...] * (1.0 + scale) + shift).astype(BF16)
    o_ref[...] = _dot(h, wbf_ref[...]).astype(o_ref.dtype)


def _in_proj(dm, layer, x, mod, w_in):
    tm, tn = IN_PROJ_ROWS, IN_PROJ_COLS
    d = dm.d
    return pl.pallas_call(
        _in_proj_kernel,
        out_shape=jax.ShapeDtypeStruct((dm.n, dm.in_w), BF16),
        grid=(dm.in_w // tn, dm.n // tm),
        in_specs=[
            pl.BlockSpec((tm, d), lambda j, i: (i, 0)),
            pl.BlockSpec((None, None, N_MOD, d), lambda j, i: (layer, dm.group_of_tile(i, tm), 0, 0)),
            pl.BlockSpec((None, d, tn), lambda j, i: (layer, 0, j)),
        ],
        out_specs=pl.BlockSpec((tm, tn), lambda j, i: (i, j)),
        scratch_shapes=[pltpu.VMEM((d, tn), BF16)],
        compiler_params=_params(("arbitrary", "arbitrary")),
        name="in_proj",
    )(x, mod, w_in)


def _rope(x, cos, sin_signed):
    lane = lax.broadcasted_iota(jnp.int32, x.shape, 1)
    nxt = pltpu.roll(x, HEAD_DIM - 1, 1)
    prv = pltpu.roll(x, 1, 1)
    partner = jnp.where((lane & 1) == 0, nxt, prv)
    return x * cos + partner * sin_signed


def _rms(x, g):
    return x * lax.rsqrt(jnp.mean(x * x, axis=-1, keepdims=True) + NORM_EPS) * g


def _qk_prep_kernel(q_ref, k_ref, v_ref, cos_ref, sin_ref, qg_ref, kg_ref,
                    qo_ref, ko_ref, kn_ref, vo_ref):
    cos = cos_ref[...]
    sin = sin_ref[...]
    qg = qg_ref[...]
    kg = kg_ref[...]
    scale = HEAD_DIM ** -0.5
    for h in range(N_HEADS):
        cols = slice(h * HEAD_DIM, (h + 1) * HEAD_DIM)
        qn = _rms(q_ref[:, cols].astype(F32), qg)
        qo_ref[:, cols] = (_rope(qn, cos, sin) * scale).astype(qo_ref.dtype)
    for h in range(N_KV_HEADS):
        cols = slice(h * HEAD_DIM, (h + 1) * HEAD_DIM)
        kn = _rms(k_ref[:, cols].astype(F32), kg)
        kn_ref[:, cols] = kn
        ko_ref[:, cols] = _rope(kn, cos, sin).astype(ko_ref.dtype)
    vo_ref[...] = v_ref[...].astype(F32)


def _rope_tables(dm, tm):
    t = dm.ts
    rows = t // GRID_W
    row = jnp.broadcast_to(jnp.arange(rows, dtype=F32)[:, None], (rows, GRID_W)).reshape(t)
    col = jnp.broadcast_to(jnp.arange(GRID_W, dtype=F32)[None, :], (rows, GRID_W)).reshape(t)
    half = HEAD_DIM // 2
    inv_freq = ROPE_THETA ** (-jnp.arange(0, half, 2, dtype=F32) / half)
    ang = jnp.concatenate([row[:, None] * inv_freq, col[:, None] * inv_freq], axis=-1)
    cos = jnp.repeat(jnp.cos(ang), 2, axis=-1)
    sin = jnp.repeat(jnp.sin(ang), 2, axis=-1)
    sign = jnp.tile(jnp.array([-1.0, 1.0], F32), half)
    cos = jnp.concatenate([jnp.ones((tm, HEAD_DIM), F32), cos], axis=0)
    sin = jnp.concatenate([jnp.zeros((tm, HEAD_DIM), F32), sin * sign], axis=0)
    return cos, sin


def _qk_prep(dm, layer, proj, cos, sin, q_norm, k_norm):
    tm = ROW_TILE
    npt = dm.np_ // tm
    tps = dm.ts // tm

    def tab(i):
        return (jnp.where(i < npt, 0, 1 + (i - npt) % tps), 0)

    return pl.pallas_call(
        _qk_prep_kernel,
        out_shape=(
            jax.ShapeDtypeStruct((dm.n, ATTN_WIDTH), BF16),
            jax.ShapeDtypeStruct((dm.n, KV_WIDTH), BF16),
            jax.ShapeDtypeStruct((dm.n, KV_WIDTH), F32),
            jax.ShapeDtypeStruct((dm.n, KV_WIDTH), F32),
        ),
        grid=(dm.n // tm,),
        in_specs=[
            pl.BlockSpec((tm, ATTN_WIDTH), lambda i: (i, 0)),
            pl.BlockSpec((tm, KV_WIDTH), lambda i: (i, dm.off_k // KV_WIDTH)),
            pl.BlockSpec((tm, KV_WIDTH), lambda i: (i, dm.off_v // KV_WIDTH)),
            pl.BlockSpec((tm, HEAD_DIM), tab),
            pl.BlockSpec((tm, HEAD_DIM), tab),
            pl.BlockSpec((None, 1, HEAD_DIM), lambda i: (layer, 0, 0)),
            pl.BlockSpec((None, 1, HEAD_DIM), lambda i: (layer, 0, 0)),
        ],
        out_specs=(
            pl.BlockSpec((tm, ATTN_WIDTH), lambda i: (i, 0)),
            pl.BlockSpec((tm, KV_WIDTH), lambda i: (i, 0)),
            pl.BlockSpec((tm, KV_WIDTH), lambda i: (i, 0)),
            pl.BlockSpec((tm, KV_WIDTH), lambda i: (i, 0)),
        ),
        compiler_params=_params(("arbitrary",)),
        name="qk_prep",
    )(proj, proj, proj, cos, sin, q_norm, k_norm)


def _ctx_attn_kernel(q_ref, k_ref, v_ref, o_ref):
    for h in range(N_HEADS):
        kv = h // GROUP
        cols = slice(h * HEAD_DIM, (h + 1) * HEAD_DIM)
        kcols = slice(kv * HEAD_DIM, (kv + 1) * HEAD_DIM)
        s = _dot_nt(q_ref[:, cols], k_ref[:, kcols])
        m = jnp.max(s, axis=-1, keepdims=True)
        p = jnp.exp(s - m)
        l = jnp.sum(p, axis=-1, keepdims=True)
        o = _dot(p.astype(BF16), v_ref[:, kcols]) / l
        o_ref[:, cols] = o.astype(o_ref.dtype)


def _ctx_attn(dm, q, k, proj):
    t = dm.tp
    return pl.pallas_call(
        _ctx_attn_kernel,
        out_shape=jax.ShapeDtypeStruct((dm.np_, ATTN_WIDTH), BF16),
        grid=(dm.bp,),
        in_specs=[
            pl.BlockSpec((t, ATTN_WIDTH), lambda b: (b, 0)),
            pl.BlockSpec((t, KV_WIDTH), lambda b: (b, 0)),
            pl.BlockSpec((t, KV_WIDTH), lambda b: (b, dm.off_v // KV_WIDTH)),
        ],
        out_specs=pl.BlockSpec((t, ATTN_WIDTH), lambda b: (b, 0)),
        compiler_params=_params(("arbitrary",)),
        name="ctx_attn",
    )(q, k, proj)


def _lat_attn_kernel(q_ref, k_ref, v_ref, ck_ref, cv_ref, o_ref):
    ck = ck_ref[...].astype(BF16)
    cv = cv_ref[...].astype(BF16)
    for h in range(N_HEADS):
        kv = h // GROUP
        cols = slice(h * HEAD_DIM, (h + 1) * HEAD_DIM)
        kcols = slice(kv * HEAD_DIM, (kv + 1) * HEAD_DIM)
        qh = q_ref[:, cols]
        s_ctx = _dot_nt(qh, ck[:, kcols])
        s_lat = _dot_nt(qh, k_ref[:, kcols])
        m = jnp.maximum(jnp.max(s_ctx, axis=-1, keepdims=True),
                        jnp.max(s_lat, axis=-1, keepdims=True))
        p_ctx = jnp.exp(s_ctx - m)
        p_lat = jnp.exp(s_lat - m)
        l = jnp.sum(p_ctx, axis=-1, keepdims=True) + jnp.sum(p_lat, axis=-1, keepdims=True)
        o = _dot(p_ctx.astype(BF16), cv[:, kcols]) + _dot(p_lat.astype(BF16), v_ref[:, kcols])
        o_ref[:, cols] = (o / l).astype(o_ref.dtype)


def _lat_attn(dm, layer, q, k, proj, cache_k, cache_v):
    tq = ROW_TILE
    ts = dm.ts
    qb = ts // tq
    return pl.pallas_call(
        _lat_attn_kernel,
        out_shape=jax.ShapeDtypeStruct((dm.ns, ATTN_WIDTH), BF16),
        grid=(dm.bs, qb),
        in_specs=[
            pl.BlockSpec((tq, ATTN_WIDTH), lambda b, i: (dm.np_ // tq + b * qb + i, 0)),
            pl.BlockSpec((ts, KV_WIDTH), lambda b, i: (dm.np_ // ts + b, 0)),
            pl.BlockSpec((ts, KV_WIDTH), lambda b, i: (dm.np_ // ts + b, dm.off_v // KV_WIDTH)),
            pl.BlockSpec((None, None, dm.past, KV_WIDTH), lambda b, i: (b, layer, 0, 0)),
            pl.BlockSpec((None, None, dm.past, KV_WIDTH), lambda b, i: (b, layer, 0, 0)),
        ],
        out_specs=pl.BlockSpec((tq, ATTN_WIDTH), lambda b, i: (b * qb + i, 0)),
        compiler_params=_params(("arbitrary", "arbitrary")),
        name="lat_attn",
    )(q, k, proj, cache_k, cache_v)


def _local_kernel(dm, a_ref, ap_ref, an_ref, g_ref, gp_ref, gn_ref, z_ref, zp_ref, zn_ref,
                  u_ref, v_ref, cw_ref, cb_ref, clg_ref, clb_ref, slg_ref, slb_ref,
                  sw_ref, sbt_ref, conv_o, pool_o, sgu_o, ubuf, zbuf):
    t = ROW_TILE
    i = pl.program_id(0)
    pos, cnt = dm.seq_tile_pos(i, t)
    has_prev = (pos > 0).astype(F32)
    has_next = (pos < cnt - 1).astype(F32)

    def glu(a, g):
        return a[...].astype(F32) * _sigmoid(g[...].astype(F32))

    ubuf[0:HALO, :] = glu(ap_ref, gp_ref) * has_prev
    ubuf[HALO:HALO + t, :] = glu(a_ref, g_ref)
    ubuf[HALO + t:, :] = glu(an_ref, gn_ref) * has_next
    zbuf[0:HALO, :] = zp_ref[...].astype(F32) * has_prev
    zbuf[HALO:HALO + t, :] = z_ref[...].astype(F32)
    zbuf[HALO + t:, :] = zn_ref[...].astype(F32) * has_next

    half_rows = t // 2
    centre = CONV_KERNEL // 2
    for r in range(2):
        pieces = []
        for c in range(dm.cw // LANES):
            cols = slice(c * LANES, (c + 1) * LANES)
            acc = jnp.zeros((half_rows, LANES), F32)
            for k in range(CONV_KERNEL):
                start = HALO + r * half_rows + k - centre
                acc = acc + cw_ref[k:k + 1, cols] * ubuf[start:start + half_rows, cols]
            pieces.append(acc)
        y = jnp.concatenate(pieces, axis=1) + cb_ref[...]
        y = _layer_norm(y, clg_ref[...], clb_ref[...])
        conv_o[r * half_rows:(r + 1) * half_rows, :] = (y * _sigmoid(y)).astype(conv_o.dtype)

    seq_len = jnp.where(i < dm.np_ // t, dm.tp, dm.ts)
    tok = pos * t + lax.broadcasted_iota(jnp.int32, (t, LANES), 0)
    for gi, w in enumerate(POOL_WINDOWS):
        cols = slice(gi * LANES, (gi + 1) * LANES)
        acc = jnp.zeros((t, LANES), F32)
        for j in range(-(w // 2), w - w // 2):
            acc = acc + zbuf[HALO + j:HALO + j + t, cols]
        lo = jnp.maximum(tok - w // 2, 0)
        hi = jnp.minimum(tok - w // 2 + w, seq_len)
        mean = acc / (hi - lo).astype(F32)
        pool_o[:, cols] = (mean - zbuf[HALO:HALO + t, cols]).astype(pool_o.dtype)

    vn = _layer_norm(v_ref[...].astype(F32), slg_ref[...], slb_ref[...]).astype(BF16)
    for gi in range(SGU_GROUPS):
        cols = slice(gi * CHUNK, (gi + 1) * CHUNK)
        w = sw_ref[gi].astype(BF16)
        bias = sbt_ref[:, gi:gi + 1]
        for c in range(t // CHUNK):
            rows = slice(c * CHUNK, (c + 1) * CHUNK)
            mixed = _dot(w, vn[rows, cols]) + bias
            sgu_o[rows, cols] = (u_ref[rows, cols].astype(F32) * mixed).astype(sgu_o.dtype)


def _local_branches(dm, layer, proj, conv_w, conv_b, conv_ln_g, conv_ln_b,
                    sgu_ln_g, sgu_ln_b, sgu_w, sgu_bt):
    t = ROW_TILE
    cw = dm.cw
    hb = t // HALO
    last = dm.n // HALO - 1
    ca = dm.off_a // cw

    def cur(c):
        return pl.BlockSpec((t, cw), lambda i: (i, c))

    def prev(c):
        return pl.BlockSpec((HALO, cw), lambda i: (jnp.maximum(i * hb - 1, 0), c))

    def nxt(c):
        return pl.BlockSpec((HALO, cw), lambda i: (jnp.minimum((i + 1) * hb, last), c))

    def vec():
        return pl.BlockSpec((None, 1, cw), lambda i: (layer, 0, 0))

    out = jax.ShapeDtypeStruct((dm.n, cw), BF16)
    return pl.pallas_call(
        functools.partial(_local_kernel, dm),
        out_shape=(out, out, out),
        grid=(dm.n // t,),
        in_specs=[
            cur(ca), prev(ca), nxt(ca),
            cur(ca + 1), prev(ca + 1), nxt(ca + 1),
            cur(ca + 2), prev(ca + 2), nxt(ca + 2),
            cur(ca + 3), cur(ca + 4),
            pl.BlockSpec((None, CONV_KERNEL, cw), lambda i: (layer, 0, 0)),
            vec(), vec(), vec(), vec(), vec(),
            pl.BlockSpec((None, SGU_GROUPS, CHUNK, CHUNK), lambda i: (layer, 0, 0, 0)),
            pl.BlockSpec((None, CHUNK, SGU_GROUPS), lambda i: (layer, 0, 0)),
        ],
        out_specs=(pl.BlockSpec((t, cw), lambda i: (i, 0)),) * 3,
        scratch_shapes=[pltpu.VMEM((t + 2 * HALO, cw), F32), pltpu.VMEM((t + 2 * HALO, cw), F32)],
        compiler_params=_params(("arbitrary",)),
        name="local_branches",
    )(proj, proj, proj, proj, proj, proj, proj, proj, proj, proj, proj,
      conv_w, conv_b, conv_ln_g, conv_ln_b, sgu_ln_g, sgu_ln_b, sgu_w, sgu_bt)


def _merge_kernel(dm, o_ref, cv_ref, pl_ref, sg_ref, g0_ref, g1_ref, g2_ref, g3_ref,
                  wa_ref, wc_ref, wp_ref, ws_ref, ps_ref, m_ref):
    cw = dm.cw
    for c in range(dm.d // cw):
        cols = slice(c * cw, (c + 1) * cw)
        ya = _dot(o_ref[...], wa_ref[:, cols])
        yb = _dot(cv_ref[...], wc_ref[:, cols])
        yc = _dot(pl_ref[:, c * LANES:(c + 1) * LANES], wp_ref[c]) * ps_ref[:, cols]
        yd = _dot(sg_ref[...], ws_ref[:, cols])
        merged = (_sigmoid(g0_ref[:, cols].astype(F32)) * ya
                  + _sigmoid(g1_ref[:, cols].astype(F32)) * yb
                  + _sigmoid(g2_ref[:, cols].astype(F32)) * yc
                  + _sigmoid(g3_ref[:, cols].astype(F32)) * yd)
        m_ref[:, cols] = merged.astype(m_ref.dtype)


def _merge(dm, layer, attn_o, conv_h, pool_h, sgu_h, proj, wa, wc, wp, ws, pool_scale):
    tm = ROW_TILE
    d, cw = dm.d, dm.cw
    g0 = dm.off_gates // d
    assert d // cw == POOL_GROUPS and wp.shape[-1] == cw

    def gate(b):
        return pl.BlockSpec((tm, d), lambda i: (i, g0 + b))

    return pl.pallas_call(
        functools.partial(_merge_kernel, dm),
        out_shape=jax.ShapeDtypeStruct((dm.n, d), BF16),
        grid=(dm.n // tm,),
        in_specs=[
            pl.BlockSpec((tm, ATTN_WIDTH), lambda i: (i, 0)),
            pl.BlockSpec((tm, cw), lambda i: (i, 0)),
            pl.BlockSpec((tm, cw), lambda i: (i, 0)),
            pl.BlockSpec((tm, cw), lambda i: (i, 0)),
            gate(0), gate(1), gate(2), gate(3),
            pl.BlockSpec((None, ATTN_WIDTH, d), lambda i: (layer, 0, 0)),
            pl.BlockSpec((None, cw, d), lambda i: (layer, 0, 0)),
            pl.BlockSpec((None, POOL_GROUPS, LANES, cw), lambda i: (layer, 0, 0, 0)),
            pl.BlockSpec((None, cw, d), lambda i: (layer, 0, 0)),
            pl.BlockSpec((None, 1, d), lambda i: (layer, 0, 0)),
        ],
        out_specs=pl.BlockSpec((tm, d), lambda i: (i, 0)),
        compiler_params=_params(("arbitrary",)),
        name="merge",
    )(attn_o, conv_h, pool_h, sgu_h, proj, proj, proj, proj, wa, wc, wp, ws, pool_scale)


def _out_proj_kernel(alpha, m_ref, x_ref, mod_ref, w_ref, lg_ref, lb_ref, wr_ref, br_ref,
                     x1_ref, h2_ref, idx_ref, comb_ref):
    gate1 = mod_ref[2:3, :]
    shift2 = mod_ref[3:4, :]
    scale2 = mod_ref[4:5, :]
    wr = wr_ref[...]
    wr_hi = wr.astype(BF16)
    wr_lo = (wr - wr_hi.astype(F32)).astype(BF16)

    half_rows = m_ref.shape[0] // 2
    for r in range(2):
        rows = slice(r * half_rows, (r + 1) * half_rows)
        y = _dot(m_ref[rows, :], w_ref[...])
        x1 = _layer_norm(alpha * x_ref[rows, :] + gate1 * y, lg_ref[...], lb_ref[...])
        x1_ref[rows, :] = x1
        h2 = x1 * (1.0 + scale2) + shift2
        h2_hi = h2.astype(BF16)
        packed = _pack_pairs(h2)
        for p in range(h2_ref.shape[0]):
            h2_ref[p, rows, :] = packed[:, p * GATHER_WORDS:(p + 1) * GATHER_WORDS]

        h2_lo = (h2 - h2_hi.astype(F32)).astype(BF16)
        logits = _dot(h2_hi, wr_hi) + _dot(h2_hi, wr_lo) + _dot(h2_lo, wr_hi) + br_ref[...]

        lane = lax.broadcasted_iota(jnp.int32, logits.shape, 1)
        idx_out = jnp.zeros(logits.shape, jnp.int32)
        val_out = jnp.zeros(logits.shape, F32)
        top = None
        den = jnp.zeros((logits.shape[0], 1), F32)
        for k in range(TOP_K):
            m = jnp.max(logits, axis=-1, keepdims=True)
            ik = jnp.min(jnp.where(logits == m, lane, LANES), axis=-1, keepdims=True)
            if top is None:
                top = m
            e = jnp.exp(m - top)
            den = den + e
            idx_out = jnp.where(lane == k, ik, idx_out)
            val_out = jnp.where(lane == k, e, val_out)
            logits = jnp.where(lane == ik, -jnp.inf, logits)
        idx_ref[rows, :] = idx_out
        comb_ref[rows, :] = val_out / den


def _out_proj(dm, layer, merged, x, mod, w_out, ln_g, ln_b, w_router, b_router):
    tm = 2 * ROW_TILE
    d = dm.d
    alpha = (2 * dm.depth) ** 0.25

    def vec():
        return pl.BlockSpec((None, 1, d), lambda i: (layer, 0, 0))

    return pl.pallas_call(
        functools.partial(_out_proj_kernel, alpha),
        out_shape=(
            jax.ShapeDtypeStruct((dm.n, d), F32),
            jax.ShapeDtypeStruct((d // 2 // GATHER_WORDS, dm.n, GATHER_WORDS), jnp.uint32),
            jax.ShapeDtypeStruct((dm.n, LANES), jnp.int32),
            jax.ShapeDtypeStruct((dm.n, LANES), F32),
        ),
        grid=(dm.n // tm,),
        in_specs=[
            pl.BlockSpec((tm, d), lambda i: (i, 0)),
            pl.BlockSpec((tm, d), lambda i: (i, 0)),
            pl.BlockSpec((None, None, N_MOD, d), lambda i: (layer, dm.group_of_tile(i, tm), 0, 0)),
            pl.BlockSpec((None, d, d), lambda i: (layer, 0, 0)),
            vec(), vec(),
            pl.BlockSpec((None, d, LANES), lambda i: (layer, 0, 0)),
            pl.BlockSpec((None, 1, LANES), lambda i: (layer, 0, 0)),
        ],
        out_specs=(
            pl.BlockSpec((tm, d), lambda i: (i, 0)),
            pl.BlockSpec((d // 2 // GATHER_WORDS, tm, GATHER_WORDS), lambda i: (0, i, 0)),
            pl.BlockSpec((tm, LANES), lambda i: (i, 0)),
            pl.BlockSpec((tm, LANES), lambda i: (i, 0)),
        ),
        compiler_params=_params(("arbitrary",)),
        name="out_proj",
    )(merged, x, mod, w_out, ln_g, ln_b, w_router, b_router)


def _moe_live_chunks(nv, fn):
    pair = 2 * MOE_SUB
    n_pairs = nv // pair

    def full_pair(i, carry):
        fn(pl.multiple_of(i * pair, pair), pair)
        return carry

    lax.fori_loop(0, n_pairs, full_pair, 0)

    base = pl.multiple_of(n_pairs * pair, pair)
    n = nv - base

    def sub_and_tail():
        fn(base, MOE_SUB)
        fn(base + MOE_SUB, MOE_TAIL)

    pl.when(n > MOE_SUB + MOE_TAIL)(functools.partial(fn, base, pair))
    pl.when(jnp.logical_and(n > MOE_SUB, n <= MOE_SUB + MOE_TAIL))(sub_and_tail)
    pl.when(jnp.logical_and(n > MOE_TAIL, n <= MOE_SUB))(functools.partial(fn, base, MOE_SUB))
    pl.when(jnp.logical_and(n > 0, n <= MOE_TAIL))(functools.partial(fn, base, MOE_TAIL))


def _moe_kernel(nfa, be_ref, nv_ref, na_ref, x_ref, wg_ref, bg_ref, wu_ref, bu_ref, wd_ref, bd_ref,
                o_ref, act_ref):
    s = pl.program_id(0)
    j = pl.program_id(1)
    nv = nv_ref[s]
    tf = MOE_FF_TILE
    active = s < na_ref[0]

    @pl.when(jnp.logical_and(active, j < nfa))
    def _():
        bg = bg_ref[...]
        bu = bu_ref[...]
        col0 = pl.multiple_of(j * tf, tf)

        def chunk(base, m):
            halves = [_unpack_pairs(x_ref[p, pl.ds(base, m), :]) for p in range(x_ref.shape[0])]
            xs = jnp.concatenate([lo for lo, _ in halves] + [hi for _, hi in halves], axis=1)
            gt = jnp.minimum(_dot(xs, wg_ref[...]) + bg, SWIGLU_LIMIT)
            up = jnp.clip(_dot(xs, wu_ref[...]) + bu, -SWIGLU_LIMIT, SWIGLU_LIMIT)
            act = gt * _sigmoid(SWIGLU_ALPHA * gt) * (up + 1.0)
            act_ref[pl.ds(base, m), pl.ds(col0, tf)] = act.astype(BF16)

        _moe_live_chunks(nv, chunk)

    @pl.when(j >= nfa)
    def _():
        bd = bd_ref[...]

        def chunk(base, m):
            y = _dot(act_ref[pl.ds(base, m), :].astype(F32), wd_ref[...]) + bd
            o_ref[pl.ds(base, m), :] = _pack_pairs(y)

        _moe_live_chunks(nv, chunk)

        def zero_fill(c, carry):
            rows = pl.ds(pl.multiple_of(c * MOE_TAIL, MOE_TAIL), MOE_TAIL)
            o_ref[rows, :] = jnp.zeros((MOE_TAIL, o_ref.shape[1]), o_ref.dtype)
            return carry

        lax.fori_loop((nv + MOE_TAIL - 1) // MOE_TAIL, MOE_SUPER // MOE_TAIL, zero_fill, 0)


def _moe(dm, layer, x_sorted, blk_e, blk_nv, n_act, w_gate, b_gate, w_up, b_up, w_down, b_down):
    d, ff, tf, tn, r = dm.d, dm.ff, MOE_FF_TILE, MOE_OUT_TILE, MOE_SUPER
    n_planes = x_sorted.shape[0]
    nsb = x_sorted.shape[1] // r
    nfa = ff // tf
    nfb = d // tn
    assert tn // 2 == GATHER_WORDS

    def blk(s, na):
        return jnp.minimum(s, na[0] - 1)

    def col_a(s, j, na):
        return jnp.where(s < na[0], jnp.minimum(j, nfa - 1), nfa - 1)

    def col_b(s, j, na):
        return jnp.where(s < na[0], jnp.maximum(j - nfa, 0), nfb - 1)

    grid_spec = pltpu.PrefetchScalarGridSpec(
        num_scalar_prefetch=3,
        grid=(nsb, nfa + nfb),
        in_specs=[
            pl.BlockSpec((n_planes, r, GATHER_WORDS), lambda s, j, be, nv, na: (0, blk(s, na), 0)),
            pl.BlockSpec((None, None, d, tf), lambda s, j, be, nv, na: (layer, be[s], 0, col_a(s, j, na))),
            pl.BlockSpec((None, None, 1, tf), lambda s, j, be, nv, na: (layer, be[s], 0, col_a(s, j, na))),
            pl.BlockSpec((None, None, d, tf), lambda s, j, be, nv, na: (layer, be[s], 0, col_a(s, j, na))),
            pl.BlockSpec((None, None, 1, tf), lambda s, j, be, nv, na: (layer, be[s], 0, col_a(s, j, na))),
            pl.BlockSpec((None, None, ff, tn), lambda s, j, be, nv, na: (layer, be[s], 0, col_b(s, j, na))),
            pl.BlockSpec((None, None, 1, tn), lambda s, j, be, nv, na: (layer, be[s], 0, col_b(s, j, na))),
        ],
        out_specs=pl.BlockSpec((None, r, tn // 2), lambda s, j, be, nv, na: (jnp.maximum(j - nfa, 0), s, 0)),
        scratch_shapes=[
            pltpu.VMEM((r, ff), BF16),
        ],
    )
    return pl.pallas_call(
        functools.partial(_moe_kernel, nfa),
        out_shape=jax.ShapeDtypeStruct((nfb, nsb * r, tn // 2), jnp.uint32),
        grid_spec=grid_spec,
        compiler_params=_params(("arbitrary", "arbitrary"), MOE_VMEM_LIMIT_BYTES),
        name="moe_ffn",
    )(blk_e, blk_nv, n_act, x_sorted, w_gate, b_gate, w_up, b_up, w_down, b_down)


def _route(dm, top_idx, n_super):
    e, r = dm.n_exp, MOE_SUPER
    flat_e = top_idx.reshape(-1)
    onehot = (flat_e[:, None] == jnp.arange(e, dtype=jnp.int32)[None, :]).astype(jnp.int32)
    rank = jnp.take_along_axis(jnp.cumsum(onehot, axis=0), flat_e[:, None], axis=1)[:, 0] - 1
    counts = jnp.sum(onehot, axis=0)
    nsb_e = (counts + r - 1) // r
    sb_end = jnp.cumsum(nsb_e)
    sb_start = sb_end - nsb_e
    dest = sb_start[flat_e] * r + rank
    n_act = sb_end[-1]
    sidx = jnp.arange(n_super, dtype=jnp.int32)
    blk_e = jnp.clip(jnp.searchsorted(sb_end, jnp.minimum(sidx, n_act - 1), side='right'), 0, e - 1)
    blk_e = blk_e.astype(jnp.int32)
    blk_nv = jnp.clip(counts[blk_e] - (sidx - sb_start[blk_e]) * r, 0, r)
    blk_nv = jnp.where(sidx < n_act, blk_nv, 0).astype(jnp.int32)
    return dest.astype(jnp.int32), blk_e, blk_nv, n_act.astype(jnp.int32).reshape(1)


def _combine_kernel(alpha, y_ref, comb_ref, x_ref, mod_ref, lg_ref, lb_ref, o_ref):
    gate2 = mod_ref[5:6, :]
    comb = comb_ref[...]
    slabs = []
    for b in range(y_ref.shape[0]):
        lo_sum = None
        hi_sum = None
        for k in range(TOP_K):
            lo, hi = _unpack_pairs(y_ref[b, k])
            wk = comb[:, k:k + 1]
            lo_sum = wk * lo if lo_sum is None else lo_sum + wk * lo
            hi_sum = wk * hi if hi_sum is None else hi_sum + wk * hi
        slabs += [lo_sum, hi_sum]
    ff = jnp.concatenate(slabs, axis=1)
    o_ref[...] = _layer_norm(alpha * x_ref[...] + gate2 * ff, lg_ref[...], lb_ref[...])


def _combine(dm, layer, y_tok, comb, x1, mod, ln_g, ln_b):
    tm = ROW_TILE
    d = dm.d
    alpha = (2 * dm.depth) ** 0.25
    return pl.pallas_call(
        functools.partial(_combine_kernel, alpha),
        out_shape=jax.ShapeDtypeStruct((dm.n, d), F32),
        grid=(dm.n // tm,),
        in_specs=[
            pl.BlockSpec((y_tok.shape[0], TOP_K, tm, y_tok.shape[3]), lambda i: (0, 0, i, 0)),
            pl.BlockSpec((tm, LANES), lambda i: (i, 0)),
            pl.BlockSpec((tm, d), lambda i: (i, 0)),
            pl.BlockSpec((None, None, N_MOD, d), lambda i: (layer, dm.group_of_tile(i, tm), 0, 0)),
            pl.BlockSpec((None, 1, d), lambda i: (layer, 0, 0)),
            pl.BlockSpec((None, 1, d), lambda i: (layer, 0, 0)),
        ],
        out_specs=pl.BlockSpec((tm, d), lambda i: (i, 0)),
        compiler_params=_params(("arbitrary",)),
        name="combine",
    )(y_tok, comb, x1, mod, ln_g, ln_b)


def kernel(x_prompt, x_sample, cache_k, cache_v, c, c_ctx, w_mod, b_mod, w_in, q_norm, k_norm,
           w_attn_o, conv_w, conv_b, conv_ln_g, conv_ln_b, w_conv_o, w_pool, pool_scale,
           sgu_ln_g, sgu_ln_b, sgu_w, sgu_b, w_sgu_o, w_out, ln1_g, ln1_b, ln2_g, ln2_b,
           w_router, b_router, w_gate, b_gate, w_up, b_up, w_down, b_down):
    bp, tp, d = x_prompt.shape
    bs, ts, _ = x_sample.shape
    depth, n_exp, _, ff = w_gate.shape
    past = cache_k.shape[2]
    dm = _Dims(bp, tp, bs, ts, d, past, depth, n_exp, ff)
    assert 1 + bs <= SUBLANES and n_exp <= LANES

    x = jnp.concatenate([x_prompt.reshape(dm.np_, d), x_sample.reshape(dm.ns, d)], axis=0)
    cond = jnp.concatenate([c_ctx[None, :], c, jnp.zeros((SUBLANES - 1 - bs, d), F32)], axis=0)
    mod = _modulation(cond, w_mod, b_mod).reshape(depth, SUBLANES, N_MOD, d)

    cos, sin = _rope_tables(dm, ROW_TILE)
    ck = cache_k.reshape(bs, depth, past, KV_WIDTH)
    cv = cache_v.reshape(bs, depth, past, KV_WIDTH)

    def row(p):
        return p.reshape(depth, 1, p.shape[-1])

    wa, wc, wp, ws, wo = (w.astype(BF16) for w in (w_attn_o, w_conv_o, w_pool, w_sgu_o, w_out))
    sgu_bt = jnp.swapaxes(sgu_b, 1, 2)
    w_router_p = jnp.pad(w_router, ((0, 0), (0, 0), (0, LANES - n_exp)))
    b_router_p = jnp.pad(b_router, ((0, 0), (0, LANES - n_exp)), constant_values=-jnp.inf)
    b_router_p = b_router_p.reshape(depth, 1, LANES)
    bg, bu = b_gate.reshape(depth, n_exp, 1, ff), b_up.reshape(depth, n_exp, 1, ff)
    bd = b_down.reshape(depth, n_exp, 1, d)

    nk = dm.n * TOP_K
    n_super = nk // MOE_SUPER + n_exp
    ks, vs = [], []
    for l in range(depth):
        proj = _in_proj(dm, l, x, mod, w_in)
        q, k, kn, v32 = _qk_prep(dm, l, proj, cos, sin, row(q_norm), row(k_norm))
        ks.append(kn[:dm.np_])
        vs.append(v32[:dm.np_])
        attn_o = jnp.concatenate(
            [_ctx_attn(dm, q, k, proj), _lat_attn(dm, l, q, k, proj, ck, cv)], axis=0)
        conv_h, pool_h, sgu_h = _local_branches(
            dm, l, proj, conv_w, row(conv_b), row(conv_ln_g), row(conv_ln_b),
            row(sgu_ln_g), row(sgu_ln_b), sgu_w, sgu_bt)
        merged = _merge(dm, l, attn_o, conv_h, pool_h, sgu_h, proj, wa, wc, wp, ws, row(pool_scale))
        x1, h2, top_idx, comb = _out_proj(dm, l, merged, x, mod, wo, row(ln1_g), row(ln1_b),
                                          w_router_p, b_router_p)
        dest, blk_e, blk_nv, n_act = _route(dm, top_idx[:, :TOP_K], n_super)
        n_slots = n_super * MOE_SUPER
        slot_tok = (jnp.arange(n_slots, dtype=jnp.int32) % dm.n).at[dest].set(
            jnp.arange(nk, dtype=jnp.int32) // TOP_K)
        x_sorted = _gather_rows(h2, slot_tok)
        y_sorted = _moe(dm, l, x_sorted, blk_e, blk_nv, n_act, w_gate, bg, w_up, bu, w_down, bd)
        y_tok = _gather_rows(y_sorted, dest.reshape(dm.n, TOP_K).T.reshape(-1))
        y_tok = y_tok.reshape(y_sorted.shape[0], TOP_K, dm.n, GATHER_WORDS)
        x = _combine(dm, l, y_tok, comb, x1, mod, row(ln2_g), row(ln2_b))

    y_prompt = x[:dm.np_].reshape(bp, tp, d)
    y_sample = x[dm.np_:].reshape(bs, ts, d)
    state_k = jnp.stack(ks, axis=1).reshape(bp, tp, depth, N_KV_HEADS, HEAD_DIM).swapaxes(1, 2)
    state_v = jnp.stack(vs, axis=1).reshape(bp, tp, depth, N_KV_HEADS, HEAD_DIM).swapaxes(1, 2)
    return (y_prompt, y_sample, state_k, state_v)
```

```python
import functools

import jax
import jax.numpy as jnp
from jax import lax
from jax.experimental import pallas as pl
from jax.experimental.pallas import tpu as pltpu
from jax.experimental.pallas import tpu_sc as plsc

F32 = jnp.float32
BF16 = jnp.bfloat16

N_HEADS = 8
N_KV_HEADS = 2
HEAD_DIM = 128
GRID_W = 64
ROPE_THETA = 10000.0
CONV_KERNEL = 31
POOL_WINDOWS = (2, 4, 8, 16)
POOL_GROUPS = 4
SGU_GROUPS = 4
CHUNK = 128
N_BRANCHES = 4
TOP_K = 4
SWIGLU_LIMIT = 7.0
SWIGLU_ALPHA = 1.702
NORM_EPS = 1e-6
N_MOD = 6

ATTN_WIDTH = N_HEADS * HEAD_DIM
KV_WIDTH = N_KV_HEADS * HEAD_DIM
GROUP = N_HEADS // N_KV_HEADS

LANES = 128
SUBLANES = 8
VMEM_LIMIT_BYTES = 56 * 1024 * 1024
MOE_VMEM_LIMIT_BYTES = 60 * 1024 * 1024

ROW_TILE = 256
HALO = 16
IN_PROJ_ROWS = 1024
IN_PROJ_COLS = 1024
MOD_COLS = 1024
MOE_SUPER = 2048
MOE_SUB = 256
MOE_TAIL = 128
MOE_FF_TILE = 512
MOE_OUT_TILE = 512
GATHER_WINDOW = 128
GATHER_WORDS = 256


def _dot(a, b):
    return jnp.dot(a, b, preferred_element_type=F32)


def _dot_nt(a, b):
    return lax.dot_general(a, b, (((1,), (1,)), ((), ())), preferred_element_type=F32)


def _layer_norm(x, g, b):
    mu = jnp.mean(x, axis=-1, keepdims=True)
    xc = x - mu
    var = jnp.mean(xc * xc, axis=-1, keepdims=True)
    return xc * lax.rsqrt(var + NORM_EPS) * g + b


def _sigmoid(x):
    return 0.5 * jnp.tanh(0.5 * x) + 0.5


def _params(sem, vmem=VMEM_LIMIT_BYTES):
    return pltpu.CompilerParams(dimension_semantics=sem, vmem_limit_bytes=vmem)


_HIGH_HALF = 0xFFFF0000


def _pack_pairs(x):
    w = x.shape[1] // 2
    lo = lax.bitcast_convert_type(x[:, :w].astype(BF16).astype(F32), jnp.uint32)
    hi = lax.bitcast_convert_type(x[:, w:].astype(BF16).astype(F32), jnp.uint32)
    return (lo >> 16) | (hi & jnp.uint32(_HIGH_HALF))


def _unpack_pairs(words):
    lo = lax.bitcast_convert_type(words << 16, F32)
    hi = lax.bitcast_convert_type(words & jnp.uint32(_HIGH_HALF), F32)
    return lo, hi


def _gather_rows(planes, idx):
    n_planes, n_rows, width = planes.shape
    assert planes.dtype.itemsize == 4 and width == GATHER_WORDS
    offsets = jnp.arange(n_planes, dtype=jnp.int32) * n_rows
    flat_idx = (offsets[:, None] + idx[None, :]).reshape(-1)
    out = _gather_pieces(planes.reshape(n_planes * n_rows, width), flat_idx)
    return out.reshape(n_planes, idx.shape[0], width)


def _gather_pieces(table, idx):
    n_idx = idx.shape[0]
    width = table.shape[1]
    assert n_idx % GATHER_WINDOW == 0
    mesh = plsc.VectorSubcoreMesh(core_axis_name="core", subcore_axis_name="subcore")

    @functools.partial(
        pl.kernel,
        out_type=jax.ShapeDtypeStruct((n_idx, width), table.dtype),
        mesh=mesh,
        scratch_types=[],
    )
    def gather_kernel(table_hbm, idx_hbm, out_hbm):
        def body(idx_vmem, out_vmem):
            pltpu.sync_copy(table_hbm.at[idx_vmem.at[0]], out_vmem)

        pltpu.emit_pipeline(
            body,
            grid=(n_idx // GATHER_WINDOW,),
            in_specs=[pl.BlockSpec((1, GATHER_WINDOW), lambda i: (0, i))],
            out_specs=[pl.BlockSpec((GATHER_WINDOW, width), lambda i: (i, 0))],
            core_axis_name=("core", "subcore"),
            dimension_semantics=(pltpu.PARALLEL,),
        )(idx_hbm, out_hbm)

    return gather_kernel(table, idx.reshape(1, n_idx))


def _scatter_rows(planes, dest, n_out):
    n_planes, n_rows, width = planes.shape
    n_copies = dest.shape[0]
    assert planes.dtype.itemsize == 4 and width == GATHER_WORDS and n_rows % GATHER_WINDOW == 0
    offsets = jnp.arange(n_planes, dtype=jnp.int32) * n_out
    flat_idx = (dest[:, None, :] + offsets[None, :, None]).reshape(-1)
    out = _scatter_pieces(planes.reshape(n_planes * n_rows, width), flat_idx, n_copies, n_planes * n_out)
    return out.reshape(n_planes, n_out, width)


def _scatter_pieces(table, idx, n_copies, n_out):
    n_src, width = table.shape
    src_blocks = n_src // GATHER_WINDOW
    mesh = plsc.VectorSubcoreMesh(core_axis_name="core", subcore_axis_name="subcore")

    @functools.partial(
        pl.kernel,
        out_type=jax.ShapeDtypeStruct((n_out, width), table.dtype),
        mesh=mesh,
        scratch_types=[],
    )
    def scatter_kernel(table_hbm, idx_hbm, out_hbm):
        def body(rows_vmem, idx_vmem):
            pltpu.sync_copy(rows_vmem, out_hbm.at[idx_vmem.at[0]])

        pltpu.emit_pipeline(
            body,
            grid=(n_copies * src_blocks,),
            in_specs=[pl.BlockSpec((GATHER_WINDOW, width), lambda i: (i % src_blocks, 0)),
                      pl.BlockSpec((1, GATHER_WINDOW), lambda i: (0, i))],
            out_specs=[],
            core_axis_name=("core", "subcore"),
            dimension_semantics=(pltpu.PARALLEL,),
        )(table_hbm, idx_hbm)

    return scatter_kernel(table, idx.reshape(1, n_copies * n_src))


class _Dims:
    def __init__(self, bp, tp, bs, ts, d, past, depth, n_exp, ff):
        self.bp, self.tp, self.bs, self.ts = bp, tp, bs, ts
        self.d, self.past, self.depth, self.n_exp, self.ff = d, past, depth, n_exp, ff
        self.np_ = bp * tp
        self.ns = bs * ts
        self.n = self.np_ + self.ns
        self.cw = d // 4
        self.in_w = ATTN_WIDTH + 2 * KV_WIDTH + 5 * self.cw + N_BRANCHES * d
        self.off_k = ATTN_WIDTH
        self.off_v = ATTN_WIDTH + KV_WIDTH
        self.off_a = ATTN_WIDTH + 2 * KV_WIDTH
        self.off_gates = self.off_a + 5 * self.cw
        assert self.cw == POOL_GROUPS * LANES == SGU_GROUPS * CHUNK
        assert self.off_a % self.cw == 0 and self.off_gates % d == 0
        assert tp % ROW_TILE == 0 and ts % ROW_TILE == 0 and self.np_ % ts == 0
        assert ts % GRID_W == 0

    def group_of_tile(self, i, tm):
        npt = self.np_ // tm
        return jnp.where(i < npt, 0, 1 + (i - npt) // (self.ts // tm))

    def seq_tile_pos(self, i, tm):
        npt = self.np_ // tm
        is_p = i < npt
        pos = jnp.where(is_p, i % (self.tp // tm), (i - npt) % (self.ts // tm))
        cnt = jnp.where(is_p, self.tp // tm, self.ts // tm)
        return pos, cnt


def _mod_kernel(c_ref, w_ref, b_ref, o_ref):
    c = c_ref[...]
    s = c * _sigmoid(c)
    o_ref[...] = _dot(s, w_ref[...]) + b_ref[...]


def _modulation(cond8, w_mod, b_mod):
    depth, d, width = w_mod.shape
    tn = MOD_COLS
    return pl.pallas_call(
        _mod_kernel,
        out_shape=jax.ShapeDtypeStruct((depth, SUBLANES, width), F32),
        grid=(depth, width // tn),
        in_specs=[
            pl.BlockSpec((SUBLANES, d), lambda l, j: (0, 0)),
            pl.BlockSpec((None, d, tn), lambda l, j: (l, 0, j)),
            pl.BlockSpec((None, 1, tn), lambda l, j: (l, 0, j)),
        ],
        out_specs=pl.BlockSpec((None, SUBLANES, tn), lambda l, j: (l, 0, j)),
        compiler_params=_params(("arbitrary", "arbitrary")),
        name="modulation",
    )(cond8, w_mod, b_mod.reshape(depth, 1, width))


def _in_proj_kernel(x_ref, mod_ref, w_ref, o_ref, wbf_ref):
    @pl.when(pl.program_id(1) == 0)
    def _():
        wbf_ref[...] = w_ref[...].astype(BF16)

    shift = mod_ref[0:1, :]
    scale = mod_ref[1:2, :]
    h = (x_ref[...] * (1.0 + scale) + shift).astype(BF16)
    o_ref[...] = _dot(h, wbf_ref[...]).astype(o_ref.dtype)


def _in_proj(dm, layer, x, mod, w_in):
    tm, tn = IN_PROJ_ROWS, IN_PROJ_COLS
    d = dm.d
    return pl.pallas_call(
        _in_proj_kernel,
        out_shape=jax.ShapeDtypeStruct((dm.n, dm.in_w), BF16),
        grid=(dm.in_w // tn, dm.n // tm),
        in_specs=[
            pl.BlockSpec((tm, d), lambda j, i: (i, 0)),
            pl.BlockSpec((None, None, N_MOD, d), lambda j, i: (layer, dm.group_of_tile(i, tm), 0, 0)),
            pl.BlockSpec((None, d, tn), lambda j, i: (layer, 0, j)),
        ],
        out_specs=pl.BlockSpec((tm, tn), lambda j, i: (i, j)),
        scratch_shapes=[pltpu.VMEM((d, tn), BF16)],
        compiler_params=_params(("arbitrary", "arbitrary")),
        name="in_proj",
    )(x, mod, w_in)


def _rope(x, cos, sin_signed):
    lane = lax.broadcasted_iota(jnp.int32, x.shape, 1)
    nxt = pltpu.roll(x, HEAD_DIM - 1, 1)
    prv = pltpu.roll(x, 1, 1)
    partner = jnp.where((lane & 1) == 0, nxt, prv)
    return x * cos + partner * sin_signed


def _rms(x, g):
    return x * lax.rsqrt(jnp.mean(x * x, axis=-1, keepdims=True) + NORM_EPS) * g


def _qk_prep_kernel(q_ref, k_ref, v_ref, cos_ref, sin_ref, qg_ref, kg_ref,
                    qo_ref, ko_ref, kn_ref, vo_ref):
    cos = cos_ref[...]
    sin = sin_ref[...]
    qg = qg_ref[...]
    kg = kg_ref[...]
    scale = HEAD_DIM ** -0.5
    for h in range(N_HEADS):
        cols = slice(h * HEAD_DIM, (h + 1) * HEAD_DIM)
        qn = _rms(q_ref[:, cols].astype(F32), qg)
        qo_ref[:, cols] = (_rope(qn, cos, sin) * scale).astype(qo_ref.dtype)
    for h in range(N_KV_HEADS):
        cols = slice(h * HEAD_DIM, (h + 1) * HEAD_DIM)
        kn = _rms(k_ref[:, cols].astype(F32), kg)
        kn_ref[:, cols] = kn
        ko_ref[:, cols] = _rope(kn, cos, sin).astype(ko_ref.dtype)
    vo_ref[...] = v_ref[...].astype(F32)


def _rope_tables(dm, tm):
    t = dm.ts
    rows = t // GRID_W
    row = jnp.broadcast_to(jnp.arange(rows, dtype=F32)[:, None], (rows, GRID_W)).reshape(t)
    col = jnp.broadcast_to(jnp.arange(GRID_W, dtype=F32)[None, :], (rows, GRID_W)).reshape(t)
    half = HEAD_DIM // 2
    inv_freq = ROPE_THETA ** (-jnp.arange(0, half, 2, dtype=F32) / half)
    ang = jnp.concatenate([row[:, None] * inv_freq, col[:, None] * inv_freq], axis=-1)
    cos = jnp.repeat(jnp.cos(ang), 2, axis=-1)
    sin = jnp.repeat(jnp.sin(ang), 2, axis=-1)
    sign = jnp.tile(jnp.array([-1.0, 1.0], F32), half)
    cos = jnp.concatenate([jnp.ones((tm, HEAD_DIM), F32), cos], axis=0)
    sin = jnp.concatenate([jnp.zeros((tm, HEAD_DIM), F32), sin * sign], axis=0)
    return cos, sin


def _qk_prep(dm, layer, proj, cos, sin, q_norm, k_norm):
    tm = ROW_TILE
    npt = dm.np_ // tm
    tps = dm.ts // tm

    def tab(i):
        return (jnp.where(i < npt, 0, 1 + (i - npt) % tps), 0)

    return pl.pallas_call(
        _qk_prep_kernel,
        out_shape=(
            jax.ShapeDtypeStruct((dm.n, ATTN_WIDTH), BF16),
            jax.ShapeDtypeStruct((dm.n, KV_WIDTH), BF16),
            jax.ShapeDtypeStruct((dm.n, KV_WIDTH), F32),
            jax.ShapeDtypeStruct((dm.n, KV_WIDTH), F32),
        ),
        grid=(dm.n // tm,),
        in_specs=[
            pl.BlockSpec((tm, ATTN_WIDTH), lambda i: (i, 0)),
            pl.BlockSpec((tm, KV_WIDTH), lambda i: (i, dm.off_k // KV_WIDTH)),
            pl.BlockSpec((tm, KV_WIDTH), lambda i: (i, dm.off_v // KV_WIDTH)),
            pl.BlockSpec((tm, HEAD_DIM), tab),
            pl.BlockSpec((tm, HEAD_DIM), tab),
            pl.BlockSpec((None, 1, HEAD_DIM), lambda i: (layer, 0, 0)),
            pl.BlockSpec((None, 1, HEAD_DIM), lambda i: (layer, 0, 0)),
        ],
        out_specs=(
            pl.BlockSpec((tm, ATTN_WIDTH), lambda i: (i, 0)),
            pl.BlockSpec((tm, KV_WIDTH), lambda i: (i, 0)),
            pl.BlockSpec((tm, KV_WIDTH), lambda i: (i, 0)),
            pl.BlockSpec((tm, KV_WIDTH), lambda i: (i, 0)),
        ),
        compiler_params=_params(("arbitrary",)),
        name="qk_prep",
    )(proj, proj, proj, cos, sin, q_norm, k_norm)


def _ctx_attn_kernel(q_ref, k_ref, v_ref, o_ref):
    for h in range(N_HEADS):
        kv = h // GROUP
        cols = slice(h * HEAD_DIM, (h + 1) * HEAD_DIM)
        kcols = slice(kv * HEAD_DIM, (kv + 1) * HEAD_DIM)
        s = _dot_nt(q_ref[:, cols], k_ref[:, kcols])
        m = jnp.max(s, axis=-1, keepdims=True)
        p = jnp.exp(s - m)
        l = jnp.sum(p, axis=-1, keepdims=True)
        o = _dot(p.astype(BF16), v_ref[:, kcols]) / l
        o_ref[:, cols] = o.astype(o_ref.dtype)


def _ctx_attn(dm, q, k, proj):
    t = dm.tp
    return pl.pallas_call(
        _ctx_attn_kernel,
        out_shape=jax.ShapeDtypeStruct((dm.np_, ATTN_WIDTH), BF16),
        grid=(dm.bp,),
        in_specs=[
            pl.BlockSpec((t, ATTN_WIDTH), lambda b: (b, 0)),
            pl.BlockSpec((t, KV_WIDTH), lambda b: (b, 0)),
            pl.BlockSpec((t, KV_WIDTH), lambda b: (b, dm.off_v // KV_WIDTH)),
        ],
        out_specs=pl.BlockSpec((t, ATTN_WIDTH), lambda b: (b, 0)),
        compiler_params=_params(("arbitrary",)),
        name="ctx_attn",
    )(q, k, proj)


def _lat_attn_kernel(q_ref, k_ref, v_ref, ck_ref, cv_ref, o_ref):
    ck = ck_ref[...].astype(BF16)
    cv = cv_ref[...].astype(BF16)
    for h in range(N_HEADS):
        kv = h // GROUP
        cols = slice(h * HEAD_DIM, (h + 1) * HEAD_DIM)
        kcols = slice(kv * HEAD_DIM, (kv + 1) * HEAD_DIM)
        qh = q_ref[:, cols]
        s_ctx = _dot_nt(qh, ck[:, kcols])
        s_lat = _dot_nt(qh, k_ref[:, kcols])
        m = jnp.maximum(jnp.max(s_ctx, axis=-1, keepdims=True),
                        jnp.max(s_lat, axis=-1, keepdims=True))
        p_ctx = jnp.exp(s_ctx - m)
        p_lat = jnp.exp(s_lat - m)
        l = jnp.sum(p_ctx, axis=-1, keepdims=True) + jnp.sum(p_lat, axis=-1, keepdims=True)
        o = _dot(p_ctx.astype(BF16), cv[:, kcols]) + _dot(p_lat.astype(BF16), v_ref[:, kcols])
        o_ref[:, cols] = (o / l).astype(o_ref.dtype)


def _lat_attn(dm, layer, q, k, proj, cache_k, cache_v):
    tq = ROW_TILE
    ts = dm.ts
    qb = ts // tq
    return pl.pallas_call(
        _lat_attn_kernel,
        out_shape=jax.ShapeDtypeStruct((dm.ns, ATTN_WIDTH), BF16),
        grid=(dm.bs, qb),
        in_specs=[
            pl.BlockSpec((tq, ATTN_WIDTH), lambda b, i: (dm.np_ // tq + b * qb + i, 0)),
            pl.BlockSpec((ts, KV_WIDTH), lambda b, i: (dm.np_ // ts + b, 0)),
            pl.BlockSpec((ts, KV_WIDTH), lambda b, i: (dm.np_ // ts + b, dm.off_v // KV_WIDTH)),
            pl.BlockSpec((None, None, dm.past, KV_WIDTH), lambda b, i: (b, layer, 0, 0)),
            pl.BlockSpec((None, None, dm.past, KV_WIDTH), lambda b, i: (b, layer, 0, 0)),
        ],
        out_specs=pl.BlockSpec((tq, ATTN_WIDTH), lambda b, i: (b * qb + i, 0)),
        compiler_params=_params(("arbitrary", "arbitrary")),
        name="lat_attn",
    )(q, k, proj, cache_k, cache_v)


def _local_kernel(dm, a_ref, ap_ref, an_ref, g_ref, gp_ref, gn_ref, z_ref, zp_ref, zn_ref,
                  u_ref, v_ref, cw_ref, cb_ref, clg_ref, clb_ref, slg_ref, slb_ref,
                  sw_ref, sbt_ref, conv_o, pool_o, sgu_o, ubuf, zbuf):
    t = ROW_TILE
    i = pl.program_id(0)
    pos, cnt = dm.seq_tile_pos(i, t)
    has_prev = (pos > 0).astype(F32)
    has_next = (pos < cnt - 1).astype(F32)

    def glu(a, g):
        return a[...].astype(F32) * _sigmoid(g[...].astype(F32))

    ubuf[0:HALO, :] = glu(ap_ref, gp_ref) * has_prev
    ubuf[HALO:HALO + t, :] = glu(a_ref, g_ref)
    ubuf[HALO + t:, :] = glu(an_ref, gn_ref) * has_next
    zbuf[0:HALO, :] = zp_ref[...].astype(F32) * has_prev
    zbuf[HALO:HALO + t, :] = z_ref[...].astype(F32)
    zbuf[HALO + t:, :] = zn_ref[...].astype(F32) * has_next

    half_rows = t // 2
    centre = CONV_KERNEL // 2
    for r in range(2):
        pieces = []
        for c in range(dm.cw // LANES):
            cols = slice(c * LANES, (c + 1) * LANES)
            acc = jnp.zeros((half_rows, LANES), F32)
            for k in range(CONV_KERNEL):
                start = HALO + r * half_rows + k - centre
                acc = acc + cw_ref[k:k + 1, cols] * ubuf[start:start + half_rows, cols]
            pieces.append(acc)
        y = jnp.concatenate(pieces, axis=1) + cb_ref[...]
        y = _layer_norm(y, clg_ref[...], clb_ref[...])
        conv_o[r * half_rows:(r + 1) * half_rows, :] = (y * _sigmoid(y)).astype(conv_o.dtype)

    seq_len = jnp.where(i < dm.np_ // t, dm.tp, dm.ts)
    tok = pos * t + lax.broadcasted_iota(jnp.int32, (t, LANES), 0)
    for gi, w in enumerate(POOL_WINDOWS):
        cols = slice(gi * LANES, (gi + 1) * LANES)
        acc = jnp.zeros((t, LANES), F32)
        for j in range(-(w // 2), w - w // 2):
            acc = acc + zbuf[HALO + j:HALO + j + t, cols]
        lo = jnp.maximum(tok - w // 2, 0)
        hi = jnp.minimum(tok - w // 2 + w, seq_len)
        mean = acc / (hi - lo).astype(F32)
        pool_o[:, cols] = (mean - zbuf[HALO:HALO + t, cols]).astype(pool_o.dtype)

    vn = _layer_norm(v_ref[...].astype(F32), slg_ref[...], slb_ref[...]).astype(BF16)
    for gi in range(SGU_GROUPS):
        cols = slice(gi * CHUNK, (gi + 1) * CHUNK)
        w = sw_ref[gi].astype(BF16)
        bias = sbt_ref[:, gi:gi + 1]
        for c in range(t // CHUNK):
            rows = slice(c * CHUNK, (c + 1) * CHUNK)
            mixed = _dot(w, vn[rows, cols]) + bias
            sgu_o[rows, cols] = (u_ref[rows, cols].astype(F32) * mixed).astype(sgu_o.dtype)


def _local_branches(dm, layer, proj, conv_w, conv_b, conv_ln_g, conv_ln_b,
                    sgu_ln_g, sgu_ln_b, sgu_w, sgu_bt):
    t = ROW_TILE
    cw = dm.cw
    hb = t // HALO
    last = dm.n // HALO - 1
    ca = dm.off_a // cw

    def cur(c):
        return pl.BlockSpec((t, cw), lambda i: (i, c))

    def prev(c):
        return pl.BlockSpec((HALO, cw), lambda i: (jnp.maximum(i * hb - 1, 0), c))

    def nxt(c):
        return pl.BlockSpec((HALO, cw), lambda i: (jnp.minimum((i + 1) * hb, last), c))

    def vec():
        return pl.BlockSpec((None, 1, cw), lambda i: (layer, 0, 0))

    out = jax.ShapeDtypeStruct((dm.n, cw), BF16)
    return pl.pallas_call(
        functools.partial(_local_kernel, dm),
        out_shape=(out, out, out),
        grid=(dm.n // t,),
        in_specs=[
            cur(ca), prev(ca), nxt(ca),
            cur(ca + 1), prev(ca + 1), nxt(ca + 1),
            cur(ca + 2), prev(ca + 2), nxt(ca + 2),
            cur(ca + 3), cur(ca + 4),
            pl.BlockSpec((None, CONV_KERNEL, cw), lambda i: (layer, 0, 0)),
            vec(), vec(), vec(), vec(), vec(),
            pl.BlockSpec((None, SGU_GROUPS, CHUNK, CHUNK), lambda i: (layer, 0, 0, 0)),
            pl.BlockSpec((None, CHUNK, SGU_GROUPS), lambda i: (layer, 0, 0)),
        ],
        out_specs=(pl.BlockSpec((t, cw), lambda i: (i, 0)),) * 3,
        scratch_shapes=[pltpu.VMEM((t + 2 * HALO, cw), F32), pltpu.VMEM((t + 2 * HALO, cw), F32)],
        compiler_params=_params(("arbitrary",)),
        name="local_branches",
    )(proj, proj, proj, proj, proj, proj, proj, proj, proj, proj, proj,
      conv_w, conv_b, conv_ln_g, conv_ln_b, sgu_ln_g, sgu_ln_b, sgu_w, sgu_bt)


def _merge_kernel(dm, o_ref, cv_ref, pl_ref, sg_ref, g0_ref, g1_ref, g2_ref, g3_ref,
                  wa_ref, wc_ref, wp_ref, ws_ref, ps_ref, m_ref):
    cw = dm.cw
    for c in range(dm.d // cw):
        cols = slice(c * cw, (c + 1) * cw)
        ya = _dot(o_ref[...], wa_ref[:, cols])
        yb = _dot(cv_ref[...], wc_ref[:, cols])
        yc = _dot(pl_ref[:, c * LANES:(c + 1) * LANES], wp_ref[c]) * ps_ref[:, cols]
        yd = _dot(sg_ref[...], ws_ref[:, cols])
        merged = (_sigmoid(g0_ref[:, cols].astype(F32)) * ya
                  + _sigmoid(g1_ref[:, cols].astype(F32)) * yb
                  + _sigmoid(g2_ref[:, cols].astype(F32)) * yc
                  + _sigmoid(g3_ref[:, cols].astype(F32)) * yd)
        m_ref[:, cols] = merged.astype(m_ref.dtype)


def _merge(dm, layer, attn_o, conv_h, pool_h, sgu_h, proj, wa, wc, wp, ws, pool_scale):
    tm = ROW_TILE
    d, cw = dm.d, dm.cw
    g0 = dm.off_gates // d
    assert d // cw == POOL_GROUPS and wp.shape[-1] == cw

    def gate(b):
        return pl.BlockSpec((tm, d), lambda i: (i, g0 + b))

    return pl.pallas_call(
        functools.partial(_merge_kernel, dm),
        out_shape=jax.ShapeDtypeStruct((dm.n, d), BF16),
        grid=(dm.n // tm,),
        in_specs=[
            pl.BlockSpec((tm, ATTN_WIDTH), lambda i: (i, 0)),
            pl.BlockSpec((tm, cw), lambda i: (i, 0)),
            pl.BlockSpec((tm, cw), lambda i: (i, 0)),
            pl.BlockSpec((tm, cw), lambda i: (i, 0)),
            gate(0), gate(1), gate(2), gate(3),
            pl.BlockSpec((None, ATTN_WIDTH, d), lambda i: (layer, 0, 0)),
            pl.BlockSpec((None, cw, d), lambda i: (layer, 0, 0)),
            pl.BlockSpec((None, POOL_GROUPS, LANES, cw), lambda i: (layer, 0, 0, 0)),
            pl.BlockSpec((None, cw, d), lambda i: (layer, 0, 0)),
            pl.BlockSpec((None, 1, d), lambda i: (layer, 0, 0)),
        ],
        out_specs=pl.BlockSpec((tm, d), lambda i: (i, 0)),
        compiler_params=_params(("arbitrary",)),
        name="merge",
    )(attn_o, conv_h, pool_h, sgu_h, proj, proj, proj, proj, wa, wc, wp, ws, pool_scale)


def _out_proj_kernel(alpha, m_ref, x_ref, mod_ref, w_ref, lg_ref, lb_ref, wr_ref, br_ref,
                     x1_ref, h2_ref, idx_ref, comb_ref):
    gate1 = mod_ref[2:3, :]
    shift2 = mod_ref[3:4, :]
    scale2 = mod_ref[4:5, :]
    wr = wr_ref[...]
    wr_hi = wr.astype(BF16)
    wr_lo = (wr - wr_hi.astype(F32)).astype(BF16)

    half_rows = m_ref.shape[0] // 2
    for r in range(2):
        rows = slice(r * half_rows, (r + 1) * half_rows)
        y = _dot(m_ref[rows, :], w_ref[...])
        x1 = _layer_norm(alpha * x_ref[rows, :] + gate1 * y, lg_ref[...], lb_ref[...])
        x1_ref[rows, :] = x1
        h2 = x1 * (1.0 + scale2) + shift2
        h2_hi = h2.astype(BF16)
        packed = _pack_pairs(h2)
        for p in range(h2_ref.shape[0]):
            h2_ref[p, rows, :] = packed[:, p * GATHER_WORDS:(p + 1) * GATHER_WORDS]

        h2_lo = (h2 - h2_hi.astype(F32)).astype(BF16)
        logits = _dot(h2_hi, wr_hi) + _dot(h2_hi, wr_lo) + _dot(h2_lo, wr_hi) + br_ref[...]

        lane = lax.broadcasted_iota(jnp.int32, logits.shape, 1)
        idx_out = jnp.zeros(logits.shape, jnp.int32)
        val_out = jnp.zeros(logits.shape, F32)
        top = None
        den = jnp.zeros((logits.shape[0], 1), F32)
        for k in range(TOP_K):
            m = jnp.max(logits, axis=-1, keepdims=True)
            ik = jnp.min(jnp.where(logits == m, lane, LANES), axis=-1, keepdims=True)
            if top is None:
                top = m
            e = jnp.exp(m - top)
            den = den + e
            idx_out = jnp.where(lane == k, ik, idx_out)
            val_out = jnp.where(lane == k, e, val_out)
            logits = jnp.where(lane == ik, -jnp.inf, logits)
        idx_ref[rows, :] = idx_out
        comb_ref[rows, :] = val_out / den


def _out_proj(dm, layer, merged, x, mod, w_out, ln_g, ln_b, w_router, b_router):
    tm = 2 * ROW_TILE
    d = dm.d
    alpha = (2 * dm.depth) ** 0.25

    def vec():
        return pl.BlockSpec((None, 1, d), lambda i: (layer, 0, 0))

    return pl.pallas_call(
        functools.partial(_out_proj_kernel, alpha),
        out_shape=(
            jax.ShapeDtypeStruct((dm.n, d), F32),
            jax.ShapeDtypeStruct((d // 2 // GATHER_WORDS, dm.n, GATHER_WORDS), jnp.uint32),
            jax.ShapeDtypeStruct((dm.n, LANES), jnp.int32),
            jax.ShapeDtypeStruct((dm.n, LANES), F32),
        ),
        grid=(dm.n // tm,),
        in_specs=[
            pl.BlockSpec((tm, d), lambda i: (i, 0)),
            pl.BlockSpec((tm, d), lambda i: (i, 0)),
            pl.BlockSpec((None, None, N_MOD, d), lambda i: (layer, dm.group_of_tile(i, tm), 0, 0)),
            pl.BlockSpec((None, d, d), lambda i: (layer, 0, 0)),
            vec(), vec(),
            pl.BlockSpec((None, d, LANES), lambda i: (layer, 0, 0)),
            pl.BlockSpec((None, 1, LANES), lambda i: (layer, 0, 0)),
        ],
        out_specs=(
            pl.BlockSpec((tm, d), lambda i: (i, 0)),
            pl.BlockSpec((d // 2 // GATHER_WORDS, tm, GATHER_WORDS), lambda i: (0, i, 0)),
            pl.BlockSpec((tm, LANES), lambda i: (i, 0)),
            pl.BlockSpec((tm, LANES), lambda i: (i, 0)),
        ),
        compiler_params=_params(("arbitrary",)),
        name="out_proj",
    )(merged, x, mod, w_out, ln_g, ln_b, w_router, b_router)


def _moe_live_chunks(nv, fn):
    pair = 2 * MOE_SUB
    n_pairs = nv // pair

    def full_pair(i, carry):
        fn(pl.multiple_of(i * pair, pair), pair, False)
        return carry

    lax.fori_loop(0, n_pairs, full_pair, 0)

    base = pl.multiple_of(n_pairs * pair, pair)
    n = nv - base

    def sub_and_tail():
        fn(base, MOE_SUB, False)
        fn(base + MOE_SUB, MOE_TAIL, True)

    pl.when(n > MOE_SUB + MOE_TAIL)(functools.partial(fn, base, pair, True))
    pl.when(jnp.logical_and(n > MOE_SUB, n <= MOE_SUB + MOE_TAIL))(sub_and_tail)
    pl.when(jnp.logical_and(n > MOE_TAIL, n <= MOE_SUB))(functools.partial(fn, base, MOE_SUB, True))
    pl.when(jnp.logical_and(n > 0, n <= MOE_TAIL))(functools.partial(fn, base, MOE_TAIL, True))


def _moe_kernel(nfa, be_ref, nv_ref, na_ref, x_ref, wg_ref, bg_ref, wu_ref, bu_ref, wd_ref, bd_ref,
                o_ref, act_ref):
    s = pl.program_id(0)
    j = pl.program_id(1)
    nv = nv_ref[s]
    tf = MOE_FF_TILE
    active = s < na_ref[0]

    @pl.when(jnp.logical_and(active, j < nfa))
    def _():
        bg = bg_ref[...]
        bu = bu_ref[...]
        col0 = pl.multiple_of(j * tf, tf)

        def chunk(base, m, has_dead_rows):
            halves = [_unpack_pairs(x_ref[p, pl.ds(base, m), :]) for p in range(x_ref.shape[0])]
            if has_dead_rows:
                live = base + lax.broadcasted_iota(jnp.int32, (m, GATHER_WORDS), 0) < nv
                halves = [(jnp.where(live, lo, 0.0), jnp.where(live, hi, 0.0)) for lo, hi in halves]
            xs = jnp.concatenate([lo for lo, _ in halves] + [hi for _, hi in halves], axis=1)
            gt = jnp.minimum(_dot(xs, wg_ref[...]) + bg, SWIGLU_LIMIT)
            up = jnp.clip(_dot(xs, wu_ref[...]) + bu, -SWIGLU_LIMIT, SWIGLU_LIMIT)
            act = gt * _sigmoid(SWIGLU_ALPHA * gt) * (up + 1.0)
            act_ref[pl.ds(base, m), pl.ds(col0, tf)] = act.astype(BF16)

        _moe_live_chunks(nv, chunk)

    @pl.when(j >= nfa)
    def _():
        bd = bd_ref[...]

        def chunk(base, m, has_dead_rows):
            del has_dead_rows
            y = _dot(act_ref[pl.ds(base, m), :].astype(F32), wd_ref[...]) + bd
            o_ref[pl.ds(base, m), :] = _pack_pairs(y)

        _moe_live_chunks(nv, chunk)

        def zero_fill(c, carry):
            rows = pl.ds(pl.multiple_of(c * MOE_TAIL, MOE_TAIL), MOE_TAIL)
            o_ref[rows, :] = jnp.zeros((MOE_TAIL, o_ref.shape[1]), o_ref.dtype)
            return carry

        lax.fori_loop((nv + MOE_TAIL - 1) // MOE_TAIL, MOE_SUPER // MOE_TAIL, zero_fill, 0)


def _moe(dm, layer, x_sorted, blk_e, blk_nv, n_act, w_gate, b_gate, w_up, b_up, w_down, b_down):
    d, ff, tf, tn, r = dm.d, dm.ff, MOE_FF_TILE, MOE_OUT_TILE, MOE_SUPER
    n_planes = x_sorted.shape[0]
    nsb = x_sorted.shape[1] // r
    nfa = ff // tf
    nfb = d // tn
    assert tn // 2 == GATHER_WORDS

    def blk(s, na):
        return jnp.minimum(s, na[0] - 1)

    def col_a(s, j, na):
        return jnp.where(s < na[0], jnp.minimum(j, nfa - 1), nfa - 1)

    def col_b(s, j, na):
        return jnp.where(s < na[0], jnp.maximum(j - nfa, 0), nfb - 1)

    grid_spec = pltpu.PrefetchScalarGridSpec(
        num_scalar_prefetch=3,
        grid=(nsb, nfa + nfb),
        in_specs=[
            pl.BlockSpec((n_planes, r, GATHER_WORDS), lambda s, j, be, nv, na: (0, blk(s, na), 0)),
            pl.BlockSpec((None, None, d, tf), lambda s, j, be, nv, na: (layer, be[s], 0, col_a(s, j, na))),
            pl.BlockSpec((None, None, 1, tf), lambda s, j, be, nv, na: (layer, be[s], 0, col_a(s, j, na))),
            pl.BlockSpec((None, None, d, tf), lambda s, j, be, nv, na: (layer, be[s], 0, col_a(s, j, na))),
            pl.BlockSpec((None, None, 1, tf), lambda s, j, be, nv, na: (layer, be[s], 0, col_a(s, j, na))),
            pl.BlockSpec((None, None, ff, tn), lambda s, j, be, nv, na: (layer, be[s], 0, col_b(s, j, na))),
            pl.BlockSpec((None, None, 1, tn), lambda s, j, be, nv, na: (layer, be[s], 0, col_b(s, j, na))),
        ],
        out_specs=pl.BlockSpec((None, r, tn // 2), lambda s, j, be, nv, na: (jnp.maximum(j - nfa, 0), s, 0)),
        scratch_shapes=[
            pltpu.VMEM((r, ff), BF16),
        ],
    )
    return pl.pallas_call(
        functools.partial(_moe_kernel, nfa),
        out_shape=jax.ShapeDtypeStruct((nfb, nsb * r, tn // 2), jnp.uint32),
        grid_spec=grid_spec,
        compiler_params=_params(("arbitrary", "arbitrary"), MOE_VMEM_LIMIT_BYTES),
        name="moe_ffn",
    )(blk_e, blk_nv, n_act, x_sorted, w_gate, b_gate, w_up, b_up, w_down, b_down)


def _route(dm, top_idx, n_super):
    e, r = dm.n_exp, MOE_SUPER
    flat_e = top_idx.reshape(-1)
    onehot = (flat_e[:, None] == jnp.arange(e, dtype=jnp.int32)[None, :]).astype(jnp.int32)
    rank = jnp.take_along_axis(jnp.cumsum(onehot, axis=0), flat_e[:, None], axis=1)[:, 0] - 1
    counts = jnp.sum(onehot, axis=0)
    nsb_e = (counts + r - 1) // r
    sb_end = jnp.cumsum(nsb_e)
    sb_start = sb_end - nsb_e
    dest = sb_start[flat_e] * r + rank
    n_act = sb_end[-1]
    sidx = jnp.arange(n_super, dtype=jnp.int32)
    blk_e = jnp.clip(jnp.searchsorted(sb_end, jnp.minimum(sidx, n_act - 1), side='right'), 0, e - 1)
    blk_e = blk_e.astype(jnp.int32)
    blk_nv = jnp.clip(counts[blk_e] - (sidx - sb_start[blk_e]) * r, 0, r)
    blk_nv = jnp.where(sidx < n_act, blk_nv, 0).astype(jnp.int32)
    return dest.astype(jnp.int32), blk_e, blk_nv, n_act.astype(jnp.int32).reshape(1)


def _combine_kernel(alpha, y_ref, comb_ref, x_ref, mod_ref, lg_ref, lb_ref, o_ref):
    gate2 = mod_ref[5:6, :]
    comb = comb_ref[...]
    slabs = []
    for b in range(y_ref.shape[0]):
        lo_sum = None
        hi_sum = None
        for k in range(TOP_K):
            lo, hi = _unpack_pairs(y_ref[b, k])
            wk = comb[:, k:k + 1]
            lo_sum = wk * lo if lo_sum is None else lo_sum + wk * lo
            hi_sum = wk * hi if hi_sum is None else hi_sum + wk * hi
        slabs += [lo_sum, hi_sum]
    ff = jnp.concatenate(slabs, axis=1)
    o_ref[...] = _layer_norm(alpha * x_ref[...] + gate2 * ff, lg_ref[...], lb_ref[...])


def _combine(dm, layer, y_tok, comb, x1, mod, ln_g, ln_b):
    tm = ROW_TILE
    d = dm.d
    alpha = (2 * dm.depth) ** 0.25
    return pl.pallas_call(
        functools.partial(_combine_kernel, alpha),
        out_shape=jax.ShapeDtypeStruct((dm.n, d), F32),
        grid=(dm.n // tm,),
        in_specs=[
            pl.BlockSpec((y_tok.shape[0], TOP_K, tm, y_tok.shape[3]), lambda i: (0, 0, i, 0)),
            pl.BlockSpec((tm, LANES), lambda i: (i, 0)),
            pl.BlockSpec((tm, d), lambda i: (i, 0)),
            pl.BlockSpec((None, None, N_MOD, d), lambda i: (layer, dm.group_of_tile(i, tm), 0, 0)),
            pl.BlockSpec((None, 1, d), lambda i: (layer, 0, 0)),
            pl.BlockSpec((None, 1, d), lambda i: (layer, 0, 0)),
        ],
        out_specs=pl.BlockSpec((tm, d), lambda i: (i, 0)),
        compiler_params=_params(("arbitrary",)),
        name="combine",
    )(y_tok, comb, x1, mod, ln_g, ln_b)


def kernel(x_prompt, x_sample, cache_k, cache_v, c, c_ctx, w_mod, b_mod, w_in, q_norm, k_norm,
           w_attn_o, conv_w, conv_b, conv_ln_g, conv_ln_b, w_conv_o, w_pool, pool_scale,
           sgu_ln_g, sgu_ln_b, sgu_w, sgu_b, w_sgu_o, w_out, ln1_g, ln1_b, ln2_g, ln2_b,
           w_router, b_router, w_gate, b_gate, w_up, b_up, w_down, b_down):
    bp, tp, d = x_prompt.shape
    bs, ts, _ = x_sample.shape
    depth, n_exp, _, ff = w_gate.shape
    past = cache_k.shape[2]
    dm = _Dims(bp, tp, bs, ts, d, past, depth, n_exp, ff)
    assert 1 + bs <= SUBLANES and n_exp <= LANES

    x = jnp.concatenate([x_prompt.reshape(dm.np_, d), x_sample.reshape(dm.ns, d)], axis=0)
    cond = jnp.concatenate([c_ctx[None, :], c, jnp.zeros((SUBLANES - 1 - bs, d), F32)], axis=0)
    mod = _modulation(cond, w_mod, b_mod).reshape(depth, SUBLANES, N_MOD, d)

    cos, sin = _rope_tables(dm, ROW_TILE)
    ck = cache_k.reshape(bs, depth, past, KV_WIDTH)
    cv = cache_v.reshape(bs, depth, past, KV_WIDTH)

    def row(p):
        return p.reshape(depth, 1, p.shape[-1])

    wa, wc, wp, ws, wo = (w.astype(BF16) for w in (w_attn_o, w_conv_o, w_pool, w_sgu_o, w_out))
    sgu_bt = jnp.swapaxes(sgu_b, 1, 2)
    w_router_p = jnp.pad(w_router, ((0, 0), (0, 0), (0, LANES - n_exp)))
    b_router_p = jnp.pad(b_router, ((0, 0), (0, LANES - n_exp)), constant_values=-jnp.inf)
    b_router_p = b_router_p.reshape(depth, 1, LANES)
    bg, bu = b_gate.reshape(depth, n_exp, 1, ff), b_up.reshape(depth, n_exp, 1, ff)
    bd = b_down.reshape(depth, n_exp, 1, d)

    nk = dm.n * TOP_K
    n_super = nk // MOE_SUPER + n_exp
    ks, vs = [], []
    for l in range(depth):
        proj = _in_proj(dm, l, x, mod, w_in)
        q, k, kn, v32 = _qk_prep(dm, l, proj, cos, sin, row(q_norm), row(k_norm))
        ks.append(kn[:dm.np_])
        vs.append(v32[:dm.np_])
        attn_o = jnp.concatenate(
            [_ctx_attn(dm, q, k, proj), _lat_attn(dm, l, q, k, proj, ck, cv)], axis=0)
        conv_h, pool_h, sgu_h = _local_branches(
            dm, l, proj, conv_w, row(conv_b), row(conv_ln_g), row(conv_ln_b),
            row(sgu_ln_g), row(sgu_ln_b), sgu_w, sgu_bt)
        merged = _merge(dm, l, attn_o, conv_h, pool_h, sgu_h, proj, wa, wc, wp, ws, row(pool_scale))
        x1, h2, top_idx, comb = _out_proj(dm, l, merged, x, mod, wo, row(ln1_g), row(ln1_b),
                                          w_router_p, b_router_p)
        dest, blk_e, blk_nv, n_act = _route(dm, top_idx[:, :TOP_K], n_super)
        dest_km = dest.reshape(dm.n, TOP_K).T
        x_sorted = _scatter_rows(h2, dest_km, n_super * MOE_SUPER)
        y_sorted = _moe(dm, l, x_sorted, blk_e, blk_nv, n_act, w_gate, bg, w_up, bu, w_down, bd)
        y_tok = _gather_rows(y_sorted, dest_km.reshape(-1))
        y_tok = y_tok.reshape(y_sorted.shape[0], TOP_K, dm.n, GATHER_WORDS)
        x = _combine(dm, l, y_tok, comb, x1, mod, row(ln2_g), row(ln2_b))

    y_prompt = x[:dm.np_].reshape(bp, tp, d)
    y_sample = x[dm.np_:].reshape(bs, ts, d)
    state_k = jnp.stack(ks, axis=1).reshape(bp, tp, depth, N_KV_HEADS, HEAD_DIM).swapaxes(1, 2)
    state_v = jnp.stack(vs, axis=1).reshape(bp, tp, depth, N_KV_HEADS, HEAD_DIM).swapaxes(1, 2)
    return (y_prompt, y_sample, state_k, state_v)
```

```python
import functools

import jax
import jax.numpy as jnp
from jax import lax
from jax.experimental import pallas as pl
from jax.experimental.pallas import tpu as pltpu
from jax.experimental.pallas import tpu_sc as plsc

F32 = jnp.float32
BF16 = jnp.bfloat16

N_HEADS = 8
N_KV_HEADS = 2
HEAD_DIM = 128
GRID_W = 64
ROPE_THETA = 10000.0
CONV_KERNEL = 31
POOL_WINDOWS = (2, 4, 8, 16)
POOL_GROUPS = 4
SGU_GROUPS = 4
CHUNK = 128
N_BRANCHES = 4
TOP_K = 4
SWIGLU_LIMIT = 7.0
SWIGLU_ALPHA = 1.702
NORM_EPS = 1e-6
N_MOD = 6

ATTN_WIDTH = N_HEADS * HEAD_DIM
KV_WIDTH = N_KV_HEADS * HEAD_DIM
GROUP = N_HEADS // N_KV_HEADS

LANES = 128
SUBLANES = 8
VMEM_LIMIT_BYTES = 56 * 1024 * 1024
MOE_VMEM_LIMIT_BYTES = 60 * 1024 * 1024

ROW_TILE = 256
WIDE_ROW_TILE = 512
HALO = 16
IN_PROJ_ROWS = 1024
IN_PROJ_COLS = 1024
MOD_COLS = 1024
MOE_SUPER = 2048
MOE_SUB = 256
MOE_TAIL = 128
MOE_FF_TILE = 512
MOE_OUT_TILE = 512
GATHER_WINDOW = 128
GATHER_WORDS = 256


def _dot(a, b):
    return jnp.dot(a, b, preferred_element_type=F32)


def _dot_nt(a, b):
    return lax.dot_general(a, b, (((1,), (1,)), ((), ())), preferred_element_type=F32)


def _layer_norm(x, g, b):
    mu = jnp.mean(x, axis=-1, keepdims=True)
    xc = x - mu
    var = jnp.mean(xc * xc, axis=-1, keepdims=True)
    return xc * lax.rsqrt(var + NORM_EPS) * g + b


def _sigmoid(x):
    return 0.5 * jnp.tanh(0.5 * x) + 0.5


def _params(sem, vmem=VMEM_LIMIT_BYTES):
    return pltpu.CompilerParams(dimension_semantics=sem, vmem_limit_bytes=vmem)


_HIGH_HALF = 0xFFFF0000


def _pack_pairs(x):
    w = x.shape[1] // 2
    lo = lax.bitcast_convert_type(x[:, :w].astype(BF16).astype(F32), jnp.uint32)
    hi = lax.bitcast_convert_type(x[:, w:].astype(BF16).astype(F32), jnp.uint32)
    return (lo >> 16) | (hi & jnp.uint32(_HIGH_HALF))


def _unpack_pairs(words):
    lo = lax.bitcast_convert_type(words << 16, F32)
    hi = lax.bitcast_convert_type(words & jnp.uint32(_HIGH_HALF), F32)
    return lo, hi


def _gather_rows(planes, idx):
    n_planes, n_rows, width = planes.shape
    assert planes.dtype.itemsize == 4 and width == GATHER_WORDS
    offsets = jnp.arange(n_planes, dtype=jnp.int32) * n_rows
    flat_idx = (offsets[:, None] + idx[None, :]).reshape(-1)
    out = _gather_pieces(planes.reshape(n_planes * n_rows, width), flat_idx)
    return out.reshape(n_planes, idx.shape[0], width)


def _gather_pieces(table, idx):
    n_idx = idx.shape[0]
    width = table.shape[1]
    assert n_idx % GATHER_WINDOW == 0
    mesh = plsc.VectorSubcoreMesh(core_axis_name="core", subcore_axis_name="subcore")

    @functools.partial(
        pl.kernel,
        out_type=jax.ShapeDtypeStruct((n_idx, width), table.dtype),
        mesh=mesh,
        scratch_types=[],
    )
    def gather_kernel(table_hbm, idx_hbm, out_hbm):
        def body(idx_vmem, out_vmem):
            pltpu.sync_copy(table_hbm.at[idx_vmem.at[0]], out_vmem)

        pltpu.emit_pipeline(
            body,
            grid=(n_idx // GATHER_WINDOW,),
            in_specs=[pl.BlockSpec((1, GATHER_WINDOW), lambda i: (0, i))],
            out_specs=[pl.BlockSpec((GATHER_WINDOW, width), lambda i: (i, 0))],
            core_axis_name=("core", "subcore"),
            dimension_semantics=(pltpu.PARALLEL,),
        )(idx_hbm, out_hbm)

    return gather_kernel(table, idx.reshape(1, n_idx))


def _scatter_rows(planes, dest, n_out):
    n_planes, n_rows, width = planes.shape
    n_copies = dest.shape[0]
    assert planes.dtype.itemsize == 4 and width == GATHER_WORDS and n_rows % GATHER_WINDOW == 0
    offsets = jnp.arange(n_planes, dtype=jnp.int32) * n_out
    flat_idx = (dest[:, None, :] + offsets[None, :, None]).reshape(-1)
    out = _scatter_pieces(planes.reshape(n_planes * n_rows, width), flat_idx, n_copies, n_planes * n_out)
    return out.reshape(n_planes, n_out, width)


def _scatter_pieces(table, idx, n_copies, n_out):
    n_src, width = table.shape
    src_blocks = n_src // GATHER_WINDOW
    mesh = plsc.VectorSubcoreMesh(core_axis_name="core", subcore_axis_name="subcore")

    @functools.partial(
        pl.kernel,
        out_type=jax.ShapeDtypeStruct((n_out, width), table.dtype),
        mesh=mesh,
        scratch_types=[],
    )
    def scatter_kernel(table_hbm, idx_hbm, out_hbm):
        def body(rows_vmem, idx_vmem):
            pltpu.sync_copy(rows_vmem, out_hbm.at[idx_vmem.at[0]])

        pltpu.emit_pipeline(
            body,
            grid=(n_copies * src_blocks,),
            in_specs=[pl.BlockSpec((GATHER_WINDOW, width), lambda i: (i % src_blocks, 0)),
                      pl.BlockSpec((1, GATHER_WINDOW), lambda i: (0, i))],
            out_specs=[],
            core_axis_name=("core", "subcore"),
            dimension_semantics=(pltpu.PARALLEL,),
        )(table_hbm, idx_hbm)

    return scatter_kernel(table, idx.reshape(1, n_copies * n_src))


class _Dims:
    def __init__(self, bp, tp, bs, ts, d, past, depth, n_exp, ff):
        self.bp, self.tp, self.bs, self.ts = bp, tp, bs, ts
        self.d, self.past, self.depth, self.n_exp, self.ff = d, past, depth, n_exp, ff
        self.np_ = bp * tp
        self.ns = bs * ts
        self.n = self.np_ + self.ns
        self.cw = d // 4
        self.in_w = ATTN_WIDTH + 2 * KV_WIDTH + 5 * self.cw + N_BRANCHES * d
        self.off_k = ATTN_WIDTH
        self.off_v = ATTN_WIDTH + KV_WIDTH
        self.off_a = ATTN_WIDTH + 2 * KV_WIDTH
        self.off_gates = self.off_a + 5 * self.cw
        assert self.cw == POOL_GROUPS * LANES == SGU_GROUPS * CHUNK
        assert self.off_a % self.cw == 0 and self.off_gates % d == 0
        assert tp % ROW_TILE == 0 and ts % ROW_TILE == 0 and self.np_ % ts == 0
        assert ts % GRID_W == 0
        assert IN_PROJ_ROWS % WIDE_ROW_TILE == 0 and self.np_ % IN_PROJ_ROWS == 0 and ts % IN_PROJ_ROWS == 0

    def group_of_tile(self, i, tm):
        npt = self.np_ // tm
        return jnp.where(i < npt, 0, 1 + (i - npt) // (self.ts // tm))

    def seq_tile_pos(self, i, tm):
        npt = self.np_ // tm
        is_p = i < npt
        pos = jnp.where(is_p, i % (self.tp // tm), (i - npt) % (self.ts // tm))
        cnt = jnp.where(is_p, self.tp // tm, self.ts // tm)
        return pos, cnt


def _mod_kernel(c_ref, w_ref, b_ref, o_ref):
    c = c_ref[...]
    s = c * _sigmoid(c)
    o_ref[...] = _dot(s, w_ref[...]) + b_ref[...]


def _modulation(cond8, w_mod, b_mod):
    depth, d, width = w_mod.shape
    tn = MOD_COLS
    return pl.pallas_call(
        _mod_kernel,
        out_shape=jax.ShapeDtypeStruct((depth, SUBLANES, width), F32),
        grid=(depth, width // tn),
        in_specs=[
            pl.BlockSpec((SUBLANES, d), lambda l, j: (0, 0)),
            pl.BlockSpec((None, d, tn), lambda l, j: (l, 0, j)),
            pl.BlockSpec((None, 1, tn), lambda l, j: (l, 0, j)),
        ],
        out_specs=pl.BlockSpec((None, SUBLANES, tn), lambda l, j: (l, 0, j)),
        compiler_params=_params(("arbitrary", "arbitrary")),
        name="modulation",
    )(cond8, w_mod, b_mod.reshape(depth, 1, width))


def _in_proj_kernel(x_ref, mod_ref, w_ref, o_ref, wbf_ref):
    @pl.when(pl.program_id(1) == 0)
    def _():
        wbf_ref[...] = w_ref[...].astype(BF16)

    shift = mod_ref[0:1, :]
    scale = mod_ref[1:2, :]
    h = (x_ref[...] * (1.0 + scale) + shift).astype(BF16)
    o_ref[...] = _dot(h, wbf_ref[...]).astype(o_ref.dtype)


def _in_proj(dm, layer, x, mod, w_in):
    tm, tn = IN_PROJ_ROWS, IN_PROJ_COLS
    d = dm.d
    return pl.pallas_call(
        _in_proj_kernel,
        out_shape=jax.ShapeDtypeStruct((dm.n, dm.in_w), BF16),
        grid=(dm.in_w // tn, dm.n // tm),
        in_specs=[
            pl.BlockSpec((tm, d), lambda j, i: (i, 0)),
            pl.BlockSpec((None, None, N_MOD, d), lambda j, i: (layer, dm.group_of_tile(i, tm), 0, 0)),
            pl.BlockSpec((None, d, tn), lambda j, i: (layer, 0, j)),
        ],
        out_specs=pl.BlockSpec((tm, tn), lambda j, i: (i, j)),
        scratch_shapes=[pltpu.VMEM((d, tn), BF16)],
        compiler_params=_params(("arbitrary", "arbitrary")),
        name="in_proj",
    )(x, mod, w_in)


def _rope(x, cos, sin_signed):
    lane = lax.broadcasted_iota(jnp.int32, x.shape, 1)
    nxt = pltpu.roll(x, HEAD_DIM - 1, 1)
    prv = pltpu.roll(x, 1, 1)
    partner = jnp.where((lane & 1) == 0, nxt, prv)
    return x * cos + partner * sin_signed


def _rms(x, g):
    return x * lax.rsqrt(jnp.mean(x * x, axis=-1, keepdims=True) + NORM_EPS) * g


def _qk_prep_kernel(q_ref, k_ref, v_ref, cos_ref, sin_ref, qg_ref, kg_ref,
                    qo_ref, ko_ref, kn_ref, vo_ref):
    cos = cos_ref[...]
    sin = sin_ref[...]
    qg = qg_ref[...]
    kg = kg_ref[...]
    scale = HEAD_DIM ** -0.5
    for h in range(N_HEADS):
        cols = slice(h * HEAD_DIM, (h + 1) * HEAD_DIM)
        qn = _rms(q_ref[:, cols].astype(F32), qg)
        qo_ref[:, cols] = (_rope(qn, cos, sin) * scale).astype(qo_ref.dtype)
    for h in range(N_KV_HEADS):
        cols = slice(h * HEAD_DIM, (h + 1) * HEAD_DIM)
        kn = _rms(k_ref[:, cols].astype(F32), kg)
        kn_ref[:, cols] = kn
        ko_ref[:, cols] = _rope(kn, cos, sin).astype(ko_ref.dtype)
    vo_ref[...] = v_ref[...].astype(F32)


def _rope_tables(dm, tm):
    t = dm.ts
    rows = t // GRID_W
    row = jnp.broadcast_to(jnp.arange(rows, dtype=F32)[:, None], (rows, GRID_W)).reshape(t)
    col = jnp.broadcast_to(jnp.arange(GRID_W, dtype=F32)[None, :], (rows, GRID_W)).reshape(t)
    half = HEAD_DIM // 2
    inv_freq = ROPE_THETA ** (-jnp.arange(0, half, 2, dtype=F32) / half)
    ang = jnp.concatenate([row[:, None] * inv_freq, col[:, None] * inv_freq], axis=-1)
    cos = jnp.repeat(jnp.cos(ang), 2, axis=-1)
    sin = jnp.repeat(jnp.sin(ang), 2, axis=-1)
    sign = jnp.tile(jnp.array([-1.0, 1.0], F32), half)
    cos = jnp.concatenate([jnp.ones((tm, HEAD_DIM), F32), cos], axis=0)
    sin = jnp.concatenate([jnp.zeros((tm, HEAD_DIM), F32), sin * sign], axis=0)
    return cos, sin


def _qk_prep(dm, layer, proj, cos, sin, q_norm, k_norm):
    tm = WIDE_ROW_TILE
    npt = dm.np_ // tm
    tps = dm.ts // tm

    def tab(i):
        return (jnp.where(i < npt, 0, 1 + (i - npt) % tps), 0)

    return pl.pallas_call(
        _qk_prep_kernel,
        out_shape=(
            jax.ShapeDtypeStruct((dm.n, ATTN_WIDTH), BF16),
            jax.ShapeDtypeStruct((dm.n, KV_WIDTH), BF16),
            jax.ShapeDtypeStruct((dm.n, KV_WIDTH), F32),
            jax.ShapeDtypeStruct((dm.n, KV_WIDTH), F32),
        ),
        grid=(dm.n // tm,),
        in_specs=[
            pl.BlockSpec((tm, ATTN_WIDTH), lambda i: (i, 0)),
            pl.BlockSpec((tm, KV_WIDTH), lambda i: (i, dm.off_k // KV_WIDTH)),
            pl.BlockSpec((tm, KV_WIDTH), lambda i: (i, dm.off_v // KV_WIDTH)),
            pl.BlockSpec((tm, HEAD_DIM), tab),
            pl.BlockSpec((tm, HEAD_DIM), tab),
            pl.BlockSpec((None, 1, HEAD_DIM), lambda i: (layer, 0, 0)),
            pl.BlockSpec((None, 1, HEAD_DIM), lambda i: (layer, 0, 0)),
        ],
        out_specs=(
            pl.BlockSpec((tm, ATTN_WIDTH), lambda i: (i, 0)),
            pl.BlockSpec((tm, KV_WIDTH), lambda i: (i, 0)),
            pl.BlockSpec((tm, KV_WIDTH), lambda i: (i, 0)),
            pl.BlockSpec((tm, KV_WIDTH), lambda i: (i, 0)),
        ),
        compiler_params=_params(("arbitrary",)),
        name="qk_prep",
    )(proj, proj, proj, cos, sin, q_norm, k_norm)


def _ctx_attn_kernel(q_ref, k_ref, v_ref, o_ref):
    for h in range(N_HEADS):
        kv = h // GROUP
        cols = slice(h * HEAD_DIM, (h + 1) * HEAD_DIM)
        kcols = slice(kv * HEAD_DIM, (kv + 1) * HEAD_DIM)
        s = _dot_nt(q_ref[:, cols], k_ref[:, kcols])
        m = jnp.max(s, axis=-1, keepdims=True)
        p = jnp.exp(s - m)
        l = jnp.sum(p, axis=-1, keepdims=True)
        o = _dot(p.astype(BF16), v_ref[:, kcols]) / l
        o_ref[:, cols] = o.astype(o_ref.dtype)


def _ctx_attn(dm, q, k, proj):
    t = dm.tp
    return pl.pallas_call(
        _ctx_attn_kernel,
        out_shape=jax.ShapeDtypeStruct((dm.np_, ATTN_WIDTH), BF16),
        grid=(dm.bp,),
        in_specs=[
            pl.BlockSpec((t, ATTN_WIDTH), lambda b: (b, 0)),
            pl.BlockSpec((t, KV_WIDTH), lambda b: (b, 0)),
            pl.BlockSpec((t, KV_WIDTH), lambda b: (b, dm.off_v // KV_WIDTH)),
        ],
        out_specs=pl.BlockSpec((t, ATTN_WIDTH), lambda b: (b, 0)),
        compiler_params=_params(("arbitrary",)),
        name="ctx_attn",
    )(q, k, proj)


def _lat_attn_kernel(q_ref, k_ref, v_ref, ck_ref, cv_ref, o_ref):
    ck = ck_ref[...].astype(BF16)
    cv = cv_ref[...].astype(BF16)
    for h in range(N_HEADS):
        kv = h // GROUP
        cols = slice(h * HEAD_DIM, (h + 1) * HEAD_DIM)
        kcols = slice(kv * HEAD_DIM, (kv + 1) * HEAD_DIM)
        qh = q_ref[:, cols]
        s_ctx = _dot_nt(qh, ck[:, kcols])
        s_lat = _dot_nt(qh, k_ref[:, kcols])
        m = jnp.maximum(jnp.max(s_ctx, axis=-1, keepdims=True),
                        jnp.max(s_lat, axis=-1, keepdims=True))
        p_ctx = jnp.exp(s_ctx - m)
        p_lat = jnp.exp(s_lat - m)
        l = jnp.sum(p_ctx, axis=-1, keepdims=True) + jnp.sum(p_lat, axis=-1, keepdims=True)
        o = _dot(p_ctx.astype(BF16), cv[:, kcols]) + _dot(p_lat.astype(BF16), v_ref[:, kcols])
        o_ref[:, cols] = (o / l).astype(o_ref.dtype)


def _lat_attn(dm, layer, q, k, proj, cache_k, cache_v):
    tq = ROW_TILE
    ts = dm.ts
    qb = ts // tq
    return pl.pallas_call(
        _lat_attn_kernel,
        out_shape=jax.ShapeDtypeStruct((dm.ns, ATTN_WIDTH), BF16),
        grid=(dm.bs, qb),
        in_specs=[
            pl.BlockSpec((tq, ATTN_WIDTH), lambda b, i: (dm.np_ // tq + b * qb + i, 0)),
            pl.BlockSpec((ts, KV_WIDTH), lambda b, i: (dm.np_ // ts + b, 0)),
            pl.BlockSpec((ts, KV_WIDTH), lambda b, i: (dm.np_ // ts + b, dm.off_v // KV_WIDTH)),
            pl.BlockSpec((None, None, dm.past, KV_WIDTH), lambda b, i: (b, layer, 0, 0)),
            pl.BlockSpec((None, None, dm.past, KV_WIDTH), lambda b, i: (b, layer, 0, 0)),
        ],
        out_specs=pl.BlockSpec((tq, ATTN_WIDTH), lambda b, i: (b * qb + i, 0)),
        compiler_params=_params(("arbitrary", "arbitrary")),
        name="lat_attn",
    )(q, k, proj, cache_k, cache_v)


def _local_kernel(dm, a_ref, ap_ref, an_ref, g_ref, gp_ref, gn_ref, z_ref, zp_ref, zn_ref,
                  u_ref, v_ref, cw_ref, cb_ref, clg_ref, clb_ref, slg_ref, slb_ref,
                  sw_ref, sbt_ref, conv_o, pool_o, sgu_o, ubuf, zbuf):
    t = ROW_TILE
    i = pl.program_id(0)
    pos, cnt = dm.seq_tile_pos(i, t)
    has_prev = (pos > 0).astype(F32)
    has_next = (pos < cnt - 1).astype(F32)

    def glu(a, g):
        return a[...].astype(F32) * _sigmoid(g[...].astype(F32))

    ubuf[0:HALO, :] = glu(ap_ref, gp_ref) * has_prev
    ubuf[HALO:HALO + t, :] = glu(a_ref, g_ref)
    ubuf[HALO + t:, :] = glu(an_ref, gn_ref) * has_next
    zbuf[0:HALO, :] = zp_ref[...].astype(F32) * has_prev
    zbuf[HALO:HALO + t, :] = z_ref[...].astype(F32)
    zbuf[HALO + t:, :] = zn_ref[...].astype(F32) * has_next

    half_rows = t // 2
    centre = CONV_KERNEL // 2
    for r in range(2):
        pieces = []
        for c in range(dm.cw // LANES):
            cols = slice(c * LANES, (c + 1) * LANES)
            acc = jnp.zeros((half_rows, LANES), F32)
            for k in range(CONV_KERNEL):
                start = HALO + r * half_rows + k - centre
                acc = acc + cw_ref[k:k + 1, cols] * ubuf[start:start + half_rows, cols]
            pieces.append(acc)
        y = jnp.concatenate(pieces, axis=1) + cb_ref[...]
        y = _layer_norm(y, clg_ref[...], clb_ref[...])
        conv_o[r * half_rows:(r + 1) * half_rows, :] = (y * _sigmoid(y)).astype(conv_o.dtype)

    seq_len = jnp.where(i < dm.np_ // t, dm.tp, dm.ts)
    tok = pos * t + lax.broadcasted_iota(jnp.int32, (t, LANES), 0)
    for gi, w in enumerate(POOL_WINDOWS):
        cols = slice(gi * LANES, (gi + 1) * LANES)
        acc = jnp.zeros((t, LANES), F32)
        for j in range(-(w // 2), w - w // 2):
            acc = acc + zbuf[HALO + j:HALO + j + t, cols]
        lo = jnp.maximum(tok - w // 2, 0)
        hi = jnp.minimum(tok - w // 2 + w, seq_len)
        mean = acc / (hi - lo).astype(F32)
        pool_o[:, cols] = (mean - zbuf[HALO:HALO + t, cols]).astype(pool_o.dtype)

    vn = _layer_norm(v_ref[...].astype(F32), slg_ref[...], slb_ref[...]).astype(BF16)
    for gi in range(SGU_GROUPS):
        cols = slice(gi * CHUNK, (gi + 1) * CHUNK)
        w = sw_ref[gi].astype(BF16)
        bias = sbt_ref[:, gi:gi + 1]
        for c in range(t // CHUNK):
            rows = slice(c * CHUNK, (c + 1) * CHUNK)
            mixed = _dot(w, vn[rows, cols]) + bias
            sgu_o[rows, cols] = (u_ref[rows, cols].astype(F32) * mixed).astype(sgu_o.dtype)


def _local_branches(dm, layer, proj, conv_w, conv_b, conv_ln_g, conv_ln_b,
                    sgu_ln_g, sgu_ln_b, sgu_w, sgu_bt):
    t = ROW_TILE
    cw = dm.cw
    hb = t // HALO
    last = dm.n // HALO - 1
    ca = dm.off_a // cw

    def cur(c):
        return pl.BlockSpec((t, cw), lambda i: (i, c))

    def prev(c):
        return pl.BlockSpec((HALO, cw), lambda i: (jnp.maximum(i * hb - 1, 0), c))

    def nxt(c):
        return pl.BlockSpec((HALO, cw), lambda i: (jnp.minimum((i + 1) * hb, last), c))

    def vec():
        return pl.BlockSpec((None, 1, cw), lambda i: (layer, 0, 0))

    out = jax.ShapeDtypeStruct((dm.n, cw), BF16)
    return pl.pallas_call(
        functools.partial(_local_kernel, dm),
        out_shape=(out, out, out),
        grid=(dm.n // t,),
        in_specs=[
            cur(ca), prev(ca), nxt(ca),
            cur(ca + 1), prev(ca + 1), nxt(ca + 1),
            cur(ca + 2), prev(ca + 2), nxt(ca + 2),
            cur(ca + 3), cur(ca + 4),
            pl.BlockSpec((None, CONV_KERNEL, cw), lambda i: (layer, 0, 0)),
            vec(), vec(), vec(), vec(), vec(),
            pl.BlockSpec((None, SGU_GROUPS, CHUNK, CHUNK), lambda i: (layer, 0, 0, 0)),
            pl.BlockSpec((None, CHUNK, SGU_GROUPS), lambda i: (layer, 0, 0)),
        ],
        out_specs=(pl.BlockSpec((t, cw), lambda i: (i, 0)),) * 3,
        scratch_shapes=[pltpu.VMEM((t + 2 * HALO, cw), F32), pltpu.VMEM((t + 2 * HALO, cw), F32)],
        compiler_params=_params(("arbitrary",)),
        name="local_branches",
    )(proj, proj, proj, proj, proj, proj, proj, proj, proj, proj, proj,
      conv_w, conv_b, conv_ln_g, conv_ln_b, sgu_ln_g, sgu_ln_b, sgu_w, sgu_bt)


def _merge_kernel(dm, o_ref, cv_ref, pl_ref, sg_ref, g0_ref, g1_ref, g2_ref, g3_ref,
                  wa_ref, wc_ref, wp_ref, ws_ref, ps_ref, m_ref):
    cw = dm.cw
    for c in range(dm.d // cw):
        cols = slice(c * cw, (c + 1) * cw)
        ya = _dot(o_ref[...], wa_ref[:, cols])
        yb = _dot(cv_ref[...], wc_ref[:, cols])
        yc = _dot(pl_ref[:, c * LANES:(c + 1) * LANES], wp_ref[c]) * ps_ref[:, cols]
        yd = _dot(sg_ref[...], ws_ref[:, cols])
        merged = (_sigmoid(g0_ref[:, cols].astype(F32)) * ya
                  + _sigmoid(g1_ref[:, cols].astype(F32)) * yb
                  + _sigmoid(g2_ref[:, cols].astype(F32)) * yc
                  + _sigmoid(g3_ref[:, cols].astype(F32)) * yd)
        m_ref[:, cols] = merged.astype(m_ref.dtype)


def _merge(dm, layer, attn_o, conv_h, pool_h, sgu_h, proj, wa, wc, wp, ws, pool_scale):
    tm = WIDE_ROW_TILE
    d, cw = dm.d, dm.cw
    g0 = dm.off_gates // d
    assert d // cw == POOL_GROUPS and wp.shape[-1] == cw

    def gate(b):
        return pl.BlockSpec((tm, d), lambda i: (i, g0 + b))

    return pl.pallas_call(
        functools.partial(_merge_kernel, dm),
        out_shape=jax.ShapeDtypeStruct((dm.n, d), BF16),
        grid=(dm.n // tm,),
        in_specs=[
            pl.BlockSpec((tm, ATTN_WIDTH), lambda i: (i, 0)),
            pl.BlockSpec((tm, cw), lambda i: (i, 0)),
            pl.BlockSpec((tm, cw), lambda i: (i, 0)),
            pl.BlockSpec((tm, cw), lambda i: (i, 0)),
            gate(0), gate(1), gate(2), gate(3),
            pl.BlockSpec((None, ATTN_WIDTH, d), lambda i: (layer, 0, 0)),
            pl.BlockSpec((None, cw, d), lambda i: (layer, 0, 0)),
            pl.BlockSpec((None, POOL_GROUPS, LANES, cw), lambda i: (layer, 0, 0, 0)),
            pl.BlockSpec((None, cw, d), lambda i: (layer, 0, 0)),
            pl.BlockSpec((None, 1, d), lambda i: (layer, 0, 0)),
        ],
        out_specs=pl.BlockSpec((tm, d), lambda i: (i, 0)),
        compiler_params=_params(("arbitrary",)),
        name="merge",
    )(attn_o, conv_h, pool_h, sgu_h, proj, proj, proj, proj, wa, wc, wp, ws, pool_scale)


def _out_proj_kernel(alpha, m_ref, x_ref, mod_ref, w_ref, lg_ref, lb_ref, wr_ref, br_ref,
                     x1_ref, h2_ref, idx_ref, comb_ref):
    gate1 = mod_ref[2:3, :]
    shift2 = mod_ref[3:4, :]
    scale2 = mod_ref[4:5, :]
    wr = wr_ref[...]
    wr_hi = wr.astype(BF16)
    wr_lo = (wr - wr_hi.astype(F32)).astype(BF16)

    half_rows = m_ref.shape[0] // 2
    for r in range(2):
        rows = slice(r * half_rows, (r + 1) * half_rows)
        y = _dot(m_ref[rows, :], w_ref[...])
        x1 = _layer_norm(alpha * x_ref[rows, :] + gate1 * y, lg_ref[...], lb_ref[...])
        x1_ref[rows, :] = x1
        h2 = x1 * (1.0 + scale2) + shift2
        h2_hi = h2.astype(BF16)
        packed = _pack_pairs(h2)
        for p in range(h2_ref.shape[0]):
            h2_ref[p, rows, :] = packed[:, p * GATHER_WORDS:(p + 1) * GATHER_WORDS]

        h2_lo = (h2 - h2_hi.astype(F32)).astype(BF16)
        logits = _dot(h2_hi, wr_hi) + _dot(h2_hi, wr_lo) + _dot(h2_lo, wr_hi) + br_ref[...]

        lane = lax.broadcasted_iota(jnp.int32, logits.shape, 1)
        idx_out = jnp.zeros(logits.shape, jnp.int32)
        val_out = jnp.zeros(logits.shape, F32)
        top = None
        den = jnp.zeros((logits.shape[0], 1), F32)
        for k in range(TOP_K):
            m = jnp.max(logits, axis=-1, keepdims=True)
            ik = jnp.min(jnp.where(logits == m, lane, LANES), axis=-1, keepdims=True)
            if top is None:
                top = m
            e = jnp.exp(m - top)
            den = den + e
            idx_out = jnp.where(lane == k, ik, idx_out)
            val_out = jnp.where(lane == k, e, val_out)
            logits = jnp.where(lane == ik, -jnp.inf, logits)
        idx_ref[rows, :] = idx_out
        comb_ref[rows, :] = val_out / den


def _out_proj(dm, layer, merged, x, mod, w_out, ln_g, ln_b, w_router, b_router):
    tm = WIDE_ROW_TILE
    d = dm.d
    alpha = (2 * dm.depth) ** 0.25

    def vec():
        return pl.BlockSpec((None, 1, d), lambda i: (layer, 0, 0))

    return pl.pallas_call(
        functools.partial(_out_proj_kernel, alpha),
        out_shape=(
            jax.ShapeDtypeStruct((dm.n, d), F32),
            jax.ShapeDtypeStruct((d // 2 // GATHER_WORDS, dm.n, GATHER_WORDS), jnp.uint32),
            jax.ShapeDtypeStruct((dm.n, LANES), jnp.int32),
            jax.ShapeDtypeStruct((dm.n, LANES), F32),
        ),
        grid=(dm.n // tm,),
        in_specs=[
            pl.BlockSpec((tm, d), lambda i: (i, 0)),
            pl.BlockSpec((tm, d), lambda i: (i, 0)),
            pl.BlockSpec((None, None, N_MOD, d), lambda i: (layer, dm.group_of_tile(i, tm), 0, 0)),
            pl.BlockSpec((None, d, d), lambda i: (layer, 0, 0)),
            vec(), vec(),
            pl.BlockSpec((None, d, LANES), lambda i: (layer, 0, 0)),
            pl.BlockSpec((None, 1, LANES), lambda i: (layer, 0, 0)),
        ],
        out_specs=(
            pl.BlockSpec((tm, d), lambda i: (i, 0)),
            pl.BlockSpec((d // 2 // GATHER_WORDS, tm, GATHER_WORDS), lambda i: (0, i, 0)),
            pl.BlockSpec((tm, LANES), lambda i: (i, 0)),
            pl.BlockSpec((tm, LANES), lambda i: (i, 0)),
        ),
        compiler_params=_params(("arbitrary",)),
        name="out_proj",
    )(merged, x, mod, w_out, ln_g, ln_b, w_router, b_router)


def _moe_live_chunks(nv, fn):
    pair = 2 * MOE_SUB
    n_pairs = nv // pair

    def full_pair(i, carry):
        fn(pl.multiple_of(i * pair, pair), pair, False)
        return carry

    lax.fori_loop(0, n_pairs, full_pair, 0)

    base = pl.multiple_of(n_pairs * pair, pair)
    n = nv - base

    def sub_and_tail():
        fn(base, MOE_SUB, False)
        fn(base + MOE_SUB, MOE_TAIL, True)

    pl.when(n > MOE_SUB + MOE_TAIL)(functools.partial(fn, base, pair, True))
    pl.when(jnp.logical_and(n > MOE_SUB, n <= MOE_SUB + MOE_TAIL))(sub_and_tail)
    pl.when(jnp.logical_and(n > MOE_TAIL, n <= MOE_SUB))(functools.partial(fn, base, MOE_SUB, True))
    pl.when(jnp.logical_and(n > 0, n <= MOE_TAIL))(functools.partial(fn, base, MOE_TAIL, True))


def _moe_kernel(nfa, be_ref, nv_ref, na_ref, x_ref, wg_ref, bg_ref, wu_ref, bu_ref, wd_ref, bd_ref,
                o_ref, act_ref):
    s = pl.program_id(0)
    j = pl.program_id(1)
    nv = nv_ref[s]
    tf = MOE_FF_TILE
    active = s < na_ref[0]

    @pl.when(jnp.logical_and(active, j < nfa))
    def _():
        bg = bg_ref[...]
        bu = bu_ref[...]
        col0 = pl.multiple_of(j * tf, tf)

        def chunk(base, m, has_dead_rows):
            halves = [_unpack_pairs(x_ref[p, pl.ds(base, m), :]) for p in range(x_ref.shape[0])]
            if has_dead_rows:
                live = base + lax.broadcasted_iota(jnp.int32, (m, GATHER_WORDS), 0) < nv
                halves = [(jnp.where(live, lo, 0.0), jnp.where(live, hi, 0.0)) for lo, hi in halves]
            xs = jnp.concatenate([lo for lo, _ in halves] + [hi for _, hi in halves], axis=1)
            gt = jnp.minimum(_dot(xs, wg_ref[...]) + bg, SWIGLU_LIMIT)
            up = jnp.clip(_dot(xs, wu_ref[...]) + bu, -SWIGLU_LIMIT, SWIGLU_LIMIT)
            act = gt * _sigmoid(SWIGLU_ALPHA * gt) * (up + 1.0)
            act_ref[pl.ds(base, m), pl.ds(col0, tf)] = act.astype(BF16)

        _moe_live_chunks(nv, chunk)

    @pl.when(j >= nfa)
    def _():
        bd = bd_ref[...]

        def chunk(base, m, has_dead_rows):
            del has_dead_rows
            y = _dot(act_ref[pl.ds(base, m), :].astype(F32), wd_ref[...]) + bd
            o_ref[pl.ds(base, m), :] = _pack_pairs(y)

        _moe_live_chunks(nv, chunk)

        def zero_fill(c, carry):
            rows = pl.ds(pl.multiple_of(c * MOE_TAIL, MOE_TAIL), MOE_TAIL)
            o_ref[rows, :] = jnp.zeros((MOE_TAIL, o_ref.shape[1]), o_ref.dtype)
            return carry

        lax.fori_loop((nv + MOE_TAIL - 1) // MOE_TAIL, MOE_SUPER // MOE_TAIL, zero_fill, 0)


def _moe(dm, layer, x_sorted, blk_e, blk_nv, n_act, w_gate, b_gate, w_up, b_up, w_down, b_down):
    d, ff, tf, tn, r = dm.d, dm.ff, MOE_FF_TILE, MOE_OUT_TILE, MOE_SUPER
    n_planes = x_sorted.shape[0]
    nsb = x_sorted.shape[1] // r
    nfa = ff // tf
    nfb = d // tn
    assert tn // 2 == GATHER_WORDS

    def blk(s, na):
        return jnp.minimum(s, na[0] - 1)

    def col_a(s, j, na):
        return jnp.where(s < na[0], jnp.minimum(j, nfa - 1), nfa - 1)

    def col_b(s, j, na):
        return jnp.where(s < na[0], jnp.maximum(j - nfa, 0), nfb - 1)

    grid_spec = pltpu.PrefetchScalarGridSpec(
        num_scalar_prefetch=3,
        grid=(nsb, nfa + nfb),
        in_specs=[
            pl.BlockSpec((n_planes, r, GATHER_WORDS), lambda s, j, be, nv, na: (0, blk(s, na), 0)),
            pl.BlockSpec((None, None, d, tf), lambda s, j, be, nv, na: (layer, be[s], 0, col_a(s, j, na))),
            pl.BlockSpec((None, None, 1, tf), lambda s, j, be, nv, na: (layer, be[s], 0, col_a(s, j, na))),
            pl.BlockSpec((None, None, d, tf), lambda s, j, be, nv, na: (layer, be[s], 0, col_a(s, j, na))),
            pl.BlockSpec((None, None, 1, tf), lambda s, j, be, nv, na: (layer, be[s], 0, col_a(s, j, na))),
            pl.BlockSpec((None, None, ff, tn), lambda s, j, be, nv, na: (layer, be[s], 0, col_b(s, j, na))),
            pl.BlockSpec((None, None, 1, tn), lambda s, j, be, nv, na: (layer, be[s], 0, col_b(s, j, na))),
        ],
        out_specs=pl.BlockSpec((None, r, tn // 2), lambda s, j, be, nv, na: (jnp.maximum(j - nfa, 0), s, 0)),
        scratch_shapes=[
            pltpu.VMEM((r, ff), BF16),
        ],
    )
    return pl.pallas_call(
        functools.partial(_moe_kernel, nfa),
        out_shape=jax.ShapeDtypeStruct((nfb, nsb * r, tn // 2), jnp.uint32),
        grid_spec=grid_spec,
        compiler_params=_params(("arbitrary", "arbitrary"), MOE_VMEM_LIMIT_BYTES),
        name="moe_ffn",
    )(blk_e, blk_nv, n_act, x_sorted, w_gate, b_gate, w_up, b_up, w_down, b_down)


def _route(dm, top_idx, n_super):
    e, r = dm.n_exp, MOE_SUPER
    flat_e = top_idx.reshape(-1)
    onehot = (flat_e[:, None] == jnp.arange(e, dtype=jnp.int32)[None, :]).astype(jnp.int32)
    rank = jnp.take_along_axis(jnp.cumsum(onehot, axis=0), flat_e[:, None], axis=1)[:, 0] - 1
    counts = jnp.sum(onehot, axis=0)
    nsb_e = (counts + r - 1) // r
    sb_end = jnp.cumsum(nsb_e)
    sb_start = sb_end - nsb_e
    dest = sb_start[flat_e] * r + rank
    n_act = sb_end[-1]
    sidx = jnp.arange(n_super, dtype=jnp.int32)
    blk_e = jnp.clip(jnp.searchsorted(sb_end, jnp.minimum(sidx, n_act - 1), side='right'), 0, e - 1)
    blk_e = blk_e.astype(jnp.int32)
    blk_nv = jnp.clip(counts[blk_e] - (sidx - sb_start[blk_e]) * r, 0, r)
    blk_nv = jnp.where(sidx < n_act, blk_nv, 0).astype(jnp.int32)
    return dest.astype(jnp.int32), blk_e, blk_nv, n_act.astype(jnp.int32).reshape(1)


def _combine_kernel(alpha, y_ref, comb_ref, x_ref, mod_ref, lg_ref, lb_ref, o_ref):
    gate2 = mod_ref[5:6, :]
    comb = comb_ref[...]
    slabs = []
    for b in range(y_ref.shape[0]):
        lo_sum = None
        hi_sum = None
        for k in range(TOP_K):
            lo, hi = _unpack_pairs(y_ref[b, k])
            wk = comb[:, k:k + 1]
            lo_sum = wk * lo if lo_sum is None else lo_sum + wk * lo
            hi_sum = wk * hi if hi_sum is None else hi_sum + wk * hi
        slabs += [lo_sum, hi_sum]
    ff = jnp.concatenate(slabs, axis=1)
    o_ref[...] = _layer_norm(alpha * x_ref[...] + gate2 * ff, lg_ref[...], lb_ref[...])


def _combine(dm, layer, y_tok, comb, x1, mod, ln_g, ln_b):
    tm = WIDE_ROW_TILE
    d = dm.d
    alpha = (2 * dm.depth) ** 0.25
    return pl.pallas_call(
        functools.partial(_combine_kernel, alpha),
        out_shape=jax.ShapeDtypeStruct((dm.n, d), F32),
        grid=(dm.n // tm,),
        in_specs=[
            pl.BlockSpec((y_tok.shape[0], TOP_K, tm, y_tok.shape[3]), lambda i: (0, 0, i, 0)),
            pl.BlockSpec((tm, LANES), lambda i: (i, 0)),
            pl.BlockSpec((tm, d), lambda i: (i, 0)),
            pl.BlockSpec((None, None, N_MOD, d), lambda i: (layer, dm.group_of_tile(i, tm), 0, 0)),
            pl.BlockSpec((None, 1, d), lambda i: (layer, 0, 0)),
            pl.BlockSpec((None, 1, d), lambda i: (layer, 0, 0)),
        ],
        out_specs=pl.BlockSpec((tm, d), lambda i: (i, 0)),
        compiler_params=_params(("arbitrary",)),
        name="combine",
    )(y_tok, comb, x1, mod, ln_g, ln_b)


def kernel(x_prompt, x_sample, cache_k, cache_v, c, c_ctx, w_mod, b_mod, w_in, q_norm, k_norm,
           w_attn_o, conv_w, conv_b, conv_ln_g, conv_ln_b, w_conv_o, w_pool, pool_scale,
           sgu_ln_g, sgu_ln_b, sgu_w, sgu_b, w_sgu_o, w_out, ln1_g, ln1_b, ln2_g, ln2_b,
           w_router, b_router, w_gate, b_gate, w_up, b_up, w_down, b_down):
    bp, tp, d = x_prompt.shape
    bs, ts, _ = x_sample.shape
    depth, n_exp, _, ff = w_gate.shape
    past = cache_k.shape[2]
    dm = _Dims(bp, tp, bs, ts, d, past, depth, n_exp, ff)
    assert 1 + bs <= SUBLANES and n_exp <= LANES

    x = jnp.concatenate([x_prompt.reshape(dm.np_, d), x_sample.reshape(dm.ns, d)], axis=0)
    cond = jnp.concatenate([c_ctx[None, :], c, jnp.zeros((SUBLANES - 1 - bs, d), F32)], axis=0)
    mod = _modulation(cond, w_mod, b_mod).reshape(depth, SUBLANES, N_MOD, d)

    cos, sin = _rope_tables(dm, WIDE_ROW_TILE)
    ck = cache_k.reshape(bs, depth, past, KV_WIDTH)
    cv = cache_v.reshape(bs, depth, past, KV_WIDTH)

    def row(p):
        return p.reshape(depth, 1, p.shape[-1])

    wa, wc, wp, ws, wo = (w.astype(BF16) for w in (w_attn_o, w_conv_o, w_pool, w_sgu_o, w_out))
    sgu_bt = jnp.swapaxes(sgu_b, 1, 2)
    w_router_p = jnp.pad(w_router, ((0, 0), (0, 0), (0, LANES - n_exp)))
    b_router_p = jnp.pad(b_router, ((0, 0), (0, LANES - n_exp)), constant_values=-jnp.inf)
    b_router_p = b_router_p.reshape(depth, 1, LANES)
    bg, bu = b_gate.reshape(depth, n_exp, 1, ff), b_up.reshape(depth, n_exp, 1, ff)
    bd = b_down.reshape(depth, n_exp, 1, d)

    nk = dm.n * TOP_K
    n_super = nk // MOE_SUPER + n_exp
    ks, vs = [], []
    for l in range(depth):
        proj = _in_proj(dm, l, x, mod, w_in)
        q, k, kn, v32 = _qk_prep(dm, l, proj, cos, sin, row(q_norm), row(k_norm))
        ks.append(kn[:dm.np_])
        vs.append(v32[:dm.np_])
        attn_o = jnp.concatenate(
            [_ctx_attn(dm, q, k, proj), _lat_attn(dm, l, q, k, proj, ck, cv)], axis=0)
        conv_h, pool_h, sgu_h = _local_branches(
            dm, l, proj, conv_w, row(conv_b), row(conv_ln_g), row(conv_ln_b),
            row(sgu_ln_g), row(sgu_ln_b), sgu_w, sgu_bt)
        merged = _merge(dm, l, attn_o, conv_h, pool_h, sgu_h, proj, wa, wc, wp, ws, row(pool_scale))
        x1, h2, top_idx, comb = _out_proj(dm, l, merged, x, mod, wo, row(ln1_g), row(ln1_b),
                                          w_router_p, b_router_p)
        dest, blk_e, blk_nv, n_act = _route(dm, top_idx[:, :TOP_K], n_super)
        dest_km = dest.reshape(dm.n, TOP_K).T
        x_sorted = _scatter_rows(h2, dest_km, n_super * MOE_SUPER)
        y_sorted = _moe(dm, l, x_sorted, blk_e, blk_nv, n_act, w_gate, bg, w_up, bu, w_down, bd)
        y_tok = _gather_rows(y_sorted, dest_km.reshape(-1))
        y_tok = y_tok.reshape(y_sorted.shape[0], TOP_K, dm.n, GATHER_WORDS)
        x = _combine(dm, l, y_tok, comb, x1, mod, row(ln2_g), row(ln2_b))

    y_prompt = x[:dm.np_].reshape(bp, tp, d)
    y_sample = x[dm.np_:].reshape(bs, ts, d)
    state_k = jnp.stack(ks, axis=1).reshape(bp, tp, depth, N_KV_HEADS, HEAD_DIM).swapaxes(1, 2)
    state_v = jnp.stack(vs, axis=1).reshape(bp, tp, depth, N_KV_HEADS, HEAD_DIM).swapaxes(1, 2)
    return (y_prompt, y_sample, state_k, state_v)
```

```python
import functools

import jax
import jax.numpy as jnp
from jax import lax
from jax.experimental import pallas as pl
from jax.experimental.pallas import tpu as pltpu
from jax.experimental.pallas import tpu_sc as plsc

F32 = jnp.float32
BF16 = jnp.bfloat16

N_HEADS = 8
N_KV_HEADS = 2
HEAD_DIM = 128
GRID_W = 64
ROPE_THETA = 10000.0
CONV_KERNEL = 31
POOL_WINDOWS = (2, 4, 8, 16)
POOL_GROUPS = 4
SGU_GROUPS = 4
CHUNK = 128
N_BRANCHES = 4
TOP_K = 4
SWIGLU_LIMIT = 7.0
SWIGLU_ALPHA = 1.702
NORM_EPS = 1e-6
N_MOD = 6

ATTN_WIDTH = N_HEADS * HEAD_DIM
KV_WIDTH = N_KV_HEADS * HEAD_DIM
GROUP = N_HEADS // N_KV_HEADS

LANES = 128
SUBLANES = 8
VMEM_LIMIT_BYTES = 56 * 1024 * 1024
MOE_VMEM_LIMIT_BYTES = 60 * 1024 * 1024

ROW_TILE = 256
WIDE_ROW_TILE = 512
HALO = 16
IN_PROJ_ROWS = 1024
IN_PROJ_COLS = 1024
MOD_COLS = 1024
MOE_SUPER = 2048
MOE_SUB = 256
MOE_TAIL = 128
MOE_FF_TILE = 512
MOE_OUT_TILE = 512
GATHER_WINDOW = 128
GATHER_WORDS = 256


def _dot(a, b):
    return jnp.dot(a, b, preferred_element_type=F32)


def _dot_nt(a, b):
    return lax.dot_general(a, b, (((1,), (1,)), ((), ())), preferred_element_type=F32)


def _layer_norm(x, g, b):
    mu = jnp.mean(x, axis=-1, keepdims=True)
    xc = x - mu
    var = jnp.mean(xc * xc, axis=-1, keepdims=True)
    return xc * lax.rsqrt(var + NORM_EPS) * g + b


def _sigmoid(x):
    return 0.5 * jnp.tanh(0.5 * x) + 0.5


def _params(sem, vmem=VMEM_LIMIT_BYTES):
    return pltpu.CompilerParams(dimension_semantics=sem, vmem_limit_bytes=vmem)


_HIGH_HALF = 0xFFFF0000


def _pack_pairs(x):
    w = x.shape[1] // 2
    lo = lax.bitcast_convert_type(x[:, :w].astype(BF16).astype(F32), jnp.uint32)
    hi = lax.bitcast_convert_type(x[:, w:].astype(BF16).astype(F32), jnp.uint32)
    return (lo >> 16) | (hi & jnp.uint32(_HIGH_HALF))


def _unpack_pairs(words):
    lo = lax.bitcast_convert_type(words << 16, F32)
    hi = lax.bitcast_convert_type(words & jnp.uint32(_HIGH_HALF), F32)
    return lo, hi


def _gather_rows(planes, idx):
    n_planes, n_rows, width = planes.shape
    assert planes.dtype.itemsize == 4 and width == GATHER_WORDS
    offsets = jnp.arange(n_planes, dtype=jnp.int32) * n_rows
    flat_idx = (offsets[:, None] + idx[None, :]).reshape(-1)
    out = _gather_pieces(planes.reshape(n_planes * n_rows, width), flat_idx)
    return out.reshape(n_planes, idx.shape[0], width)


def _gather_pieces(table, idx):
    n_idx = idx.shape[0]
    width = table.shape[1]
    assert n_idx % GATHER_WINDOW == 0
    mesh = plsc.VectorSubcoreMesh(core_axis_name="core", subcore_axis_name="subcore")

    @functools.partial(
        pl.kernel,
        out_type=jax.ShapeDtypeStruct((n_idx, width), table.dtype),
        mesh=mesh,
        scratch_types=[],
    )
    def gather_kernel(table_hbm, idx_hbm, out_hbm):
        def body(idx_vmem, out_vmem):
            pltpu.sync_copy(table_hbm.at[idx_vmem.at[0]], out_vmem)

        pltpu.emit_pipeline(
            body,
            grid=(n_idx // GATHER_WINDOW,),
            in_specs=[pl.BlockSpec((1, GATHER_WINDOW), lambda i: (0, i))],
            out_specs=[pl.BlockSpec((GATHER_WINDOW, width), lambda i: (i, 0))],
            core_axis_name=("core", "subcore"),
            dimension_semantics=(pltpu.PARALLEL,),
        )(idx_hbm, out_hbm)

    return gather_kernel(table, idx.reshape(1, n_idx))


def _scatter_rows(planes, dest, n_out):
    n_planes, n_rows, width = planes.shape
    n_copies = dest.shape[0]
    assert planes.dtype.itemsize == 4 and width == GATHER_WORDS and n_rows % GATHER_WINDOW == 0
    offsets = jnp.arange(n_planes, dtype=jnp.int32) * n_out
    flat_idx = (dest[:, None, :] + offsets[None, :, None]).reshape(-1)
    out = _scatter_pieces(planes.reshape(n_planes * n_rows, width), flat_idx, n_copies, n_planes * n_out)
    return out.reshape(n_planes, n_out, width)


def _scatter_pieces(table, idx, n_copies, n_out):
    n_src, width = table.shape
    src_blocks = n_src // GATHER_WINDOW
    mesh = plsc.VectorSubcoreMesh(core_axis_name="core", subcore_axis_name="subcore")

    @functools.partial(
        pl.kernel,
        out_type=jax.ShapeDtypeStruct((n_out, width), table.dtype),
        mesh=mesh,
        scratch_types=[],
    )
    def scatter_kernel(table_hbm, idx_hbm, out_hbm):
        def body(rows_vmem, idx_vmem):
            pltpu.sync_copy(rows_vmem, out_hbm.at[idx_vmem.at[0]])

        pltpu.emit_pipeline(
            body,
            grid=(n_copies * src_blocks,),
            in_specs=[pl.BlockSpec((GATHER_WINDOW, width), lambda i: (i % src_blocks, 0)),
                      pl.BlockSpec((1, GATHER_WINDOW), lambda i: (0, i))],
            out_specs=[],
            core_axis_name=("core", "subcore"),
            dimension_semantics=(pltpu.PARALLEL,),
        )(table_hbm, idx_hbm)

    return scatter_kernel(table, idx.reshape(1, n_copies * n_src))


class _Dims:
    def __init__(self, bp, tp, bs, ts, d, past, depth, n_exp, ff):
        self.bp, self.tp, self.bs, self.ts = bp, tp, bs, ts
        self.d, self.past, self.depth, self.n_exp, self.ff = d, past, depth, n_exp, ff
        self.np_ = bp * tp
        self.ns = bs * ts
        self.n = self.np_ + self.ns
        self.cw = d // 4
        self.in_w = ATTN_WIDTH + 2 * KV_WIDTH + 5 * self.cw + N_BRANCHES * d
        self.off_k = ATTN_WIDTH
        self.off_v = ATTN_WIDTH + KV_WIDTH
        self.off_a = ATTN_WIDTH + 2 * KV_WIDTH
        self.off_gates = self.off_a + 5 * self.cw
        assert self.cw == POOL_GROUPS * LANES == SGU_GROUPS * CHUNK
        assert self.off_a % self.cw == 0 and self.off_gates % d == 0
        assert tp % ROW_TILE == 0 and ts % ROW_TILE == 0 and self.np_ % ts == 0
        assert ts % GRID_W == 0
        assert IN_PROJ_ROWS % WIDE_ROW_TILE == 0 and self.np_ % IN_PROJ_ROWS == 0 and ts % IN_PROJ_ROWS == 0

    def group_of_tile(self, i, tm):
        npt = self.np_ // tm
        return jnp.where(i < npt, 0, 1 + (i - npt) // (self.ts // tm))

    def seq_tile_pos(self, i, tm):
        npt = self.np_ // tm
        is_p = i < npt
        pos = jnp.where(is_p, i % (self.tp // tm), (i - npt) % (self.ts // tm))
        cnt = jnp.where(is_p, self.tp // tm, self.ts // tm)
        return pos, cnt


def _mod_kernel(c_ref, w_ref, b_ref, o_ref):
    c = c_ref[...]
    s = c * _sigmoid(c)
    o_ref[...] = _dot(s, w_ref[...]) + b_ref[...]


def _modulation(cond8, w_mod, b_mod):
    depth, d, width = w_mod.shape
    tn = MOD_COLS
    return pl.pallas_call(
        _mod_kernel,
        out_shape=jax.ShapeDtypeStruct((depth, SUBLANES, width), F32),
        grid=(depth, width // tn),
        in_specs=[
            pl.BlockSpec((SUBLANES, d), lambda l, j: (0, 0)),
            pl.BlockSpec((None, d, tn), lambda l, j: (l, 0, j)),
            pl.BlockSpec((None, 1, tn), lambda l, j: (l, 0, j)),
        ],
        out_specs=pl.BlockSpec((None, SUBLANES, tn), lambda l, j: (l, 0, j)),
        compiler_params=_params(("arbitrary", "arbitrary")),
        name="modulation",
    )(cond8, w_mod, b_mod.reshape(depth, 1, width))


def _in_proj_kernel(x_ref, mod_ref, w_ref, o_ref, wbf_ref):
    @pl.when(pl.program_id(1) == 0)
    def _():
        wbf_ref[...] = w_ref[...].astype(BF16)

    shift = mod_ref[0:1, :]
    scale = mod_ref[1:2, :]
    h = (x_ref[...] * (1.0 + scale) + shift).astype(BF16)
    o_ref[...] = _dot(h, wbf_ref[...]).astype(o_ref.dtype)


def _in_proj(dm, layer, x, mod, w_in):
    tm, tn = IN_PROJ_ROWS, IN_PROJ_COLS
    d = dm.d
    return pl.pallas_call(
        _in_proj_kernel,
        out_shape=jax.ShapeDtypeStruct((dm.n, dm.in_w), BF16),
        grid=(dm.in_w // tn, dm.n // tm),
        in_specs=[
            pl.BlockSpec((tm, d), lambda j, i: (i, 0)),
            pl.BlockSpec((None, None, N_MOD, d), lambda j, i: (layer, dm.group_of_tile(i, tm), 0, 0)),
            pl.BlockSpec((None, d, tn), lambda j, i: (layer, 0, j)),
        ],
        out_specs=pl.BlockSpec((tm, tn), lambda j, i: (i, j)),
        scratch_shapes=[pltpu.VMEM((d, tn), BF16)],
        compiler_params=_params(("arbitrary", "arbitrary")),
        name="in_proj",
    )(x, mod, w_in)


def _rope(x, cos, sin_signed):
    lane = lax.broadcasted_iota(jnp.int32, x.shape, 1)
    nxt = pltpu.roll(x, HEAD_DIM - 1, 1)
    prv = pltpu.roll(x, 1, 1)
    partner = jnp.where((lane & 1) == 0, nxt, prv)
    return x * cos + partner * sin_signed


def _rms(x, g):
    return x * lax.rsqrt(jnp.mean(x * x, axis=-1, keepdims=True) + NORM_EPS) * g


def _qk_prep_kernel(q_ref, k_ref, v_ref, cos_ref, sin_ref, qg_ref, kg_ref,
                    qo_ref, ko_ref, kn_ref, vo_ref):
    cos = cos_ref[...]
    sin = sin_ref[...]
    qg = qg_ref[...]
    kg = kg_ref[...]
    scale = HEAD_DIM ** -0.5
    for h in range(N_HEADS):
        cols = slice(h * HEAD_DIM, (h + 1) * HEAD_DIM)
        qn = _rms(q_ref[:, cols].astype(F32), qg)
        qo_ref[:, cols] = (_rope(qn, cos, sin) * scale).astype(qo_ref.dtype)
    for h in range(N_KV_HEADS):
        cols = slice(h * HEAD_DIM, (h + 1) * HEAD_DIM)
        kn = _rms(k_ref[:, cols].astype(F32), kg)
        kn_ref[:, cols] = kn
        ko_ref[:, cols] = _rope(kn, cos, sin).astype(ko_ref.dtype)
    vo_ref[...] = v_ref[...].astype(F32)


def _rope_tables(dm, tm):
    t = dm.ts
    rows = t // GRID_W
    row = jnp.broadcast_to(jnp.arange(rows, dtype=F32)[:, None], (rows, GRID_W)).reshape(t)
    col = jnp.broadcast_to(jnp.arange(GRID_W, dtype=F32)[None, :], (rows, GRID_W)).reshape(t)
    half = HEAD_DIM // 2
    inv_freq = ROPE_THETA ** (-jnp.arange(0, half, 2, dtype=F32) / half)
    ang = jnp.concatenate([row[:, None] * inv_freq, col[:, None] * inv_freq], axis=-1)
    cos = jnp.repeat(jnp.cos(ang), 2, axis=-1)
    sin = jnp.repeat(jnp.sin(ang), 2, axis=-1)
    sign = jnp.tile(jnp.array([-1.0, 1.0], F32), half)
    cos = jnp.concatenate([jnp.ones((tm, HEAD_DIM), F32), cos], axis=0)
    sin = jnp.concatenate([jnp.zeros((tm, HEAD_DIM), F32), sin * sign], axis=0)
    return cos, sin


def _qk_prep(dm, layer, proj, cos, sin, q_norm, k_norm):
    tm = WIDE_ROW_TILE
    npt = dm.np_ // tm
    tps = dm.ts // tm

    def tab(i):
        return (jnp.where(i < npt, 0, 1 + (i - npt) % tps), 0)

    return pl.pallas_call(
        _qk_prep_kernel,
        out_shape=(
            jax.ShapeDtypeStruct((dm.n, ATTN_WIDTH), BF16),
            jax.ShapeDtypeStruct((dm.n, KV_WIDTH), BF16),
            jax.ShapeDtypeStruct((dm.n, KV_WIDTH), F32),
            jax.ShapeDtypeStruct((dm.n, KV_WIDTH), F32),
        ),
        grid=(dm.n // tm,),
        in_specs=[
            pl.BlockSpec((tm, ATTN_WIDTH), lambda i: (i, 0)),
            pl.BlockSpec((tm, KV_WIDTH), lambda i: (i, dm.off_k // KV_WIDTH)),
            pl.BlockSpec((tm, KV_WIDTH), lambda i: (i, dm.off_v // KV_WIDTH)),
            pl.BlockSpec((tm, HEAD_DIM), tab),
            pl.BlockSpec((tm, HEAD_DIM), tab),
            pl.BlockSpec((None, 1, HEAD_DIM), lambda i: (layer, 0, 0)),
            pl.BlockSpec((None, 1, HEAD_DIM), lambda i: (layer, 0, 0)),
        ],
        out_specs=(
            pl.BlockSpec((tm, ATTN_WIDTH), lambda i: (i, 0)),
            pl.BlockSpec((tm, KV_WIDTH), lambda i: (i, 0)),
            pl.BlockSpec((tm, KV_WIDTH), lambda i: (i, 0)),
            pl.BlockSpec((tm, KV_WIDTH), lambda i: (i, 0)),
        ),
        compiler_params=_params(("arbitrary",)),
        name="qk_prep",
    )(proj, proj, proj, cos, sin, q_norm, k_norm)


def _ctx_attn_kernel(q_ref, k_ref, v_ref, o_ref):
    for h in range(N_HEADS):
        kv = h // GROUP
        cols = slice(h * HEAD_DIM, (h + 1) * HEAD_DIM)
        kcols = slice(kv * HEAD_DIM, (kv + 1) * HEAD_DIM)
        s = _dot_nt(q_ref[:, cols], k_ref[:, kcols])
        m = jnp.max(s, axis=-1, keepdims=True)
        p = jnp.exp(s - m)
        l = jnp.sum(p, axis=-1, keepdims=True)
        o = _dot(p.astype(BF16), v_ref[:, kcols]) / l
        o_ref[:, cols] = o.astype(o_ref.dtype)


def _ctx_attn(dm, q, k, proj):
    t = dm.tp
    return pl.pallas_call(
        _ctx_attn_kernel,
        out_shape=jax.ShapeDtypeStruct((dm.np_, ATTN_WIDTH), BF16),
        grid=(dm.bp,),
        in_specs=[
            pl.BlockSpec((t, ATTN_WIDTH), lambda b: (b, 0)),
            pl.BlockSpec((t, KV_WIDTH), lambda b: (b, 0)),
            pl.BlockSpec((t, KV_WIDTH), lambda b: (b, dm.off_v // KV_WIDTH)),
        ],
        out_specs=pl.BlockSpec((t, ATTN_WIDTH), lambda b: (b, 0)),
        compiler_params=_params(("arbitrary",)),
        name="ctx_attn",
    )(q, k, proj)


def _lat_attn_kernel(q_ref, k_ref, v_ref, ck_ref, cv_ref, o_ref):
    ck = ck_ref[...].astype(BF16)
    cv = cv_ref[...].astype(BF16)
    for h in range(N_HEADS):
        kv = h // GROUP
        cols = slice(h * HEAD_DIM, (h + 1) * HEAD_DIM)
        kcols = slice(kv * HEAD_DIM, (kv + 1) * HEAD_DIM)
        qh = q_ref[:, cols]
        s_ctx = _dot_nt(qh, ck[:, kcols])
        s_lat = _dot_nt(qh, k_ref[:, kcols])
        m = jnp.maximum(jnp.max(s_ctx, axis=-1, keepdims=True),
                        jnp.max(s_lat, axis=-1, keepdims=True))
        p_ctx = jnp.exp(s_ctx - m)
        p_lat = jnp.exp(s_lat - m)
        l = jnp.sum(p_ctx, axis=-1, keepdims=True) + jnp.sum(p_lat, axis=-1, keepdims=True)
        o = _dot(p_ctx.astype(BF16), cv[:, kcols]) + _dot(p_lat.astype(BF16), v_ref[:, kcols])
        o_ref[:, cols] = (o / l).astype(o_ref.dtype)


def _lat_attn(dm, layer, q, k, proj, cache_k, cache_v):
    tq = ROW_TILE
    ts = dm.ts
    qb = ts // tq
    return pl.pallas_call(
        _lat_attn_kernel,
        out_shape=jax.ShapeDtypeStruct((dm.ns, ATTN_WIDTH), BF16),
        grid=(dm.bs, qb),
        in_specs=[
            pl.BlockSpec((tq, ATTN_WIDTH), lambda b, i: (dm.np_ // tq + b * qb + i, 0)),
            pl.BlockSpec((ts, KV_WIDTH), lambda b, i: (dm.np_ // ts + b, 0)),
            pl.BlockSpec((ts, KV_WIDTH), lambda b, i: (dm.np_ // ts + b, dm.off_v // KV_WIDTH)),
            pl.BlockSpec((None, None, dm.past, KV_WIDTH), lambda b, i: (b, layer, 0, 0)),
            pl.BlockSpec((None, None, dm.past, KV_WIDTH), lambda b, i: (b, layer, 0, 0)),
        ],
        out_specs=pl.BlockSpec((tq, ATTN_WIDTH), lambda b, i: (b * qb + i, 0)),
        compiler_params=_params(("arbitrary", "arbitrary")),
        name="lat_attn",
    )(q, k, proj, cache_k, cache_v)


def _local_kernel(dm, a_ref, ap_ref, an_ref, g_ref, gp_ref, gn_ref, z_ref, zp_ref, zn_ref,
                  u_ref, v_ref, cw_ref, cb_ref, clg_ref, clb_ref, slg_ref, slb_ref,
                  sw_ref, sbt_ref, conv_o, pool_o, sgu_o, ubuf, zbuf):
    t = ROW_TILE
    i = pl.program_id(0)
    pos, cnt = dm.seq_tile_pos(i, t)
    has_prev = (pos > 0).astype(F32)
    has_next = (pos < cnt - 1).astype(F32)

    def glu(a, g):
        return a[...].astype(F32) * _sigmoid(g[...].astype(F32))

    ubuf[0:HALO, :] = glu(ap_ref, gp_ref) * has_prev
    ubuf[HALO:HALO + t, :] = glu(a_ref, g_ref)
    ubuf[HALO + t:, :] = glu(an_ref, gn_ref) * has_next
    zbuf[0:HALO, :] = zp_ref[...].astype(F32) * has_prev
    zbuf[HALO:HALO + t, :] = z_ref[...].astype(F32)
    zbuf[HALO + t:, :] = zn_ref[...].astype(F32) * has_next

    half_rows = t // 2
    centre = CONV_KERNEL // 2
    for r in range(2):
        pieces = []
        for c in range(dm.cw // LANES):
            cols = slice(c * LANES, (c + 1) * LANES)
            acc = jnp.zeros((half_rows, LANES), F32)
            for k in range(CONV_KERNEL):
                start = HALO + r * half_rows + k - centre
                acc = acc + cw_ref[k:k + 1, cols] * ubuf[start:start + half_rows, cols]
            pieces.append(acc)
        y = jnp.concatenate(pieces, axis=1) + cb_ref[...]
        y = _layer_norm(y, clg_ref[...], clb_ref[...])
        conv_o[r * half_rows:(r + 1) * half_rows, :] = (y * _sigmoid(y)).astype(conv_o.dtype)

    seq_len = jnp.where(i < dm.np_ // t, dm.tp, dm.ts)
    tok = pos * t + lax.broadcasted_iota(jnp.int32, (t, LANES), 0)
    for gi, w in enumerate(POOL_WINDOWS):
        cols = slice(gi * LANES, (gi + 1) * LANES)
        acc = jnp.zeros((t, LANES), F32)
        for j in range(-(w // 2), w - w // 2):
            acc = acc + zbuf[HALO + j:HALO + j + t, cols]
        lo = jnp.maximum(tok - w // 2, 0)
        hi = jnp.minimum(tok - w // 2 + w, seq_len)
        mean = acc / (hi - lo).astype(F32)
        pool_o[:, cols] = (mean - zbuf[HALO:HALO + t, cols]).astype(pool_o.dtype)

    vn = _layer_norm(v_ref[...].astype(F32), slg_ref[...], slb_ref[...]).astype(BF16)
    for gi in range(SGU_GROUPS):
        cols = slice(gi * CHUNK, (gi + 1) * CHUNK)
        w = sw_ref[gi].astype(BF16)
        bias = sbt_ref[:, gi:gi + 1]
        for c in range(t // CHUNK):
            rows = slice(c * CHUNK, (c + 1) * CHUNK)
            mixed = _dot(w, vn[rows, cols]) + bias
            sgu_o[rows, cols] = (u_ref[rows, cols].astype(F32) * mixed).astype(sgu_o.dtype)


def _local_branches(dm, layer, proj, conv_w, conv_b, conv_ln_g, conv_ln_b,
                    sgu_ln_g, sgu_ln_b, sgu_w, sgu_bt):
    t = ROW_TILE
    cw = dm.cw
    hb = t // HALO
    last = dm.n // HALO - 1
    ca = dm.off_a // cw

    def cur(c):
        return pl.BlockSpec((t, cw), lambda i: (i, c))

    def prev(c):
        return pl.BlockSpec((HALO, cw), lambda i: (jnp.maximum(i * hb - 1, 0), c))

    def nxt(c):
        return pl.BlockSpec((HALO, cw), lambda i: (jnp.minimum((i + 1) * hb, last), c))

    def vec():
        return pl.BlockSpec((None, 1, cw), lambda i: (layer, 0, 0))

    out = jax.ShapeDtypeStruct((dm.n, cw), BF16)
    return pl.pallas_call(
        functools.partial(_local_kernel, dm),
        out_shape=(out, out, out),
        grid=(dm.n // t,),
        in_specs=[
            cur(ca), prev(ca), nxt(ca),
            cur(ca + 1), prev(ca + 1), nxt(ca + 1),
            cur(ca + 2), prev(ca + 2), nxt(ca + 2),
            cur(ca + 3), cur(ca + 4),
            pl.BlockSpec((None, CONV_KERNEL, cw), lambda i: (layer, 0, 0)),
            vec(), vec(), vec(), vec(), vec(),
            pl.BlockSpec((None, SGU_GROUPS, CHUNK, CHUNK), lambda i: (layer, 0, 0, 0)),
            pl.BlockSpec((None, CHUNK, SGU_GROUPS), lambda i: (layer, 0, 0)),
        ],
        out_specs=(pl.BlockSpec((t, cw), lambda i: (i, 0)),) * 3,
        scratch_shapes=[pltpu.VMEM((t + 2 * HALO, cw), F32), pltpu.VMEM((t + 2 * HALO, cw), F32)],
        compiler_params=_params(("arbitrary",)),
        name="local_branches",
    )(proj, proj, proj, proj, proj, proj, proj, proj, proj, proj, proj,
      conv_w, conv_b, conv_ln_g, conv_ln_b, sgu_ln_g, sgu_ln_b, sgu_w, sgu_bt)


def _merge_kernel(dm, o_ref, cv_ref, pl_ref, sg_ref, g0_ref, g1_ref, g2_ref, g3_ref,
                  wa_ref, wc_ref, wp_ref, ws_ref, ps_ref, m_ref):
    cw = dm.cw
    for c in range(dm.d // cw):
        cols = slice(c * cw, (c + 1) * cw)
        ya = _dot(o_ref[...], wa_ref[:, cols])
        yb = _dot(cv_ref[...], wc_ref[:, cols])
        yc = _dot(pl_ref[:, c * LANES:(c + 1) * LANES], wp_ref[c]) * ps_ref[:, cols]
        yd = _dot(sg_ref[...], ws_ref[:, cols])
        merged = (_sigmoid(g0_ref[:, cols].astype(F32)) * ya
                  + _sigmoid(g1_ref[:, cols].astype(F32)) * yb
                  + _sigmoid(g2_ref[:, cols].astype(F32)) * yc
                  + _sigmoid(g3_ref[:, cols].astype(F32)) * yd)
        m_ref[:, cols] = merged.astype(m_ref.dtype)


def _merge(dm, layer, attn_o, conv_h, pool_h, sgu_h, proj, wa, wc, wp, ws, pool_scale):
    tm = WIDE_ROW_TILE
    d, cw = dm.d, dm.cw
    g0 = dm.off_gates // d
    assert d // cw == POOL_GROUPS and wp.shape[-1] == cw

    def gate(b):
        return pl.BlockSpec((tm, d), lambda i: (i, g0 + b))

    return pl.pallas_call(
        functools.partial(_merge_kernel, dm),
        out_shape=jax.ShapeDtypeStruct((dm.n, d), BF16),
        grid=(dm.n // tm,),
        in_specs=[
            pl.BlockSpec((tm, ATTN_WIDTH), lambda i: (i, 0)),
            pl.BlockSpec((tm, cw), lambda i: (i, 0)),
            pl.BlockSpec((tm, cw), lambda i: (i, 0)),
            pl.BlockSpec((tm, cw), lambda i: (i, 0)),
            gate(0), gate(1), gate(2), gate(3),
            pl.BlockSpec((None, ATTN_WIDTH, d), lambda i: (layer, 0, 0)),
            pl.BlockSpec((None, cw, d), lambda i: (layer, 0, 0)),
            pl.BlockSpec((None, POOL_GROUPS, LANES, cw), lambda i: (layer, 0, 0, 0)),
            pl.BlockSpec((None, cw, d), lambda i: (layer, 0, 0)),
            pl.BlockSpec((None, 1, d), lambda i: (layer, 0, 0)),
        ],
        out_specs=pl.BlockSpec((tm, d), lambda i: (i, 0)),
        compiler_params=_params(("arbitrary",)),
        name="merge",
    )(attn_o, conv_h, pool_h, sgu_h, proj, proj, proj, proj, wa, wc, wp, ws, pool_scale)


def _out_proj_kernel(alpha, m_ref, x_ref, mod_ref, w_ref, lg_ref, lb_ref, wr_ref, br_ref,
                     x1_ref, h2_ref, idx_ref, comb_ref):
    gate1 = mod_ref[2:3, :]
    shift2 = mod_ref[3:4, :]
    scale2 = mod_ref[4:5, :]
    wr = wr_ref[...]
    wr_hi = wr.astype(BF16)
    wr_lo = (wr - wr_hi.astype(F32)).astype(BF16)

    half_rows = m_ref.shape[0] // 2
    for r in range(2):
        rows = slice(r * half_rows, (r + 1) * half_rows)
        y = _dot(m_ref[rows, :], w_ref[...])
        x1 = _layer_norm(alpha * x_ref[rows, :] + gate1 * y, lg_ref[...], lb_ref[...])
        x1_ref[rows, :] = x1
        h2 = x1 * (1.0 + scale2) + shift2
        h2_hi = h2.astype(BF16)
        packed = _pack_pairs(h2)
        for p in range(h2_ref.shape[0]):
            h2_ref[p, rows, :] = packed[:, p * GATHER_WORDS:(p + 1) * GATHER_WORDS]

        h2_lo = (h2 - h2_hi.astype(F32)).astype(BF16)
        logits = _dot(h2_hi, wr_hi) + _dot(h2_hi, wr_lo) + _dot(h2_lo, wr_hi) + br_ref[...]

        lane = lax.broadcasted_iota(jnp.int32, logits.shape, 1)
        idx_out = jnp.zeros(logits.shape, jnp.int32)
        val_out = jnp.zeros(logits.shape, F32)
        top = None
        den = jnp.zeros((logits.shape[0], 1), F32)
        for k in range(TOP_K):
            m = jnp.max(logits, axis=-1, keepdims=True)
            ik = jnp.min(jnp.where(logits == m, lane, LANES), axis=-1, keepdims=True)
            if top is None:
                top = m
            e = jnp.exp(m - top)
            den = den + e
            idx_out = jnp.where(lane == k, ik, idx_out)
            val_out = jnp.where(lane == k, e, val_out)
            logits = jnp.where(lane == ik, -jnp.inf, logits)
        idx_ref[rows, :] = idx_out
        comb_ref[rows, :] = val_out / den


def _out_proj(dm, layer, merged, x, mod, w_out, ln_g, ln_b, w_router, b_router):
    tm = WIDE_ROW_TILE
    d = dm.d
    alpha = (2 * dm.depth) ** 0.25

    def vec():
        return pl.BlockSpec((None, 1, d), lambda i: (layer, 0, 0))

    return pl.pallas_call(
        functools.partial(_out_proj_kernel, alpha),
        out_shape=(
            jax.ShapeDtypeStruct((dm.n, d), F32),
            jax.ShapeDtypeStruct((d // 2 // GATHER_WORDS, dm.n, GATHER_WORDS), jnp.uint32),
            jax.ShapeDtypeStruct((dm.n, LANES), jnp.int32),
            jax.ShapeDtypeStruct((dm.n, LANES), F32),
        ),
        grid=(dm.n // tm,),
        in_specs=[
            pl.BlockSpec((tm, d), lambda i: (i, 0)),
            pl.BlockSpec((tm, d), lambda i: (i, 0)),
            pl.BlockSpec((None, None, N_MOD, d), lambda i: (layer, dm.group_of_tile(i, tm), 0, 0)),
            pl.BlockSpec((None, d, d), lambda i: (layer, 0, 0)),
            vec(), vec(),
            pl.BlockSpec((None, d, LANES), lambda i: (layer, 0, 0)),
            pl.BlockSpec((None, 1, LANES), lambda i: (layer, 0, 0)),
        ],
        out_specs=(
            pl.BlockSpec((tm, d), lambda i: (i, 0)),
            pl.BlockSpec((d // 2 // GATHER_WORDS, tm, GATHER_WORDS), lambda i: (0, i, 0)),
            pl.BlockSpec((tm, LANES), lambda i: (i, 0)),
            pl.BlockSpec((tm, LANES), lambda i: (i, 0)),
        ),
        compiler_params=_params(("arbitrary",)),
        name="out_proj",
    )(merged, x, mod, w_out, ln_g, ln_b, w_router, b_router)


def _moe_live_chunks(nv, fn):
    pair = 2 * MOE_SUB
    n_pairs = nv // pair

    def full_pair(i, carry):
        fn(pl.multiple_of(i * pair, pair), pair, False)
        return carry

    lax.fori_loop(0, n_pairs, full_pair, 0)

    base = pl.multiple_of(n_pairs * pair, pair)
    n = nv - base

    def sub_and_tail():
        fn(base, MOE_SUB, False)
        fn(base + MOE_SUB, MOE_TAIL, True)

    pl.when(n > MOE_SUB + MOE_TAIL)(functools.partial(fn, base, pair, True))
    pl.when(jnp.logical_and(n > MOE_SUB, n <= MOE_SUB + MOE_TAIL))(sub_and_tail)
    pl.when(jnp.logical_and(n > MOE_TAIL, n <= MOE_SUB))(functools.partial(fn, base, MOE_SUB, True))
    pl.when(jnp.logical_and(n > 0, n <= MOE_TAIL))(functools.partial(fn, base, MOE_TAIL, True))


def _moe_kernel(nfa, be_ref, nv_ref, na_ref, x_ref, wg_ref, bg_ref, wu_ref, bu_ref, wd_ref, bd_ref,
                o_ref, act_ref):
    s = pl.program_id(0)
    j = pl.program_id(1)
    nv = nv_ref[s]
    tf = MOE_FF_TILE
    active = s < na_ref[0]

    @pl.when(jnp.logical_and(active, j < nfa))
    def _():
        bg = bg_ref[...]
        bu = bu_ref[...]
        col0 = pl.multiple_of(j * tf, tf)

        def chunk(base, m, has_dead_rows):
            halves = [_unpack_pairs(x_ref[p, pl.ds(base, m), :]) for p in range(x_ref.shape[0])]
            if has_dead_rows:
                live = base + lax.broadcasted_iota(jnp.int32, (m, GATHER_WORDS), 0) < nv
                halves = [(jnp.where(live, lo, 0.0), jnp.where(live, hi, 0.0)) for lo, hi in halves]
            xs = jnp.concatenate([lo for lo, _ in halves] + [hi for _, hi in halves], axis=1)
            gt = jnp.minimum(_dot(xs, wg_ref[...]) + bg, SWIGLU_LIMIT)
            up = jnp.clip(_dot(xs, wu_ref[...]) + bu, -SWIGLU_LIMIT, SWIGLU_LIMIT)
            act = gt * _sigmoid(SWIGLU_ALPHA * gt) * (up + 1.0)
            act_ref[pl.ds(base, m), pl.ds(col0, tf)] = act.astype(BF16)

        _moe_live_chunks(nv, chunk)

    @pl.when(j >= nfa)
    def _():
        bd = bd_ref[...]

        def chunk(base, m, has_dead_rows):
            del has_dead_rows
            y = _dot(act_ref[pl.ds(base, m), :].astype(F32), wd_ref[...]) + bd
            o_ref[pl.ds(base, m), :] = _pack_pairs(y)

        _moe_live_chunks(nv, chunk)

        def zero_fill(c, carry):
            rows = pl.ds(pl.multiple_of(c * MOE_TAIL, MOE_TAIL), MOE_TAIL)
            o_ref[rows, :] = jnp.zeros((MOE_TAIL, o_ref.shape[1]), o_ref.dtype)
            return carry

        lax.fori_loop((nv + MOE_TAIL - 1) // MOE_TAIL, MOE_SUPER // MOE_TAIL, zero_fill, 0)


def _moe(dm, layer, x_sorted, blk_e, blk_nv, n_act, w_gate, b_gate, w_up, b_up, w_down, b_down):
    d, ff, tf, tn, r = dm.d, dm.ff, MOE_FF_TILE, MOE_OUT_TILE, MOE_SUPER
    n_planes = x_sorted.shape[0]
    nsb = x_sorted.shape[1] // r
    nfa = ff // tf
    nfb = d // tn
    assert tn // 2 == GATHER_WORDS

    def blk(s, na):
        return jnp.minimum(s, na[0] - 1)

    def col_a(s, j, na):
        return jnp.where(s < na[0], jnp.minimum(j, nfa - 1), nfa - 1)

    def col_b(s, j, na):
        return jnp.where(s < na[0], jnp.maximum(j - nfa, 0), nfb - 1)

    grid_spec = pltpu.PrefetchScalarGridSpec(
        num_scalar_prefetch=3,
        grid=(nsb, nfa + nfb),
        in_specs=[
            pl.BlockSpec((n_planes, r, GATHER_WORDS), lambda s, j, be, nv, na: (0, blk(s, na), 0)),
            pl.BlockSpec((None, None, d, tf), lambda s, j, be, nv, na: (layer, be[s], 0, col_a(s, j, na))),
            pl.BlockSpec((None, None, 1, tf), lambda s, j, be, nv, na: (layer, be[s], 0, col_a(s, j, na))),
            pl.BlockSpec((None, None, d, tf), lambda s, j, be, nv, na: (layer, be[s], 0, col_a(s, j, na))),
            pl.BlockSpec((None, None, 1, tf), lambda s, j, be, nv, na: (layer, be[s], 0, col_a(s, j, na))),
            pl.BlockSpec((None, None, ff, tn), lambda s, j, be, nv, na: (layer, be[s], 0, col_b(s, j, na))),
            pl.BlockSpec((None, None, 1, tn), lambda s, j, be, nv, na: (layer, be[s], 0, col_b(s, j, na))),
        ],
        out_specs=pl.BlockSpec((None, r, tn // 2), lambda s, j, be, nv, na: (jnp.maximum(j - nfa, 0), s, 0)),
        scratch_shapes=[
            pltpu.VMEM((r, ff), BF16),
        ],
    )
    return pl.pallas_call(
        functools.partial(_moe_kernel, nfa),
        out_shape=jax.ShapeDtypeStruct((nfb, nsb * r, tn // 2), jnp.uint32),
        grid_spec=grid_spec,
        compiler_params=_params(("arbitrary", "arbitrary"), MOE_VMEM_LIMIT_BYTES),
        name="moe_ffn",
    )(blk_e, blk_nv, n_act, x_sorted, w_gate, b_gate, w_up, b_up, w_down, b_down)


def _route(dm, top_idx, n_super):
    e, r = dm.n_exp, MOE_SUPER
    flat_e = top_idx.reshape(-1)
    onehot = (flat_e[:, None] == jnp.arange(e, dtype=jnp.int32)[None, :]).astype(jnp.int32)
    rank = jnp.take_along_axis(jnp.cumsum(onehot, axis=0), flat_e[:, None], axis=1)[:, 0] - 1
    counts = jnp.sum(onehot, axis=0)
    nsb_e = (counts + r - 1) // r
    sb_end = jnp.cumsum(nsb_e)
    sb_start = sb_end - nsb_e
    dest = sb_start[flat_e] * r + rank
    n_act = sb_end[-1]
    sidx = jnp.arange(n_super, dtype=jnp.int32)
    blk_e = jnp.clip(jnp.searchsorted(sb_end, jnp.minimum(sidx, n_act - 1), side='right'), 0, e - 1)
    blk_e = blk_e.astype(jnp.int32)
    blk_nv = jnp.clip(counts[blk_e] - (sidx - sb_start[blk_e]) * r, 0, r)
    blk_nv = jnp.where(sidx < n_act, blk_nv, 0).astype(jnp.int32)
    return dest.astype(jnp.int32), blk_e, blk_nv, n_act.astype(jnp.int32).reshape(1)


def _combine_kernel(alpha, ctx_tiles, y_ref, comb_ref, x_ref, mod_ref, lg_ref, lb_ref, *o_refs):
    gate2 = mod_ref[5:6, :]
    comb = comb_ref[...]
    slabs = []
    for b in range(y_ref.shape[0]):
        lo_sum = None
        hi_sum = None
        for k in range(TOP_K):
            lo, hi = _unpack_pairs(y_ref[b, k])
            wk = comb[:, k:k + 1]
            lo_sum = wk * lo if lo_sum is None else lo_sum + wk * lo
            hi_sum = wk * hi if hi_sum is None else hi_sum + wk * hi
        slabs += [lo_sum, hi_sum]
    ff = jnp.concatenate(slabs, axis=1)
    out = _layer_norm(alpha * x_ref[...] + gate2 * ff, lg_ref[...], lb_ref[...])
    if ctx_tiles is None:
        o_refs[0][...] = out
    else:
        i = pl.program_id(0)

        @pl.when(i < ctx_tiles)
        def _():
            o_refs[0][...] = out

        @pl.when(i >= ctx_tiles)
        def _():
            o_refs[1][...] = out


def _combine(dm, layer, y_tok, comb, x1, mod, ln_g, ln_b, split_groups):
    tm = WIDE_ROW_TILE
    d = dm.d
    alpha = (2 * dm.depth) ** 0.25
    if split_groups:
        npt = dm.np_ // tm
        out_shape = (jax.ShapeDtypeStruct((dm.np_, d), F32), jax.ShapeDtypeStruct((dm.ns, d), F32))
        out_specs = (pl.BlockSpec((tm, d), lambda i: (jnp.minimum(i, npt - 1), 0)),
                     pl.BlockSpec((tm, d), lambda i: (jnp.maximum(i - npt, 0), 0)))
    else:
        npt = None
        out_shape = jax.ShapeDtypeStruct((dm.n, d), F32)
        out_specs = pl.BlockSpec((tm, d), lambda i: (i, 0))
    return pl.pallas_call(
        functools.partial(_combine_kernel, alpha, npt),
        out_shape=out_shape,
        grid=(dm.n // tm,),
        in_specs=[
            pl.BlockSpec((y_tok.shape[0], TOP_K, tm, y_tok.shape[3]), lambda i: (0, 0, i, 0)),
            pl.BlockSpec((tm, LANES), lambda i: (i, 0)),
            pl.BlockSpec((tm, d), lambda i: (i, 0)),
            pl.BlockSpec((None, None, N_MOD, d), lambda i: (layer, dm.group_of_tile(i, tm), 0, 0)),
            pl.BlockSpec((None, 1, d), lambda i: (layer, 0, 0)),
            pl.BlockSpec((None, 1, d), lambda i: (layer, 0, 0)),
        ],
        out_specs=out_specs,
        compiler_params=_params(("arbitrary",)),
        name="combine",
    )(y_tok, comb, x1, mod, ln_g, ln_b)


def kernel(x_prompt, x_sample, cache_k, cache_v, c, c_ctx, w_mod, b_mod, w_in, q_norm, k_norm,
           w_attn_o, conv_w, conv_b, conv_ln_g, conv_ln_b, w_conv_o, w_pool, pool_scale,
           sgu_ln_g, sgu_ln_b, sgu_w, sgu_b, w_sgu_o, w_out, ln1_g, ln1_b, ln2_g, ln2_b,
           w_router, b_router, w_gate, b_gate, w_up, b_up, w_down, b_down):
    bp, tp, d = x_prompt.shape
    bs, ts, _ = x_sample.shape
    depth, n_exp, _, ff = w_gate.shape
    past = cache_k.shape[2]
    dm = _Dims(bp, tp, bs, ts, d, past, depth, n_exp, ff)
    assert 1 + bs <= SUBLANES and n_exp <= LANES

    x = jnp.concatenate([x_prompt.reshape(dm.np_, d), x_sample.reshape(dm.ns, d)], axis=0)
    cond = jnp.concatenate([c_ctx[None, :], c, jnp.zeros((SUBLANES - 1 - bs, d), F32)], axis=0)
    mod = _modulation(cond, w_mod, b_mod).reshape(depth, SUBLANES, N_MOD, d)

    cos, sin = _rope_tables(dm, WIDE_ROW_TILE)
    ck = cache_k.reshape(bs, depth, past, KV_WIDTH)
    cv = cache_v.reshape(bs, depth, past, KV_WIDTH)

    def row(p):
        return p.reshape(depth, 1, p.shape[-1])

    wa, wc, wp, ws, wo = (w.astype(BF16) for w in (w_attn_o, w_conv_o, w_pool, w_sgu_o, w_out))
    sgu_bt = jnp.swapaxes(sgu_b, 1, 2)
    w_router_p = jnp.pad(w_router, ((0, 0), (0, 0), (0, LANES - n_exp)))
    b_router_p = jnp.pad(b_router, ((0, 0), (0, LANES - n_exp)), constant_values=-jnp.inf)
    b_router_p = b_router_p.reshape(depth, 1, LANES)
    bg, bu = b_gate.reshape(depth, n_exp, 1, ff), b_up.reshape(depth, n_exp, 1, ff)
    bd = b_down.reshape(depth, n_exp, 1, d)

    nk = dm.n * TOP_K
    n_super = nk // MOE_SUPER + n_exp
    ks, vs = [], []
    for l in range(depth):
        proj = _in_proj(dm, l, x, mod, w_in)
        q, k, kn, v32 = _qk_prep(dm, l, proj, cos, sin, row(q_norm), row(k_norm))
        ks.append(kn[:dm.np_])
        vs.append(v32[:dm.np_])
        attn_o = jnp.concatenate(
            [_ctx_attn(dm, q, k, proj), _lat_attn(dm, l, q, k, proj, ck, cv)], axis=0)
        conv_h, pool_h, sgu_h = _local_branches(
            dm, l, proj, conv_w, row(conv_b), row(conv_ln_g), row(conv_ln_b),
            row(sgu_ln_g), row(sgu_ln_b), sgu_w, sgu_bt)
        merged = _merge(dm, l, attn_o, conv_h, pool_h, sgu_h, proj, wa, wc, wp, ws, row(pool_scale))
        x1, h2, top_idx, comb = _out_proj(dm, l, merged, x, mod, wo, row(ln1_g), row(ln1_b),
                                          w_router_p, b_router_p)
        dest, blk_e, blk_nv, n_act = _route(dm, top_idx[:, :TOP_K], n_super)
        dest_km = dest.reshape(dm.n, TOP_K).T
        x_sorted = _scatter_rows(h2, dest_km, n_super * MOE_SUPER)
        y_sorted = _moe(dm, l, x_sorted, blk_e, blk_nv, n_act, w_gate, bg, w_up, bu, w_down, bd)
        y_tok = _gather_rows(y_sorted, dest_km.reshape(-1))
        y_tok = y_tok.reshape(y_sorted.shape[0], TOP_K, dm.n, GATHER_WORDS)
        x = _combine(dm, l, y_tok, comb, x1, mod, row(ln2_g), row(ln2_b), split_groups=(l == depth - 1))

    y_prompt = x[0].reshape(bp, tp, d)
    y_sample = x[1].reshape(bs, ts, d)
    state_k = jnp.stack(ks, axis=1).reshape(bp, tp, depth, N_KV_HEADS, HEAD_DIM).swapaxes(1, 2)
    state_v = jnp.stack(vs, axis=1).reshape(bp, tp, depth, N_KV_HEADS, HEAD_DIM).swapaxes(1, 2)
    return (y_prompt, y_sample, state_k, state_v)
```
